```python
import jax, jax.numpy as jnp
from jax import lax
import numpy as np

D_MODEL = 2048
BATCH = 4
SEQ = 2048
DEPTH = 1
DEC_BATCH = 128
DEC_SEQ = 1
PAST_LEN = 16384
PAGE_SIZE = 128

A_HEADS = 8
A_WIDTH = D_MODEL // 2
A_HEAD_DIM = A_WIDTH // A_HEADS
SGU_CHUNK = 128
B_HEADS = 4
B_VAL_WIDTH = D_MODEL - A_WIDTH
B_VAL_DIM = B_VAL_WIDTH // B_HEADS
B_KEY_DIM = B_VAL_DIM // 2
B_KEY_WIDTH = B_HEADS * B_KEY_DIM
GATE_RANK = 16
GATE_TAU = 16.0
GLA_CHUNK = 64
MIX_WIDTH = A_WIDTH + B_VAL_WIDTH
IN_SIZES = (2 * A_WIDTH, B_KEY_WIDTH, B_KEY_WIDTH, B_VAL_WIDTH, B_VAL_WIDTH, GATE_RANK)
IN_COLS = 2 * A_WIDTH + 2 * B_KEY_WIDTH + 2 * B_VAL_WIDTH + GATE_RANK
IN_SPLITS = [2 * A_WIDTH, 2 * A_WIDTH + B_KEY_WIDTH, 2 * A_WIDTH + 2 * B_KEY_WIDTH,
             2 * A_WIDTH + 2 * B_KEY_WIDTH + B_VAL_WIDTH, 2 * A_WIDTH + 2 * B_KEY_WIDTH + 2 * B_VAL_WIDTH]
MEM_LEN = 256
X_HEADS = 4
X_HEAD_DIM = D_MODEL // X_HEADS
N_KEYS = 128
N_EXPERTS = N_KEYS * N_KEYS
PEER_HEADS = 8
PEER_QDIM = 256
PEER_HALF = PEER_QDIM // 2
PEER_TOPK = 16
PEER_BLOCK = 256
EPS = 1e-6

kernel_name = "hybrid_sgu_gla_peer_memxattn_step"


def rmsnorm(x, g):
    xf = x.astype(jnp.float32)
    y = xf * lax.rsqrt(jnp.mean(xf * xf, axis=-1, keepdims=True) + EPS)
    return (y * g.astype(jnp.float32)).astype(x.dtype)


def spatial_gate(u, v, w_s, b_s):
    bsz, L, H, P = v.shape
    c = min(L, SGU_CHUNK)
    pad = (-L) % c
    n = (L + pad) // c
    vc = jnp.pad(v, ((0, 0), (0, pad), (0, 0), (0, 0))).reshape(bsz, n, c, H, P)
    causal = jnp.tril(jnp.ones((c, c), dtype=bool))
    w = jnp.where(causal, w_s[:, :c, :c], 0).astype(v.dtype)
    bias = b_s[:, :c].T.astype(v.dtype)[None, None, :, :, None]
    mixed = jnp.einsum('hts,bnshp->bnthp', w, vc) + bias
    mixed = mixed.reshape(bsz, n * c, H, P)[:, :L]
    return u * mixed


def gla_chunk(state, inp):
    q, k, v, lg = inp
    C = q.shape[2]
    G = jnp.cumsum(lg, axis=2)
    causal = jnp.tril(jnp.ones((C, C), dtype=bool))[:, :, None]
    diff = G[:, :, :, None, :] - G[:, :, None, :, :]
    decay = jnp.exp(jnp.where(causal, diff, -jnp.inf))
    scores = jnp.einsum('bhtd,bhsd,bhtsd->bhts', q, k, decay)
    o = jnp.einsum('bhts,bhsv->bhtv', scores, v) + jnp.einsum('bhtd,bhdv->bhtv', q * jnp.exp(G), state)
    g_last = G[:, :, -1:, :]
    k_dec = k * jnp.exp(g_last - G)
    new_state = jnp.exp(g_last[:, :, 0, :])[..., None] * state + jnp.einsum('bhsd,bhsv->bhdv', k_dec, v)
    return new_state, o


def gla_scan(state, q, k, v, lg):
    bsz, L, H, _ = q.shape
    c = min(L, GLA_CHUNK)
    pad = (-L) % c
    n = (L + pad) // c

    def chunks(a):
        a = jnp.pad(a.astype(jnp.float32), ((0, 0), (0, pad), (0, 0), (0, 0)))
        return a.reshape(bsz, n, c, H, a.shape[-1]).transpose(1, 0, 3, 2, 4)

    new_state, o = lax.scan(gla_chunk, state.astype(jnp.float32), (chunks(q), chunks(k), chunks(v), chunks(lg)))
    o = o.transpose(1, 0, 3, 2, 4).reshape(bsz, n * c, H, -1)[:, :L]
    return new_state, o


def head_group_mixer(h, gla_state, w_in, w_alpha, b_alpha, w_s, b_s, g_sgu, g_gla, w_out):
    bsz, L, _ = h.shape
    z = h @ w_in
    za, zq, zk, zv, zr, zg = jnp.split(z, IN_SPLITS, axis=-1)
    za = jax.nn.gelu(za)
    u, va = jnp.split(za, 2, axis=-1)
    u = u.reshape(bsz, L, A_HEADS, A_HEAD_DIM)
    va = rmsnorm(va.reshape(bsz, L, A_HEADS, A_HEAD_DIM), g_sgu)
    a_out = spatial_gate(u, va, w_s, b_s).reshape(bsz, L, A_WIDTH)
    start = ((L - 1) // SGU_CHUNK) * SGU_CHUNK
    chunk_v = va[:, start:]
    q = zq.reshape(bsz, L, B_HEADS, B_KEY_DIM) * (B_KEY_DIM ** -0.5)
    k = zk.reshape(bsz, L, B_HEADS, B_KEY_DIM)
    vb = zv.reshape(bsz, L, B_HEADS, B_VAL_DIM)
    lg = jax.nn.log_sigmoid((zg @ w_alpha + b_alpha).astype(jnp.float32)) / GATE_TAU
    lg = lg.reshape(bsz, L, B_HEADS, B_KEY_DIM)
    new_state, o = gla_scan(gla_state, q, k, vb, lg)
    o = rmsnorm(o, g_gla).astype(h.dtype).reshape(bsz, L, B_VAL_WIDTH)
    b_out = jax.nn.silu(zr) * o
    y = jnp.concatenate([a_out.astype(h.dtype), b_out], axis=-1) @ w_out
    return y, new_state.astype(gla_state.dtype), chunk_v


def mem_kv(mem, g_mem, w_xk, w_xv):
    bsz = mem.shape[0]
    m = rmsnorm(mem, g_mem)
    k = (m @ w_xk).reshape(bsz, MEM_LEN, X_HEADS, X_HEAD_DIM)
    v = (m @ w_xv).reshape(bsz, MEM_LEN, X_HEADS, X_HEAD_DIM)
    return k, v


def mem_attend(h, mem_k, mem_v, w_xq, w_xo):
    bsz, L, _ = h.shape
    q = (h @ w_xq).reshape(bsz, L, X_HEADS, X_HEAD_DIM)
    s = jnp.einsum('blhd,bmhd->bhlm', q, mem_k).astype(jnp.float32) * (X_HEAD_DIM ** -0.5)
    p = jax.nn.softmax(s, axis=-1).astype(mem_v.dtype)
    o = jnp.einsum('bhlm,bmhd->blhd', p, mem_v).reshape(bsz, L, D_MODEL)
    return o @ w_xo


def peer(h, w_pq, sub_keys, u_emb, v_emb):
    bsz, L, D = h.shape
    x = h.reshape(-1, D)
    T = x.shape[0]
    q = (x @ w_pq).reshape(T, PEER_HEADS, 2, PEER_HALF)
    s = jnp.einsum('thcd,hcnd->thcn', q, sub_keys).astype(jnp.float32)
    sv, si = lax.top_k(s, PEER_TOPK)
    cand = (sv[:, :, 0, :, None] + sv[:, :, 1, None, :]).reshape(T, PEER_HEADS, -1)
    cidx = (si[:, :, 0, :, None] * N_KEYS + si[:, :, 1, None, :]).reshape(T, PEER_HEADS, -1)
    top_s, pos = lax.top_k(cand, PEER_TOPK)
    idx = jnp.take_along_axis(cidx, pos, axis=-1).reshape(T, -1)
    g = jax.nn.softmax(top_s, axis=-1).reshape(T, -1).astype(h.dtype)
    tb = min(PEER_BLOCK, T)
    pad = (-T) % tb
    xb = jnp.pad(x, ((0, pad), (0, 0))).reshape(-1, tb, D)
    ib = jnp.pad(idx, ((0, pad), (0, 0))).reshape(-1, tb, idx.shape[-1])
    gb = jnp.pad(g, ((0, pad), (0, 0))).reshape(-1, tb, g.shape[-1])

    def block(args):
        xt, it, gt = args
        ue = jnp.take(u_emb, it, axis=0)
        ve = jnp.take(v_emb, it, axis=0)
        act = jax.nn.gelu(jnp.einsum('td,ted->te', xt, ue)) * gt
        return jnp.einsum('te,ted->td', act, ve)

    out = lax.map(block, (xb, ib, gb)).reshape(-1, D)[:T]
    return out.reshape(bsz, L, D)


def trunk_layer(x, gla_state, mem_k, mem_v, g_mix, w_in, w_alpha, b_alpha, w_s, b_s, g_sgu, g_gla, w_out,
                g_xq, w_xq, w_xo, g_ffn, w_pq, sub_keys, u_emb, v_emb):
    y, new_state, chunk_v = head_group_mixer(rmsnorm(x, g_mix), gla_state, w_in, w_alpha, b_alpha,
                                             w_s, b_s, g_sgu, g_gla, w_out)
    x = x + y
    x = x + mem_attend(rmsnorm(x, g_xq), mem_k, mem_v, w_xq, w_xo)
    x = x + peer(rmsnorm(x, g_ffn), w_pq, sub_keys, u_emb, v_emb)
    return x, new_state, chunk_v


def setup_inputs(seed: int = 0) -> dict:
    key = jax.random.key(seed)
    ks = jax.random.split(key, 32)

    def nrm(k, shape, scale):
        return jax.random.normal(k, shape, dtype=jnp.float32) * scale

    def gain(k, shape):
        return 1.0 + nrm(k, shape, 0.05)

    L = DEPTH
    return {
        "x_prompt": nrm(ks[0], (BATCH, SEQ, D_MODEL), 1.0),
        "x_sample": nrm(ks[1], (DEC_BATCH, DEC_SEQ, D_MODEL), 1.0),
        "mem_prompt": nrm(ks[2], (BATCH, MEM_LEN, D_MODEL), 1.0),
        "cache_mem_k": nrm(ks[3], (L, DEC_BATCH, MEM_LEN, X_HEADS, X_HEAD_DIM), 1.0),
        "cache_mem_v": nrm(ks[4], (L, DEC_BATCH, MEM_LEN, X_HEADS, X_HEAD_DIM), 1.0),
        "state_gla": nrm(ks[5], (L, DEC_BATCH, B_HEADS, B_KEY_DIM, B_VAL_DIM), 0.1),
        "g_mix": gain(ks[6], (L, D_MODEL)),
        "w_in": nrm(ks[7], (L, D_MODEL, IN_COLS), D_MODEL ** -0.5),
        "w_alpha": nrm(ks[8], (L, GATE_RANK, B_KEY_WIDTH), GATE_RANK ** -0.5),
        "b_alpha": nrm(ks[9], (L, B_KEY_WIDTH), 0.1),
        "w_s": nrm(ks[10], (L, A_HEADS, SGU_CHUNK, SGU_CHUNK), SGU_CHUNK ** -0.5),
        "b_s": gain(ks[11], (L, A_HEADS, SGU_CHUNK)),
        "g_sgu": gain(ks[12], (L, A_HEADS, A_HEAD_DIM)),
        "g_gla": gain(ks[13], (L, B_HEADS, B_VAL_DIM)),
        "w_out": nrm(ks[14], (L, MIX_WIDTH, D_MODEL), MIX_WIDTH ** -0.5),
        "g_mem": gain(ks[15], (L, D_MODEL)),
        "w_xk": nrm(ks[16], (L, D_MODEL, D_MODEL), D_MODEL ** -0.5),
        "w_xv": nrm(ks[17], (L, D_MODEL, D_MODEL), D_MODEL ** -0.5),
        "g_xq": gain(ks[18], (L, D_MODEL)),
        "w_xq": nrm(ks[19], (L, D_MODEL, D_MODEL), D_MODEL ** -0.5),
        "w_xo": nrm(ks[20], (L, D_MODEL, D_MODEL), D_MODEL ** -0.5),
        "g_ffn": gain(ks[21], (L, D_MODEL)),
        "w_pq": nrm(ks[22], (L, D_MODEL, PEER_HEADS * PEER_QDIM), D_MODEL ** -0.5),
        "sub_keys": nrm(ks[23], (L, PEER_HEADS, 2, N_KEYS, PEER_HALF), PEER_HALF ** -0.5),
        "u_emb": nrm(ks[24], (L, N_EXPERTS, D_MODEL), D_MODEL ** -0.5),
        "v_emb": nrm(ks[25], (L, N_EXPERTS, D_MODEL), 0.5),
        "g_final": gain(ks[26], (D_MODEL,)),
    }


def reference(x_prompt, x_sample, mem_prompt, cache_mem_k, cache_mem_v, state_gla, g_mix, w_in, w_alpha, b_alpha,
              w_s, b_s, g_sgu, g_gla, w_out, g_mem, w_xk, w_xv, g_xq, w_xq, w_xo, g_ffn, w_pq, sub_keys,
              u_emb, v_emb, g_final):
    xp = x_prompt
    xs = x_sample
    mk_list, mv_list, stp_list, sts_list, cvp_list, cvs_list = [], [], [], [], [], []
    for l in range(DEPTH):
        lp = (g_mix[l], w_in[l], w_alpha[l], b_alpha[l], w_s[l], b_s[l], g_sgu[l], g_gla[l], w_out[l],
              g_xq[l], w_xq[l], w_xo[l], g_ffn[l], w_pq[l], sub_keys[l], u_emb[l], v_emb[l])
        mk, mv = mem_kv(mem_prompt, g_mem[l], w_xk[l], w_xv[l])
        init_state = jnp.zeros((xp.shape[0], B_HEADS, B_KEY_DIM, B_VAL_DIM), dtype=xp.dtype)
        xp, st_p, cv_p = trunk_layer(xp, init_state, mk, mv, *lp)
        xs, st_s, cv_s = trunk_layer(xs, state_gla[l], cache_mem_k[l], cache_mem_v[l], *lp)
        mk_list.append(mk)
        mv_list.append(mv)
        stp_list.append(st_p)
        sts_list.append(st_s)
        cvp_list.append(cv_p)
        cvs_list.append(cv_s)
    y_prompt = rmsnorm(xp, g_final)
    y_sample = rmsnorm(xs, g_final)
    mem_k_prompt = jnp.stack(mk_list)
    mem_v_prompt = jnp.stack(mv_list)
    state_gla_prompt = jnp.stack(stp_list)
    state_gla_sample = jnp.stack(sts_list)
    chunk_v_prompt = jnp.stack(cvp_list)
    chunk_v_sample = jnp.stack(cvs_list)
    return (y_prompt, y_sample, mem_k_prompt, mem_v_prompt, state_gla_prompt, state_gla_sample, chunk_v_prompt, chunk_v_sample)
```

```python
import functools

import jax
import jax.numpy as jnp
from jax import lax
from jax.experimental import pallas as pl
from jax.experimental.pallas import tpu as pltpu

F32 = jnp.float32
BF16 = jnp.bfloat16

D_MODEL = 2048
BATCH = 4
SEQ = 2048
DEC_BATCH = 128
T_PROMPT = BATCH * SEQ
T_SAMPLE = DEC_BATCH
T_ALL = T_PROMPT + T_SAMPLE

A_HEADS = 8
A_WIDTH = 1024
SGU_CHUNK = 128
B_HEADS = 4
B_KEY_DIM = 128
B_VAL_DIM = 256
B_KEY_WIDTH = 512
B_VAL_WIDTH = 1024
GATE_RANK = 16
GATE_TAU = 16.0
GLA_CHUNK = 64
GLA_SUB = 16
Z_COLS = 2 * A_WIDTH + 2 * B_KEY_WIDTH + 2 * B_VAL_WIDTH
MEM_LEN = 256
X_HEADS = 4
X_HEAD_DIM = 512
N_KEYS = 128
N_EXPERTS = N_KEYS * N_KEYS
PEER_HEADS = 8
PEER_TOPK = 16
EPS = 1e-6

LANES = 128
TOKEN_TILE = 640
EXPERT_TILE = 1024
VMEM_LIMIT = 56 * 1024 * 1024


def _params(*sem):
    return pltpu.CompilerParams(dimension_semantics=sem, vmem_limit_bytes=VMEM_LIMIT)


def _gelu(x):
    return jax.nn.gelu(x)


def _log_sigmoid(x):
    return jnp.minimum(x, 0.0) - jnp.log1p(jnp.exp(-jnp.abs(x)))


def _row_to_col(row, eye):
    return jnp.sum(jnp.where(eye, row, 0.0), axis=-1, keepdims=True)


def _eye(n):
    return lax.broadcasted_iota(jnp.int32, (n, n), 0) == lax.broadcasted_iota(jnp.int32, (n, n), 1)


def _nmm_body(*refs, has_aux, emit_xn):
    x_ref, g_ref, w_ref = refs[:3]
    rest = list(refs[3:])
    w2_ref = rest.pop(0) if has_aux else None
    o_ref = rest.pop(0)
    o2_ref = rest.pop(0) if has_aux else None
    xo_ref = rest.pop(0) if emit_xn else None
    xn_ref = rest.pop(0)

    @pl.when(pl.program_id(1) == 0)
    def _():
        x = x_ref[...]
        ms = jnp.mean(x * x, axis=-1, keepdims=True)
        xn = ((x * lax.rsqrt(ms + EPS)) * g_ref[...]).astype(BF16)
        xn_ref[...] = xn
        if has_aux:
            o2_ref[...] = jnp.dot(xn, w2_ref[...], preferred_element_type=F32)
        if emit_xn:
            xo_ref[...] = xn

    o_ref[...] = jnp.dot(xn_ref[...], w_ref[...], preferred_element_type=F32)


def _nmm(x, g, w, *, tm, tn, w_aux=None, emit_xn=False):
    m, k = x.shape
    n = w.shape[1]
    in_specs = [pl.BlockSpec((tm, k), lambda i, j: (i, 0)),
                pl.BlockSpec((1, k), lambda i, j: (0, 0)),
                pl.BlockSpec((k, tn), lambda i, j: (0, j))]
    args = [x, g.reshape(1, k), w]
    out_shape = [jax.ShapeDtypeStruct((m, n), F32)]
    out_specs = [pl.BlockSpec((tm, tn), lambda i, j: (i, j))]
    if w_aux is not None:
        in_specs.append(pl.BlockSpec((k, LANES), lambda i, j: (0, 0)))
        args.append(w_aux)
        out_shape.append(jax.ShapeDtypeStruct((m, LANES), F32))
        out_specs.append(pl.BlockSpec((tm, LANES), lambda i, j: (i, 0)))
    if emit_xn:
        out_shape.append(jax.ShapeDtypeStruct((m, k), BF16))
        out_specs.append(pl.BlockSpec((tm, k), lambda i, j: (i, 0)))
    return pl.pallas_call(
        functools.partial(_nmm_body, has_aux=w_aux is not None, emit_xn=emit_xn),
        grid=(m // tm, n // tn),
        in_specs=in_specs, out_specs=out_specs, out_shape=out_shape,
        scratch_shapes=[pltpu.VMEM((tm, k), BF16)],
        compiler_params=_params("parallel", "arbitrary"),
        name="nmm",
    )(*args)


def _mmres_body(x_ref, w_ref, r_ref, o_ref):
    o_ref[...] = r_ref[...] + jnp.dot(x_ref[...].astype(BF16), w_ref[...], preferred_element_type=F32)


def _mmres(x, w, res, *, tm, tn):
    m, k = x.shape
    n = w.shape[1]
    return pl.pallas_call(
        _mmres_body,
        grid=(m // tm, n // tn),
        in_specs=[pl.BlockSpec((tm, k), lambda i, j: (i, 0)),
                  pl.BlockSpec((k, tn), lambda i, j: (0, j)),
                  pl.BlockSpec((tm, tn), lambda i, j: (i, j))],
        out_specs=pl.BlockSpec((tm, tn), lambda i, j: (i, j)),
        out_shape=jax.ShapeDtypeStruct((m, n), F32),
        compiler_params=_params("parallel", "parallel"),
        name="mmres",
    )(x, w, res)


def _sgu_body(u_ref, v_ref, ws_ref, bs_ref, gs_ref, o_ref, cv_ref):
    for h in range(A_HEADS):
        sl = slice(h * LANES, (h + 1) * LANES)
        u = _gelu(u_ref[:, sl])
        v = _gelu(v_ref[:, sl])
        ms = jnp.mean(v * v, axis=-1, keepdims=True)
        va = (v * lax.rsqrt(ms + EPS)) * gs_ref[:, sl]
        mixed = jnp.dot(ws_ref[h], va.astype(BF16), preferred_element_type=F32) + bs_ref[:, h:h + 1]
        o_ref[:, sl] = u * mixed
        cv_ref[0, :, sl] = va


def _sgu_prompt(z, ws_tril, bs_t, gs):
    n_chunks = SEQ // SGU_CHUNK
    return pl.pallas_call(
        _sgu_body,
        grid=(T_PROMPT // SGU_CHUNK,),
        in_specs=[pl.BlockSpec((SGU_CHUNK, A_WIDTH), lambda i: (i, 0)),
                  pl.BlockSpec((SGU_CHUNK, A_WIDTH), lambda i: (i, 1)),
                  pl.BlockSpec((A_HEADS, SGU_CHUNK, SGU_CHUNK), lambda i: (0, 0, 0)),
                  pl.BlockSpec((SGU_CHUNK, A_HEADS), lambda i: (0, 0)),
                  pl.BlockSpec((1, A_WIDTH), lambda i: (0, 0))],
        out_specs=[pl.BlockSpec((SGU_CHUNK, A_WIDTH), lambda i: (i, 0)),
                   pl.BlockSpec((1, SGU_CHUNK, A_WIDTH), lambda i: (i // n_chunks, 0, 0))],
        out_shape=[jax.ShapeDtypeStruct((T_ALL, D_MODEL), F32),
                   jax.ShapeDtypeStruct((BATCH, SGU_CHUNK, A_WIDTH), F32)],
        compiler_params=_params("arbitrary"),
        name="sgu_prompt",
    )(z, z, ws_tril, bs_t, gs)


def _gla_body(q_ref, k_ref, v_ref, r_ref, zg_ref, wa_ref, ba_ref, gg_ref, mix_ref, o_ref, st_ref, s_ref):
    del mix_ref
    c = pl.program_id(1)

    @pl.when(c == 0)
    def _():
        s_ref[...] = jnp.zeros_like(s_ref)

    x = jnp.dot(zg_ref[...].astype(BF16), wa_ref[...], preferred_element_type=F32) + ba_ref[...]
    lg = _log_sigmoid(x) * (1.0 / GATE_TAU)
    tri = (lax.broadcasted_iota(jnp.int32, (GLA_CHUNK, GLA_CHUNK), 1)
           <= lax.broadcasted_iota(jnp.int32, (GLA_CHUNK, GLA_CHUNK), 0)).astype(BF16)
    hi = lg.astype(BF16)
    r1 = lg - hi.astype(F32)
    mid = r1.astype(BF16)
    lo = (r1 - mid.astype(F32)).astype(BF16)
    gcum = (jnp.dot(tri, hi, preferred_element_type=F32) + jnp.dot(tri, mid, preferred_element_type=F32)
            + jnp.dot(tri, lo, preferred_element_type=F32))
    eye = _eye(B_KEY_DIM)
    t_iota = lax.broadcasted_iota(jnp.int32, (GLA_SUB, B_KEY_DIM), 0)

    for h in range(B_HEADS):
        dk = slice(h * B_KEY_DIM, (h + 1) * B_KEY_DIM)
        dv = slice(h * B_VAL_DIM, (h + 1) * B_VAL_DIM)
        q = q_ref[:, dk] * (B_KEY_DIM ** -0.5)
        k = k_ref[:, dk]
        v = v_ref[:, dv]
        g = gcum[:, dk]
        vb = v.astype(BF16)
        state = s_ref[h]
        o_inter = jnp.dot((q * jnp.exp(g)).astype(BF16), state.astype(BF16), preferred_element_type=F32)
        parts = []
        for i in range(GLA_CHUNK // GLA_SUB):
            lo_r, hi_r = GLA_SUB * i, GLA_SUB * (i + 1)
            qi, ki, vi, gi = q[lo_r:hi_r], k[lo_r:hi_r], v[lo_r:hi_r], g[lo_r:hi_r]
            acc = jnp.zeros((GLA_SUB, B_VAL_DIM), F32)
            for s in range(GLA_SUB):
                dec = jnp.exp(jnp.where(t_iota >= s, gi - gi[s:s + 1, :], -jnp.inf))
                col = jnp.sum(qi * ki[s:s + 1, :] * dec, axis=-1, keepdims=True)
                acc = acc + col * vi[s:s + 1, :]
            if i > 0:
                gref = g[lo_r - 1:lo_r, :]
                qd = (qi * jnp.exp(gi - gref)).astype(BF16)
                kd = (k[:lo_r] * jnp.exp(gref - g[:lo_r])).astype(BF16)
                sc = lax.dot_general(qd, kd, (((1,), (1,)), ((), ())), preferred_element_type=F32)
                acc = acc + jnp.dot(sc.astype(BF16), vb[:lo_r], preferred_element_type=F32)
            parts.append(acc)
        o = o_inter + jnp.concatenate(parts, axis=0)

        g_last = g[GLA_CHUNK - 1:GLA_CHUNK, :]
        kdec = k * jnp.exp(g_last - g)
        upd = jnp.dot(kdec.T.astype(BF16), vb, preferred_element_type=F32)
        s_ref[h] = _row_to_col(jnp.exp(g_last), eye) * state + upd

        ms = jnp.mean(o * o, axis=-1, keepdims=True)
        on = (o * lax.rsqrt(ms + EPS)) * gg_ref[:, dv]
        r = r_ref[:, dv]
        o_ref[:, dv] = (r * jax.nn.sigmoid(r)) * on

    @pl.when(c == pl.num_programs(1) - 1)
    def _():
        st_ref[0] = s_ref[...]


def _gla_prompt(z, zg, wa_pad, ba, gg, mix):
    n_chunks = SEQ // GLA_CHUNK
    row = lambda b, c: b * n_chunks + c
    return pl.pallas_call(
        _gla_body,
        grid=(BATCH, n_chunks),
        in_specs=[pl.BlockSpec((GLA_CHUNK, B_KEY_WIDTH), lambda b, c: (row(b, c), 4)),
                  pl.BlockSpec((GLA_CHUNK, B_KEY_WIDTH), lambda b, c: (row(b, c), 5)),
                  pl.BlockSpec((GLA_CHUNK, B_VAL_WIDTH), lambda b, c: (row(b, c), 3)),
                  pl.BlockSpec((GLA_CHUNK, B_VAL_WIDTH), lambda b, c: (row(b, c), 4)),
                  pl.BlockSpec((GLA_CHUNK, LANES), lambda b, c: (row(b, c), 0)),
                  pl.BlockSpec((LANES, B_KEY_WIDTH), lambda b, c: (0, 0)),
                  pl.BlockSpec((1, B_KEY_WIDTH), lambda b, c: (0, 0)),
                  pl.BlockSpec((1, B_VAL_WIDTH), lambda b, c: (0, 0)),
                  pl.BlockSpec(memory_space=pl.ANY)],
        out_specs=[pl.BlockSpec((GLA_CHUNK, B_VAL_WIDTH), lambda b, c: (row(b, c), 1)),
                   pl.BlockSpec((1, B_HEADS, B_KEY_DIM, B_VAL_DIM), lambda b, c: (b, 0, 0, 0))],
        out_shape=[jax.ShapeDtypeStruct((T_ALL, D_MODEL), F32),
                   jax.ShapeDtypeStruct((BATCH, B_HEADS, B_KEY_DIM, B_VAL_DIM), F32)],
        scratch_shapes=[pltpu.VMEM((B_HEADS, B_KEY_DIM, B_VAL_DIM), F32)],
        input_output_aliases={8: 0},
        compiler_params=_params("arbitrary", "arbitrary"),
        name="gla_prompt",
    )(z, z, z, z, zg, wa_pad, ba, gg, mix)


SAMPLE_ROWS = 8


def _smix_body(z_ref, zg_ref, st_ref, wa_ref, ba_ref, gs_ref, gg_ref, w00_ref, b00_ref, mix_ref,
               o_ref, cv_ref, so_ref):
    del mix_ref
    u = _gelu(z_ref[:, 0:A_WIDTH])
    v = _gelu(z_ref[:, A_WIDTH:2 * A_WIDTH])
    for h in range(A_HEADS):
        sl = slice(h * LANES, (h + 1) * LANES)
        vh = v[:, sl]
        ms = jnp.mean(vh * vh, axis=-1, keepdims=True)
        va = (vh * lax.rsqrt(ms + EPS)) * gs_ref[:, sl]
        cv_ref[:, sl] = va
        o_ref[:, sl] = u[:, sl] * (w00_ref[:, sl] * va + b00_ref[:, sl])

    x = jnp.dot(zg_ref[...].astype(BF16), wa_ref[...], preferred_element_type=F32) + ba_ref[...]
    a = jnp.exp(_log_sigmoid(x) * (1.0 / GATE_TAU))
    q0, k0, v0, r0 = 2 * A_WIDTH, 2 * A_WIDTH + B_KEY_WIDTH, 2 * A_WIDTH + 2 * B_KEY_WIDTH, Z_COLS - B_VAL_WIDTH
    eye = _eye(B_KEY_DIM)
    for b in range(SAMPLE_ROWS):
        for h in range(B_HEADS):
            dk = slice(h * B_KEY_DIM, (h + 1) * B_KEY_DIM)
            a_col = _row_to_col(a[b:b + 1, dk], eye)
            k_col = _row_to_col(z_ref[b:b + 1, k0 + h * B_KEY_DIM:k0 + (h + 1) * B_KEY_DIM], eye)
            q_col = _row_to_col(z_ref[b:b + 1, q0 + h * B_KEY_DIM:q0 + (h + 1) * B_KEY_DIM], eye) * (B_KEY_DIM ** -0.5)
            vrow = z_ref[b:b + 1, v0 + h * B_VAL_DIM:v0 + (h + 1) * B_VAL_DIM]
            s_new = a_col * st_ref[b, h] + k_col * vrow
            so_ref[b, h] = s_new
            o = jnp.sum(q_col * s_new, axis=0, keepdims=True)
            ms = jnp.mean(o * o, axis=-1, keepdims=True)
            on = (o * lax.rsqrt(ms + EPS)) * gg_ref[:, h * B_VAL_DIM:(h + 1) * B_VAL_DIM]
            r = z_ref[b:b + 1, r0 + h * B_VAL_DIM:r0 + (h + 1) * B_VAL_DIM]
            o_ref[b:b + 1, A_WIDTH + h * B_VAL_DIM:A_WIDTH + (h + 1) * B_VAL_DIM] = (r * jax.nn.sigmoid(r)) * on


def _sample_mixers(z, zg, state, wa_pad, ba, gs, gg, w00, b00, mix):
    first = T_PROMPT // SAMPLE_ROWS
    return pl.pallas_call(
        _smix_body,
        grid=(T_SAMPLE // SAMPLE_ROWS,),
        in_specs=[pl.BlockSpec((SAMPLE_ROWS, Z_COLS), lambda i: (first + i, 0)),
                  pl.BlockSpec((SAMPLE_ROWS, LANES), lambda i: (first + i, 0)),
                  pl.BlockSpec((SAMPLE_ROWS, B_HEADS, B_KEY_DIM, B_VAL_DIM), lambda i: (i, 0, 0, 0)),
                  pl.BlockSpec((LANES, B_KEY_WIDTH), lambda i: (0, 0)),
                  pl.BlockSpec((1, B_KEY_WIDTH), lambda i: (0, 0)),
                  pl.BlockSpec((1, A_WIDTH), lambda i: (0, 0)),
                  pl.BlockSpec((1, B_VAL_WIDTH), lambda i: (0, 0)),
                  pl.BlockSpec((1, A_WIDTH), lambda i: (0, 0)),
                  pl.BlockSpec((1, A_WIDTH), lambda i: (0, 0)),
                  pl.BlockSpec(memory_space=pl.ANY)],
        out_specs=[pl.BlockSpec((SAMPLE_ROWS, D_MODEL), lambda i: (first + i, 0)),
                   pl.BlockSpec((SAMPLE_ROWS, A_WIDTH), lambda i: (i, 0)),
                   pl.BlockSpec((SAMPLE_ROWS, B_HEADS, B_KEY_DIM, B_VAL_DIM), lambda i: (i, 0, 0, 0))],
        out_shape=[jax.ShapeDtypeStruct((T_ALL, D_MODEL), F32),
                   jax.ShapeDtypeStruct((T_SAMPLE, A_WIDTH), F32),
                   jax.ShapeDtypeStruct((T_SAMPLE, B_HEADS, B_KEY_DIM, B_VAL_DIM), F32)],
        input_output_aliases={9: 0},
        compiler_params=_params("parallel"),
        name="sample_mixers",
    )(z, zg, state, wa_pad, ba, gs, gg, w00, b00, mix)


ATT_TILE = 512


def _attn_prompt_body(q_ref, k_ref, v_ref, o_ref):
    for h in range(X_HEADS):
        sl = slice(h * X_HEAD_DIM, (h + 1) * X_HEAD_DIM)
        q = q_ref[:, sl].astype(BF16)
        k = k_ref[0, :, sl].astype(BF16)
        s = lax.dot_general(q, k, (((1,), (1,)), ((), ())), preferred_element_type=F32) * (X_HEAD_DIM ** -0.5)
        s = s - jnp.max(s, axis=-1, keepdims=True)
        e = jnp.exp(s)
        p = e / jnp.sum(e, axis=-1, keepdims=True)
        o_ref[:, sl] = jnp.dot(p.astype(BF16), v_ref[0, :, sl].astype(BF16), preferred_element_type=F32)


def _attn_prompt(qx, mk, mv):
    tiles = SEQ // ATT_TILE
    return pl.pallas_call(
        _attn_prompt_body,
        grid=(BATCH, tiles),
        in_specs=[pl.BlockSpec((ATT_TILE, D_MODEL), lambda b, t: (b * tiles + t, 0)),
                  pl.BlockSpec((1, MEM_LEN, D_MODEL), lambda b, t: (b, 0, 0)),
                  pl.BlockSpec((1, MEM_LEN, D_MODEL), lambda b, t: (b, 0, 0))],
        out_specs=pl.BlockSpec((ATT_TILE, D_MODEL), lambda b, t: (b * tiles + t, 0)),
        out_shape=jax.ShapeDtypeStruct((T_ALL, D_MODEL), F32),
        compiler_params=_params("parallel", "parallel"),
        name="attn_prompt",
    )(qx, mk, mv)


def _attn_sample_body(q_ref, k_ref, v_ref, att_ref, o_ref):
    del att_ref
    r = pl.program_id(0) % SAMPLE_ROWS
    for h in range(X_HEADS):
        sl = slice(h * X_HEAD_DIM, (h + 1) * X_HEAD_DIM)
        q = q_ref[pl.ds(r, 1), sl]
        s = jnp.sum(k_ref[0, :, sl] * q, axis=-1, keepdims=True) * (X_HEAD_DIM ** -0.5)
        s = s - jnp.max(s, axis=0, keepdims=True)
        e = jnp.exp(s)
        p = e / jnp.sum(e, axis=0, keepdims=True)
        o_ref[pl.ds(r, 1), sl] = jnp.sum(p * v_ref[0, :, sl], axis=0, keepdims=True)


def _attn_sample(qx, ck, cv, att):
    first = T_PROMPT // SAMPLE_ROWS
    return pl.pallas_call(
        _attn_sample_body,
        grid=(T_SAMPLE,),
        in_specs=[pl.BlockSpec((SAMPLE_ROWS, D_MODEL), lambda i: (first + i // SAMPLE_ROWS, 0)),
                  pl.BlockSpec((1, MEM_LEN, D_MODEL), lambda i: (i, 0, 0)),
                  pl.BlockSpec((1, MEM_LEN, D_MODEL), lambda i: (i, 0, 0)),
                  pl.BlockSpec(memory_space=pl.ANY)],
        out_specs=pl.BlockSpec((SAMPLE_ROWS, D_MODEL), lambda i: (first + i // SAMPLE_ROWS, 0)),
        out_shape=jax.ShapeDtypeStruct((T_ALL, D_MODEL), F32),
        input_output_aliases={3: 0},
        compiler_params=_params("arbitrary"),
        name="attn_sample",
    )(qx, ck, cv, att)


def _top16_distinct(s):
    rows = lax.broadcasted_iota(jnp.int32, (PEER_TOPK, LANES), 0)
    vals = jnp.full((PEER_TOPK, LANES), -jnp.inf, F32)
    cnts = jnp.zeros((PEER_TOPK, LANES), F32)
    work = s
    for kk in range(PEER_TOPK):
        m = jnp.max(work, axis=0, keepdims=True)
        eq = work == m
        cnt = jnp.sum(jnp.where(eq, 1.0, 0.0), axis=0, keepdims=True)
        vals = jnp.where(rows == kk, m, vals)
        cnts = jnp.where(rows == kk, cnt, cnts)
        work = jnp.where(eq, -jnp.inf, work)
    return vals, cnts


_CAND_ROWS = tuple((a, PEER_TOPK // (a + 1)) for a in range(1, 8))


def _peer_route_body(pq_ref, keys_ref, s0_ref, s1_ref, a_ref, b_ref, th_ref):
    rows8 = lax.broadcasted_iota(jnp.int32, (8, LANES), 0)

    def head(h, carry):
        q0 = pq_ref[:, pl.ds(pl.multiple_of(h * 2 * LANES, LANES), LANES)].astype(BF16)
        q1 = pq_ref[:, pl.ds(pl.multiple_of(h * 2 * LANES + LANES, LANES), LANES)].astype(BF16)
        nt = (((1,), (1,)), ((), ()))
        s0 = lax.dot_general(keys_ref[2 * h], q0, nt, preferred_element_type=F32)
        s1 = lax.dot_general(keys_ref[2 * h + 1], q1, nt, preferred_element_type=F32)
        v0, n0 = _top16_distinct(s0)
        v1, n1 = _top16_distinct(s1)
        cs = [v0[0:1] + v1]
        ws = [n0[0:1] * n1]
        for a, nb in _CAND_ROWS:
            ok = rows8 < nb
            cs.append(jnp.where(ok, v0[a:a + 1] + v1[0:8], -jnp.inf))
            ws.append(jnp.where(ok, n0[a:a + 1] * n1[0:8], 0.0))
        cs.append(v0[8:16] + v1[0:1])
        ws.append(n0[8:16] * n1[0:1])
        cand = jnp.concatenate(cs, axis=0)
        wgt = jnp.concatenate(ws, axis=0)
        starts = [0] + [16 + 8 * i for i in range(len(_CAND_ROWS))] + [16 + 8 * len(_CAND_ROWS)]
        counts = [PEER_TOPK] + [nb for _, nb in _CAND_ROWS] + [8]
        above = jnp.zeros_like(cand)
        for st, cn in zip(starts, counts):
            for rr in range(st, st + cn):
                above = above + jnp.where(cand[rr:rr + 1] > cand, wgt[rr:rr + 1], 0.0)
        theta = jnp.min(jnp.where(above <= PEER_TOPK - 1.0, cand, jnp.inf), axis=0, keepdims=True)
        top = cand[0:1]
        zsum = jnp.sum(jnp.where(cand >= theta, wgt * jnp.exp(cand - top), 0.0), axis=0, keepdims=True)
        s0_ref[h] = s0
        s1_ref[h] = s1
        a_ref[h] = jnp.exp(s0 - v0[0:1]) / zsum
        b_ref[h] = jnp.exp(s1 - v1[0:1])
        th_ref[h] = theta
        return carry

    lax.fori_loop(0, PEER_HEADS, head, 0)


def _peer_route(pq, keys):
    big = jax.ShapeDtypeStruct((PEER_HEADS, N_KEYS, T_ALL), F32)
    big_spec = pl.BlockSpec((PEER_HEADS, N_KEYS, LANES), lambda i: (0, 0, i))
    return pl.pallas_call(
        _peer_route_body,
        grid=(T_ALL // LANES,),
        in_specs=[pl.BlockSpec((LANES, D_MODEL), lambda i: (i, 0)),
                  pl.BlockSpec((2 * PEER_HEADS, N_KEYS, LANES), lambda i: (0, 0, 0))],
        out_specs=[big_spec, big_spec, big_spec, big_spec,
                   pl.BlockSpec((PEER_HEADS, 1, LANES), lambda i: (0, 0, i))],
        out_shape=[big, big, big, big, jax.ShapeDtypeStruct((PEER_HEADS, 1, T_ALL), F32)],
        compiler_params=_params("parallel"),
        name="peer_route",
    )(pq, keys)


def _peer_dense_body(h_ref, u_ref, vt_ref, s0_ref, s1_ref, a_ref, b_ref, th_ref, o_ref, st_ref, at_ref):
    j = pl.program_id(1)
    rows_per_blk = EXPERT_TILE // N_KEYS
    lane_tiles = TOKEN_TILE // LANES

    @pl.when(j == 0)
    def _():
        o_ref[...] = jnp.zeros_like(o_ref)

    st_ref[...] = lax.dot_general(u_ref[...], h_ref[...], (((1,), (1,)), ((), ())), preferred_element_type=F32)

    rows8 = lax.broadcasted_iota(jnp.int32, (8, LANES), 0)

    def tile(n, carry):
        ii = n // lane_tiles
        l0 = pl.multiple_of((n % lane_tiles) * LANES, LANES)
        r0 = pl.multiple_of(ii * N_KEYS, N_KEYS)
        pick = rows8 == ii
        gate = jnp.zeros((N_KEYS, LANES), F32)
        for h in range(PEER_HEADS):
            s0_row = jnp.sum(jnp.where(pick, s0_ref[h, :, pl.ds(l0, LANES)], 0.0), axis=0, keepdims=True)
            a_row = jnp.sum(jnp.where(pick, a_ref[h, :, pl.ds(l0, LANES)], 0.0), axis=0, keepdims=True)
            c = s0_row + s1_ref[h, :, pl.ds(l0, LANES)]
            w = a_row * b_ref[h, :, pl.ds(l0, LANES)]
            gate = gate + jnp.where(c >= th_ref[h, :, pl.ds(l0, LANES)], w, 0.0)
        act = _gelu(st_ref[pl.ds(r0, N_KEYS), pl.ds(l0, LANES)]) * gate
        at_ref[pl.ds(r0, N_KEYS), pl.ds(l0, LANES)] = act.astype(BF16)
        return carry

    lax.fori_loop(0, rows_per_blk * lane_tiles, tile, 0)
    o_ref[...] += jnp.dot(vt_ref[...], at_ref[...], preferred_element_type=F32)


def _peer_dense(hn, u_bf, vt_bf, s0, s1, a, b, th):
    route_spec = pl.BlockSpec((PEER_HEADS, N_KEYS, TOKEN_TILE), lambda i, j: (0, 0, i))
    row_spec = pl.BlockSpec((PEER_HEADS, EXPERT_TILE // N_KEYS, TOKEN_TILE), lambda i, j: (0, j, i))
    return pl.pallas_call(
        _peer_dense_body,
        grid=(T_ALL // TOKEN_TILE, N_EXPERTS // EXPERT_TILE),
        in_specs=[pl.BlockSpec((TOKEN_TILE, D_MODEL), lambda i, j: (i, 0)),
                  pl.BlockSpec((EXPERT_TILE, D_MODEL), lambda i, j: (j, 0)),
                  pl.BlockSpec((D_MODEL, EXPERT_TILE), lambda i, j: (0, j)),
                  row_spec, route_spec, row_spec, route_spec,
                  pl.BlockSpec((PEER_HEADS, 1, TOKEN_TILE), lambda i, j: (0, 0, i))],
        out_specs=pl.BlockSpec((D_MODEL, TOKEN_TILE), lambda i, j: (0, i)),
        out_shape=jax.ShapeDtypeStruct((D_MODEL, T_ALL), F32),
        scratch_shapes=[pltpu.VMEM((EXPERT_TILE, TOKEN_TILE), F32),
                        pltpu.VMEM((EXPERT_TILE, TOKEN_TILE), BF16)],
        compiler_params=_params("parallel", "arbitrary"),
        name="peer_dense",
    )(hn, u_bf, vt_bf, s0, s1, a, b, th)


def _final_body(x_ref, pt_ref, g_ref, o_ref):
    x = x_ref[...] + pt_ref[...].T
    ms = jnp.mean(x * x, axis=-1, keepdims=True)
    o_ref[...] = (x * lax.rsqrt(ms + EPS)) * g_ref[...]


def _final(x2, peer_t, g, first_tile, n_tiles):
    return pl.pallas_call(
        _final_body,
        grid=(n_tiles,),
        in_specs=[pl.BlockSpec((LANES, D_MODEL), lambda i: (first_tile + i, 0)),
                  pl.BlockSpec((D_MODEL, LANES), lambda i: (0, first_tile + i)),
                  pl.BlockSpec((1, D_MODEL), lambda i: (0, 0))],
        out_specs=pl.BlockSpec((LANES, D_MODEL), lambda i: (i, 0)),
        out_shape=jax.ShapeDtypeStruct((n_tiles * LANES, D_MODEL), F32),
        compiler_params=_params("parallel"),
        name="final_norm",
    )(x2, peer_t, g.reshape(1, D_MODEL))


def kernel(x_prompt, x_sample, mem_prompt, cache_mem_k, cache_mem_v, state_gla, g_mix, w_in, w_alpha, b_alpha, w_s, b_s, g_sgu, g_gla, w_out, g_mem, w_xk, w_xv, g_xq, w_xq, w_xo, g_ffn, w_pq, sub_keys, u_emb, v_emb, g_final):
    w_in_main = w_in[0, :, :Z_COLS].astype(BF16)
    w_in_gate = jnp.pad(w_in[0, :, Z_COLS:], ((0, 0), (0, LANES - GATE_RANK))).astype(BF16)
    wa_pad = jnp.pad(w_alpha[0], ((0, LANES - GATE_RANK), (0, 0))).astype(BF16)
    ba = b_alpha[0].reshape(1, B_KEY_WIDTH)
    causal = jnp.tril(jnp.ones((SGU_CHUNK, SGU_CHUNK), dtype=bool))
    ws_tril = jnp.where(causal, w_s[0], 0).astype(BF16)
    bs_t = b_s[0].T
    w00 = jnp.repeat(w_s[0, :, 0, 0], LANES).reshape(1, A_WIDTH)
    b00 = jnp.repeat(b_s[0, :, 0], LANES).reshape(1, A_WIDTH)
    gs = g_sgu[0].reshape(1, A_WIDTH)
    gg = g_gla[0].reshape(1, B_VAL_WIDTH)
    keys = sub_keys[0].reshape(2 * PEER_HEADS, N_KEYS, LANES).astype(BF16)
    u_bf = u_emb[0].astype(BF16)
    vt_bf = v_emb[0].T.astype(BF16)

    x_all = jnp.concatenate([x_prompt.reshape(T_PROMPT, D_MODEL), x_sample.reshape(T_SAMPLE, D_MODEL)], axis=0)

    mem = mem_prompt.reshape(BATCH * MEM_LEN, D_MODEL)
    mk, = _nmm(mem, g_mem[0], w_xk[0].astype(BF16), tm=512, tn=1024)
    mv, = _nmm(mem, g_mem[0], w_xv[0].astype(BF16), tm=512, tn=1024)

    z, zg = _nmm(x_all, g_mix[0], w_in_main, tm=TOKEN_TILE, tn=1024, w_aux=w_in_gate)
    mix, cv_p = _sgu_prompt(z, ws_tril, bs_t, gs)
    mix, st_p = _gla_prompt(z, zg, wa_pad, ba, gg, mix)
    mix, cv_s, st_s = _sample_mixers(z, zg, state_gla[0], wa_pad, ba, gs, gg, w00, b00, mix)
    x1 = _mmres(mix, w_out[0].astype(BF16), x_all, tm=TOKEN_TILE, tn=1024)

    qx, = _nmm(x1, g_xq[0], w_xq[0].astype(BF16), tm=TOKEN_TILE, tn=1024)
    att = _attn_prompt(qx, mk.reshape(BATCH, MEM_LEN, D_MODEL), mv.reshape(BATCH, MEM_LEN, D_MODEL))
    att = _attn_sample(qx, cache_mem_k[0].reshape(DEC_BATCH, MEM_LEN, D_MODEL),
                       cache_mem_v[0].reshape(DEC_BATCH, MEM_LEN, D_MODEL), att)
    x2 = _mmres(att, w_xo[0].astype(BF16), x1, tm=TOKEN_TILE, tn=1024)

    pq, hn = _nmm(x2, g_ffn[0], w_pq[0].astype(BF16), tm=TOKEN_TILE, tn=1024, emit_xn=True)
    s0, s1, a, b, th = _peer_route(pq, keys)
    peer_t = _peer_dense(hn, u_bf, vt_bf, s0, s1, a, b, th)

    y_p = _final(x2, peer_t, g_final, 0, T_PROMPT // LANES)
    y_s = _final(x2, peer_t, g_final, T_PROMPT // LANES, T_SAMPLE // LANES)

    return (y_p.reshape(BATCH, SEQ, D_MODEL),
            y_s.reshape(DEC_BATCH, 1, D_MODEL),
            mk.reshape(1, BATCH, MEM_LEN, X_HEADS, X_HEAD_DIM),
            mv.reshape(1, BATCH, MEM_LEN, X_HEADS, X_HEAD_DIM),
            st_p.reshape(1, BATCH, B_HEADS, B_KEY_DIM, B_VAL_DIM),
            st_s.reshape(1, DEC_BATCH, B_HEADS, B_KEY_DIM, B_VAL_DIM),
            cv_p.reshape(1, BATCH, SGU_CHUNK, A_HEADS, LANES),
            cv_s.reshape(1, DEC_BATCH, 1, A_HEADS, LANES))
```

```python
import functools

import jax
import jax.numpy as jnp
from jax import lax
from jax.experimental import pallas as pl
from jax.experimental.pallas import tpu as pltpu

F32 = jnp.float32
BF16 = jnp.bfloat16

D_MODEL = 2048
BATCH = 4
SEQ = 2048
DEC_BATCH = 128
T_PROMPT = BATCH * SEQ
T_SAMPLE = DEC_BATCH
T_ALL = T_PROMPT + T_SAMPLE

A_HEADS = 8
A_WIDTH = 1024
SGU_CHUNK = 128
B_HEADS = 4
B_KEY_DIM = 128
B_VAL_DIM = 256
B_KEY_WIDTH = 512
B_VAL_WIDTH = 1024
GATE_RANK = 16
GATE_TAU = 16.0
GLA_CHUNK = 64
GLA_SUB = 16
Z_COLS = 2 * A_WIDTH + 2 * B_KEY_WIDTH + 2 * B_VAL_WIDTH
MEM_LEN = 256
X_HEADS = 4
X_HEAD_DIM = 512
N_KEYS = 128
N_EXPERTS = N_KEYS * N_KEYS
PEER_HEADS = 8
PEER_TOPK = 16
EPS = 1e-6

LANES = 128
MXU_DIM = 256
TOKEN_TILE = 3 * MXU_DIM
T_PAD = 11 * TOKEN_TILE
EXPERT_TILE = 1024
VMEM_LIMIT = 56 * 1024 * 1024


def _params(*sem, flags=None):
    return pltpu.CompilerParams(dimension_semantics=sem, vmem_limit_bytes=VMEM_LIMIT, flags=flags)


def _gelu(x):
    return jax.nn.gelu(x)


def _log_sigmoid(x):
    return jnp.minimum(x, 0.0) - jnp.log1p(jnp.exp(-jnp.abs(x)))


def _row_to_col(row, eye):
    return jnp.sum(jnp.where(eye, row, 0.0), axis=-1, keepdims=True)


def _eye(n):
    return lax.broadcasted_iota(jnp.int32, (n, n), 0) == lax.broadcasted_iota(jnp.int32, (n, n), 1)


def _nmm_body(*refs, has_aux, emit_xn):
    x_ref, g_ref, w_ref = refs[:3]
    rest = list(refs[3:])
    w2_ref = rest.pop(0) if has_aux else None
    o_ref = rest.pop(0)
    o2_ref = rest.pop(0) if has_aux else None
    xo_ref = rest.pop(0) if emit_xn else None
    xn_ref = rest.pop(0)

    @pl.when(pl.program_id(1) == 0)
    def _():
        x = x_ref[...]
        ms = jnp.mean(x * x, axis=-1, keepdims=True)
        xn = ((x * lax.rsqrt(ms + EPS)) * g_ref[...]).astype(BF16)
        xn_ref[...] = xn
        if has_aux:
            o2_ref[...] = jnp.dot(xn, w2_ref[...], preferred_element_type=F32)
        if emit_xn:
            xo_ref[...] = xn

    o_ref[...] = jnp.dot(xn_ref[...], w_ref[...], preferred_element_type=F32)


def _nmm(x, g, w, *, tm, tn, w_aux=None, emit_xn=False):
    m, k = x.shape
    n = w.shape[1]
    in_specs = [pl.BlockSpec((tm, k), lambda i, j: (i, 0)),
                pl.BlockSpec((1, k), lambda i, j: (0, 0)),
                pl.BlockSpec((k, tn), lambda i, j: (0, j))]
    args = [x, g.reshape(1, k), w]
    out_shape = [jax.ShapeDtypeStruct((m, n), F32)]
    out_specs = [pl.BlockSpec((tm, tn), lambda i, j: (i, j))]
    if w_aux is not None:
        in_specs.append(pl.BlockSpec((k, LANES), lambda i, j: (0, 0)))
        args.append(w_aux)
        out_shape.append(jax.ShapeDtypeStruct((m, LANES), F32))
        out_specs.append(pl.BlockSpec((tm, LANES), lambda i, j: (i, 0)))
    if emit_xn:
        out_shape.append(jax.ShapeDtypeStruct((m, k), BF16))
        out_specs.append(pl.BlockSpec((tm, k), lambda i, j: (i, 0)))
    return pl.pallas_call(
        functools.partial(_nmm_body, has_aux=w_aux is not None, emit_xn=emit_xn),
        grid=(m // tm, n // tn),
        in_specs=in_specs, out_specs=out_specs, out_shape=out_shape,
        scratch_shapes=[pltpu.VMEM((tm, k), BF16)],
        compiler_params=_params("parallel", "arbitrary"),
        name="nmm",
    )(*args)


def _mmres_body(x_ref, w_ref, r_ref, o_ref):
    o_ref[...] = r_ref[...] + jnp.dot(x_ref[...].astype(BF16), w_ref[...], preferred_element_type=F32)


def _mmres(x, w, res, *, tm, tn):
    m, k = x.shape
    n = w.shape[1]
    return pl.pallas_call(
        _mmres_body,
        grid=(m // tm, n // tn),
        in_specs=[pl.BlockSpec((tm, k), lambda i, j: (i, 0)),
                  pl.BlockSpec((k, tn), lambda i, j: (0, j)),
                  pl.BlockSpec((tm, tn), lambda i, j: (i, j))],
        out_specs=pl.BlockSpec((tm, tn), lambda i, j: (i, j)),
        out_shape=jax.ShapeDtypeStruct((m, n), F32),
        compiler_params=_params("parallel", "parallel"),
        name="mmres",
    )(x, w, res)


def _sgu_body(u_ref, v_ref, ws_ref, bs_ref, gs_ref, o_ref, cv_ref):
    for h in range(A_HEADS):
        sl = slice(h * LANES, (h + 1) * LANES)
        u = _gelu(u_ref[:, sl])
        v = _gelu(v_ref[:, sl])
        ms = jnp.mean(v * v, axis=-1, keepdims=True)
        va = (v * lax.rsqrt(ms + EPS)) * gs_ref[:, sl]
        mixed = jnp.dot(ws_ref[h], va.astype(BF16), preferred_element_type=F32) + bs_ref[:, h:h + 1]
        o_ref[:, sl] = u * mixed
        cv_ref[0, :, sl] = va


def _sgu_prompt(z, ws_tril, bs_t, gs):
    n_chunks = SEQ // SGU_CHUNK
    return pl.pallas_call(
        _sgu_body,
        grid=(T_PROMPT // SGU_CHUNK,),
        in_specs=[pl.BlockSpec((SGU_CHUNK, A_WIDTH), lambda i: (i, 0)),
                  pl.BlockSpec((SGU_CHUNK, A_WIDTH), lambda i: (i, 1)),
                  pl.BlockSpec((A_HEADS, SGU_CHUNK, SGU_CHUNK), lambda i: (0, 0, 0)),
                  pl.BlockSpec((SGU_CHUNK, A_HEADS), lambda i: (0, 0)),
                  pl.BlockSpec((1, A_WIDTH), lambda i: (0, 0))],
        out_specs=[pl.BlockSpec((SGU_CHUNK, A_WIDTH), lambda i: (i, 0)),
                   pl.BlockSpec((1, SGU_CHUNK, A_WIDTH), lambda i: (i // n_chunks, 0, 0))],
        out_shape=[jax.ShapeDtypeStruct((T_PAD, D_MODEL), F32),
                   jax.ShapeDtypeStruct((BATCH, SGU_CHUNK, A_WIDTH), F32)],
        compiler_params=_params("arbitrary"),
        name="sgu_prompt",
    )(z, z, ws_tril, bs_t, gs)


def _gla_body(q_ref, k_ref, v_ref, r_ref, zg_ref, wa_ref, ba_ref, gg_ref, mix_ref, o_ref, st_ref, s_ref):
    del mix_ref
    c = pl.program_id(1)

    @pl.when(c == 0)
    def _():
        s_ref[...] = jnp.zeros_like(s_ref)

    x = jnp.dot(zg_ref[...].astype(BF16), wa_ref[...], preferred_element_type=F32) + ba_ref[...]
    lg = _log_sigmoid(x) * (1.0 / GATE_TAU)
    tri = (lax.broadcasted_iota(jnp.int32, (GLA_CHUNK, GLA_CHUNK), 1)
           <= lax.broadcasted_iota(jnp.int32, (GLA_CHUNK, GLA_CHUNK), 0)).astype(BF16)
    hi = lg.astype(BF16)
    r1 = lg - hi.astype(F32)
    mid = r1.astype(BF16)
    lo = (r1 - mid.astype(F32)).astype(BF16)
    gcum = (jnp.dot(tri, hi, preferred_element_type=F32) + jnp.dot(tri, mid, preferred_element_type=F32)
            + jnp.dot(tri, lo, preferred_element_type=F32))
    eye = _eye(B_KEY_DIM)
    t_iota = lax.broadcasted_iota(jnp.int32, (GLA_SUB, B_KEY_DIM), 0)

    for h in range(B_HEADS):
        dk = slice(h * B_KEY_DIM, (h + 1) * B_KEY_DIM)
        dv = slice(h * B_VAL_DIM, (h + 1) * B_VAL_DIM)
        q = q_ref[:, dk] * (B_KEY_DIM ** -0.5)
        k = k_ref[:, dk]
        v = v_ref[:, dv]
        g = gcum[:, dk]
        vb = v.astype(BF16)
        state = s_ref[h]
        o_inter = jnp.dot((q * jnp.exp(g)).astype(BF16), state.astype(BF16), preferred_element_type=F32)
        parts = []
        for i in range(GLA_CHUNK // GLA_SUB):
            lo_r, hi_r = GLA_SUB * i, GLA_SUB * (i + 1)
            qi, ki, vi, gi = q[lo_r:hi_r], k[lo_r:hi_r], v[lo_r:hi_r], g[lo_r:hi_r]
            acc = jnp.zeros((GLA_SUB, B_VAL_DIM), F32)
            for s in range(GLA_SUB):
                dec = jnp.exp(jnp.where(t_iota >= s, gi - gi[s:s + 1, :], -jnp.inf))
                col = jnp.sum(qi * ki[s:s + 1, :] * dec, axis=-1, keepdims=True)
                acc = acc + col * vi[s:s + 1, :]
            if i > 0:
                gref = g[lo_r - 1:lo_r, :]
                qd = (qi * jnp.exp(gi - gref)).astype(BF16)
                kd = (k[:lo_r] * jnp.exp(gref - g[:lo_r])).astype(BF16)
                sc = lax.dot_general(qd, kd, (((1,), (1,)), ((), ())), preferred_element_type=F32)
                acc = acc + jnp.dot(sc.astype(BF16), vb[:lo_r], preferred_element_type=F32)
            parts.append(acc)
        o = o_inter + jnp.concatenate(parts, axis=0)

        g_last = g[GLA_CHUNK - 1:GLA_CHUNK, :]
        kdec = k * jnp.exp(g_last - g)
        upd = jnp.dot(kdec.T.astype(BF16), vb, preferred_element_type=F32)
        s_ref[h] = _row_to_col(jnp.exp(g_last), eye) * state + upd

        ms = jnp.mean(o * o, axis=-1, keepdims=True)
        on = (o * lax.rsqrt(ms + EPS)) * gg_ref[:, dv]
        r = r_ref[:, dv]
        o_ref[:, dv] = (r * jax.nn.sigmoid(r)) * on

    @pl.when(c == pl.num_programs(1) - 1)
    def _():
        st_ref[0] = s_ref[...]


def _gla_prompt(z, zg, wa_pad, ba, gg, mix):
    n_chunks = SEQ // GLA_CHUNK
    row = lambda b, c: b * n_chunks + c
    return pl.pallas_call(
        _gla_body,
        grid=(BATCH, n_chunks),
        in_specs=[pl.BlockSpec((GLA_CHUNK, B_KEY_WIDTH), lambda b, c: (row(b, c), 4)),
                  pl.BlockSpec((GLA_CHUNK, B_KEY_WIDTH), lambda b, c: (row(b, c), 5)),
                  pl.BlockSpec((GLA_CHUNK, B_VAL_WIDTH), lambda b, c: (row(b, c), 3)),
                  pl.BlockSpec((GLA_CHUNK, B_VAL_WIDTH), lambda b, c: (row(b, c), 4)),
                  pl.BlockSpec((GLA_CHUNK, LANES), lambda b, c: (row(b, c), 0)),
                  pl.BlockSpec((LANES, B_KEY_WIDTH), lambda b, c: (0, 0)),
                  pl.BlockSpec((1, B_KEY_WIDTH), lambda b, c: (0, 0)),
                  pl.BlockSpec((1, B_VAL_WIDTH), lambda b, c: (0, 0)),
                  pl.BlockSpec(memory_space=pl.ANY)],
        out_specs=[pl.BlockSpec((GLA_CHUNK, B_VAL_WIDTH), lambda b, c: (row(b, c), 1)),
                   pl.BlockSpec((1, B_HEADS, B_KEY_DIM, B_VAL_DIM), lambda b, c: (b, 0, 0, 0))],
        out_shape=[jax.ShapeDtypeStruct((T_PAD, D_MODEL), F32),
                   jax.ShapeDtypeStruct((BATCH, B_HEADS, B_KEY_DIM, B_VAL_DIM), F32)],
        scratch_shapes=[pltpu.VMEM((B_HEADS, B_KEY_DIM, B_VAL_DIM), F32)],
        input_output_aliases={8: 0},
        compiler_params=_params("arbitrary", "arbitrary"),
        name="gla_prompt",
    )(z, z, z, z, zg, wa_pad, ba, gg, mix)


SAMPLE_ROWS = 8


def _smix_body(z_ref, zg_ref, st_ref, wa_ref, ba_ref, gs_ref, gg_ref, w00_ref, b00_ref, mix_ref,
               o_ref, cv_ref, so_ref):
    del mix_ref
    u = _gelu(z_ref[:, 0:A_WIDTH])
    v = _gelu(z_ref[:, A_WIDTH:2 * A_WIDTH])
    for h in range(A_HEADS):
        sl = slice(h * LANES, (h + 1) * LANES)
        vh = v[:, sl]
        ms = jnp.mean(vh * vh, axis=-1, keepdims=True)
        va = (vh * lax.rsqrt(ms + EPS)) * gs_ref[:, sl]
        cv_ref[:, sl] = va
        o_ref[:, sl] = u[:, sl] * (w00_ref[:, sl] * va + b00_ref[:, sl])

    x = jnp.dot(zg_ref[...].astype(BF16), wa_ref[...], preferred_element_type=F32) + ba_ref[...]
    a = jnp.exp(_log_sigmoid(x) * (1.0 / GATE_TAU))
    q0, k0, v0, r0 = 2 * A_WIDTH, 2 * A_WIDTH + B_KEY_WIDTH, 2 * A_WIDTH + 2 * B_KEY_WIDTH, Z_COLS - B_VAL_WIDTH
    eye = _eye(B_KEY_DIM)
    for b in range(SAMPLE_ROWS):
        for h in range(B_HEADS):
            dk = slice(h * B_KEY_DIM, (h + 1) * B_KEY_DIM)
            a_col = _row_to_col(a[b:b + 1, dk], eye)
            k_col = _row_to_col(z_ref[b:b + 1, k0 + h * B_KEY_DIM:k0 + (h + 1) * B_KEY_DIM], eye)
            q_col = _row_to_col(z_ref[b:b + 1, q0 + h * B_KEY_DIM:q0 + (h + 1) * B_KEY_DIM], eye) * (B_KEY_DIM ** -0.5)
            vrow = z_ref[b:b + 1, v0 + h * B_VAL_DIM:v0 + (h + 1) * B_VAL_DIM]
            s_new = a_col * st_ref[b, h] + k_col * vrow
            so_ref[b, h] = s_new
            o = jnp.sum(q_col * s_new, axis=0, keepdims=True)
            ms = jnp.mean(o * o, axis=-1, keepdims=True)
            on = (o * lax.rsqrt(ms + EPS)) * gg_ref[:, h * B_VAL_DIM:(h + 1) * B_VAL_DIM]
            r = z_ref[b:b + 1, r0 + h * B_VAL_DIM:r0 + (h + 1) * B_VAL_DIM]
            o_ref[b:b + 1, A_WIDTH + h * B_VAL_DIM:A_WIDTH + (h + 1) * B_VAL_DIM] = (r * jax.nn.sigmoid(r)) * on


def _sample_mixers(z, zg, state, wa_pad, ba, gs, gg, w00, b00, mix):
    first = T_PROMPT // SAMPLE_ROWS
    return pl.pallas_call(
        _smix_body,
        grid=(T_SAMPLE // SAMPLE_ROWS,),
        in_specs=[pl.BlockSpec((SAMPLE_ROWS, Z_COLS), lambda i: (first + i, 0)),
                  pl.BlockSpec((SAMPLE_ROWS, LANES), lambda i: (first + i, 0)),
                  pl.BlockSpec((SAMPLE_ROWS, B_HEADS, B_KEY_DIM, B_VAL_DIM), lambda i: (i, 0, 0, 0)),
                  pl.BlockSpec((LANES, B_KEY_WIDTH), lambda i: (0, 0)),
                  pl.BlockSpec((1, B_KEY_WIDTH), lambda i: (0, 0)),
                  pl.BlockSpec((1, A_WIDTH), lambda i: (0, 0)),
                  pl.BlockSpec((1, B_VAL_WIDTH), lambda i: (0, 0)),
                  pl.BlockSpec((1, A_WIDTH), lambda i: (0, 0)),
                  pl.BlockSpec((1, A_WIDTH), lambda i: (0, 0)),
                  pl.BlockSpec(memory_space=pl.ANY)],
        out_specs=[pl.BlockSpec((SAMPLE_ROWS, D_MODEL), lambda i: (first + i, 0)),
                   pl.BlockSpec((SAMPLE_ROWS, A_WIDTH), lambda i: (i, 0)),
                   pl.BlockSpec((SAMPLE_ROWS, B_HEADS, B_KEY_DIM, B_VAL_DIM), lambda i: (i, 0, 0, 0))],
        out_shape=[jax.ShapeDtypeStruct((T_PAD, D_MODEL), F32),
                   jax.ShapeDtypeStruct((T_SAMPLE, A_WIDTH), F32),
                   jax.ShapeDtypeStruct((T_SAMPLE, B_HEADS, B_KEY_DIM, B_VAL_DIM), F32)],
        input_output_aliases={9: 0},
        compiler_params=_params("parallel"),
        name="sample_mixers",
    )(z, zg, state, wa_pad, ba, gs, gg, w00, b00, mix)


ATT_TILE = 512


def _attn_prompt_body(q_ref, k_ref, v_ref, o_ref):
    for h in range(X_HEADS):
        sl = slice(h * X_HEAD_DIM, (h + 1) * X_HEAD_DIM)
        q = q_ref[:, sl].astype(BF16)
        k = k_ref[0, :, sl].astype(BF16)
        s = lax.dot_general(q, k, (((1,), (1,)), ((), ())), preferred_element_type=F32) * (X_HEAD_DIM ** -0.5)
        s = s - jnp.max(s, axis=-1, keepdims=True)
        e = jnp.exp(s)
        p = e / jnp.sum(e, axis=-1, keepdims=True)
        o_ref[:, sl] = jnp.dot(p.astype(BF16), v_ref[0, :, sl].astype(BF16), preferred_element_type=F32)


def _attn_prompt(qx, mk, mv):
    tiles = SEQ // ATT_TILE
    return pl.pallas_call(
        _attn_prompt_body,
        grid=(BATCH, tiles),
        in_specs=[pl.BlockSpec((ATT_TILE, D_MODEL), lambda b, t: (b * tiles + t, 0)),
                  pl.BlockSpec((1, MEM_LEN, D_MODEL), lambda b, t: (b, 0, 0)),
                  pl.BlockSpec((1, MEM_LEN, D_MODEL), lambda b, t: (b, 0, 0))],
        out_specs=pl.BlockSpec((ATT_TILE, D_MODEL), lambda b, t: (b * tiles + t, 0)),
        out_shape=jax.ShapeDtypeStruct((T_PAD, D_MODEL), F32),
        compiler_params=_params("parallel", "parallel"),
        name="attn_prompt",
    )(qx, mk, mv)


def _attn_sample_body(q_ref, k_ref, v_ref, o_ref):
    s = jnp.sum(k_ref[0, 0] * q_ref[...], axis=-1, keepdims=True) * (X_HEAD_DIM ** -0.5)
    s = s - jnp.max(s, axis=0, keepdims=True)
    e = jnp.exp(s)
    p = e / jnp.sum(e, axis=0, keepdims=True)
    o_ref[0] = jnp.sum(p * v_ref[0, 0], axis=0)


def _attn_sample(q, ck, cv):
    kv_spec = pl.BlockSpec((1, 1, MEM_LEN, X_HEADS, X_HEAD_DIM), lambda i: (0, i, 0, 0, 0))
    return pl.pallas_call(
        _attn_sample_body,
        grid=(T_SAMPLE,),
        in_specs=[pl.BlockSpec((1, X_HEADS, X_HEAD_DIM), lambda i: (i, 0, 0)), kv_spec, kv_spec],
        out_specs=pl.BlockSpec((1, X_HEADS, X_HEAD_DIM), lambda i: (i, 0, 0)),
        out_shape=jax.ShapeDtypeStruct((T_SAMPLE, X_HEADS, X_HEAD_DIM), F32),
        compiler_params=_params("parallel"),
        name="attn_sample",
    )(q, ck, cv)


def _top16_distinct(s):
    rows = lax.broadcasted_iota(jnp.int32, (PEER_TOPK, LANES), 0)
    vals = jnp.full((PEER_TOPK, LANES), -jnp.inf, F32)
    cnts = jnp.zeros((PEER_TOPK, LANES), F32)
    work = s
    for kk in range(PEER_TOPK):
        m = jnp.max(work, axis=0, keepdims=True)
        eq = work == m
        cnt = jnp.sum(jnp.where(eq, 1.0, 0.0), axis=0, keepdims=True)
        vals = jnp.where(rows == kk, m, vals)
        cnts = jnp.where(rows == kk, cnt, cnts)
        work = jnp.where(eq, -jnp.inf, work)
    return vals, cnts


_CAND_ROWS = tuple((a, PEER_TOPK // (a + 1)) for a in range(1, 8))


def _peer_route_body(pq_ref, keys_ref, s0_ref, s1_ref, a_ref, b_ref, th_ref):
    rows8 = lax.broadcasted_iota(jnp.int32, (8, LANES), 0)

    def head(h, carry):
        q0 = pq_ref[:, pl.ds(pl.multiple_of(h * 2 * LANES, LANES), LANES)].astype(BF16)
        q1 = pq_ref[:, pl.ds(pl.multiple_of(h * 2 * LANES + LANES, LANES), LANES)].astype(BF16)
        nt = (((1,), (1,)), ((), ()))
        s0 = lax.dot_general(keys_ref[2 * h], q0, nt, preferred_element_type=F32)
        s1 = lax.dot_general(keys_ref[2 * h + 1], q1, nt, preferred_element_type=F32)
        v0, n0 = _top16_distinct(s0)
        v1, n1 = _top16_distinct(s1)
        cs = [v0[0:1] + v1]
        ws = [n0[0:1] * n1]
        for a, nb in _CAND_ROWS:
            ok = rows8 < nb
            cs.append(jnp.where(ok, v0[a:a + 1] + v1[0:8], -jnp.inf))
            ws.append(jnp.where(ok, n0[a:a + 1] * n1[0:8], 0.0))
        cs.append(v0[8:16] + v1[0:1])
        ws.append(n0[8:16] * n1[0:1])
        cand = jnp.concatenate(cs, axis=0)
        wgt = jnp.concatenate(ws, axis=0)
        starts = [0] + [16 + 8 * i for i in range(len(_CAND_ROWS))] + [16 + 8 * len(_CAND_ROWS)]
        counts = [PEER_TOPK] + [nb for _, nb in _CAND_ROWS] + [8]
        above = jnp.zeros_like(cand)
        for st, cn in zip(starts, counts):
            for rr in range(st, st + cn):
                above = above + jnp.where(cand[rr:rr + 1] > cand, wgt[rr:rr + 1], 0.0)
        theta = jnp.min(jnp.where(above <= PEER_TOPK - 1.0, cand, jnp.inf), axis=0, keepdims=True)
        top = cand[0:1]
        zsum = jnp.sum(jnp.where(cand >= theta, wgt * jnp.exp(cand - top), 0.0), axis=0, keepdims=True)
        s0_ref[h] = s0
        s1_ref[h] = s1
        a_ref[h] = jnp.exp(s0 - v0[0:1]) / zsum
        b_ref[h] = jnp.exp(s1 - v1[0:1])
        th_ref[h] = theta
        return carry

    lax.fori_loop(0, PEER_HEADS, head, 0)


def _peer_route(pq, keys):
    big = jax.ShapeDtypeStruct((PEER_HEADS, N_KEYS, T_PAD), F32)
    big_spec = pl.BlockSpec((PEER_HEADS, N_KEYS, LANES), lambda i: (0, 0, i))
    return pl.pallas_call(
        _peer_route_body,
        grid=(T_PAD // LANES,),
        in_specs=[pl.BlockSpec((LANES, D_MODEL), lambda i: (i, 0)),
                  pl.BlockSpec((2 * PEER_HEADS, N_KEYS, LANES), lambda i: (0, 0, 0))],
        out_specs=[big_spec, big_spec, big_spec, big_spec,
                   pl.BlockSpec((PEER_HEADS, 1, LANES), lambda i: (0, 0, i))],
        out_shape=[big, big, big, big, jax.ShapeDtypeStruct((PEER_HEADS, 1, T_PAD), F32)],
        compiler_params=_params("parallel"),
        name="peer_route",
    )(pq, keys)


N_EXPERT_BLOCKS = N_EXPERTS // EXPERT_TILE
N_TOKEN_BLOCKS = T_PAD // TOKEN_TILE
N_PAIRS = N_TOKEN_BLOCKS * N_EXPERT_BLOCKS
PIPE_LAG = 2
GATE_ROWS = 32
N_GROUPS = EXPERT_TILE // N_KEYS
OUT_ROWS = D_MODEL // N_GROUPS


def _pair(n, lag):
    c = jnp.clip(n - lag, 0, N_PAIRS - 1)
    return c // N_EXPERT_BLOCKS, c % N_EXPERT_BLOCKS


def _peer_dense_body(h_ref, u_ref, vt_ref, s0_ref, s1_ref, a_ref, b_ref, th_ref, o_ref, s_even, s_odd, a_even, a_odd):
    n = pl.program_id(0)

    @pl.when(n == 0)
    def _():
        for ref in (s_even, s_odd, a_even, a_odd):
            ref[...] = jnp.zeros_like(ref)

    @pl.when(jnp.logical_or(n == 0, jnp.maximum(n - PIPE_LAG, 0) % N_EXPERT_BLOCKS == 0))
    def _():
        o_ref[...] = jnp.zeros_like(o_ref)

    def run(s_w, s_r, a_w, a_r):
        def cols(nn, carry):
            t0 = pl.multiple_of(nn * MXU_DIM, MXU_DIM)
            tok = pl.ds(t0, MXU_DIM)
            for g in range(N_GROUPS):
                for lt in range(MXU_DIM // LANES):
                    lanes = pl.ds(pl.multiple_of(t0 + lt * LANES, LANES), LANES)
                    for j0 in range(0, N_KEYS, GATE_ROWS):
                        keys = slice(j0, j0 + GATE_ROWS)
                        rows = slice(g * N_KEYS + j0, g * N_KEYS + j0 + GATE_ROWS)
                        gate = jnp.zeros((GATE_ROWS, LANES), F32)
                        for h in range(PEER_HEADS):
                            c = s0_ref[h, g:g + 1, lanes] + s1_ref[h, keys, lanes]
                            w = a_ref[h, g:g + 1, lanes] * b_ref[h, keys, lanes]
                            gate = gate + jnp.where(c >= th_ref[h, :, lanes], w, 0.0)
                        a_w[rows, lanes] = (_gelu(s_r[rows, lanes]) * gate).astype(BF16)
                orow = slice(g * OUT_ROWS, (g + 1) * OUT_ROWS)
                o_ref[orow, tok] += jnp.dot(vt_ref[orow, :], a_r[:, tok], preferred_element_type=F32)
                srow = slice(g * N_KEYS, (g + 1) * N_KEYS)
                s_w[srow, tok] = jnp.dot(u_ref[srow, :], h_ref[:, tok], preferred_element_type=F32)
            return carry

        lax.fori_loop(0, TOKEN_TILE // MXU_DIM, cols, 0)

    @pl.when(n % 2 == 0)
    def _():
        run(s_even, s_odd, a_odd, a_even)

    @pl.when(n % 2 == 1)
    def _():
        run(s_odd, s_even, a_even, a_odd)


def _peer_dense(hn_t, u_bf, vt_bf, s0, s1, a, b, th):
    once = pl.Buffered(1)
    route_spec = pl.BlockSpec((PEER_HEADS, N_KEYS, TOKEN_TILE), lambda n: (0, 0, _pair(n, 1)[0]), pipeline_mode=once)
    row_spec = pl.BlockSpec((PEER_HEADS, EXPERT_TILE // N_KEYS, TOKEN_TILE),
                            lambda n: (0, _pair(n, 1)[1], _pair(n, 1)[0]))
    return pl.pallas_call(
        _peer_dense_body,
        grid=(N_PAIRS + PIPE_LAG,),
        in_specs=[pl.BlockSpec((D_MODEL, TOKEN_TILE), lambda n: (0, _pair(n, 0)[0]), pipeline_mode=once),
                  pl.BlockSpec((EXPERT_TILE, D_MODEL), lambda n: (_pair(n, 0)[1], 0)),
                  pl.BlockSpec((D_MODEL, EXPERT_TILE), lambda n: (0, _pair(n, PIPE_LAG)[1])),
                  row_spec, route_spec, row_spec, route_spec,
                  pl.BlockSpec((PEER_HEADS, 1, TOKEN_TILE), lambda n: (0, 0, _pair(n, 1)[0]))],
        out_specs=pl.BlockSpec((D_MODEL, TOKEN_TILE), lambda n: (0, _pair(n, PIPE_LAG)[0])),
        out_shape=jax.ShapeDtypeStruct((D_MODEL, T_PAD), F32),
        scratch_shapes=[pltpu.VMEM((EXPERT_TILE, TOKEN_TILE), F32), pltpu.VMEM((EXPERT_TILE, TOKEN_TILE), F32),
                        pltpu.VMEM((EXPERT_TILE, TOKEN_TILE), BF16), pltpu.VMEM((EXPERT_TILE, TOKEN_TILE), BF16)],
        compiler_params=_params("arbitrary"),
        name="peer_dense",
    )(hn_t, u_bf, vt_bf, s0, s1, a, b, th)


def _final_body(x_ref, pt_ref, g_ref, o_ref):
    x = x_ref[...] + pt_ref[...].T
    ms = jnp.mean(x * x, axis=-1, keepdims=True)
    o_ref[...] = (x * lax.rsqrt(ms + EPS)) * g_ref[...]


def _final(x2, peer_t, g, first_tile, n_tiles):
    return pl.pallas_call(
        _final_body,
        grid=(n_tiles,),
        in_specs=[pl.BlockSpec((LANES, D_MODEL), lambda i: (first_tile + i, 0)),
                  pl.BlockSpec((D_MODEL, LANES), lambda i: (0, first_tile + i)),
                  pl.BlockSpec((1, D_MODEL), lambda i: (0, 0))],
        out_specs=pl.BlockSpec((LANES, D_MODEL), lambda i: (i, 0)),
        out_shape=jax.ShapeDtypeStruct((n_tiles * LANES, D_MODEL), F32),
        compiler_params=_params("parallel"),
        name="final_norm",
    )(x2, peer_t, g.reshape(1, D_MODEL))


def kernel(x_prompt, x_sample, mem_prompt, cache_mem_k, cache_mem_v, state_gla, g_mix, w_in, w_alpha, b_alpha, w_s, b_s, g_sgu, g_gla, w_out, g_mem, w_xk, w_xv, g_xq, w_xq, w_xo, g_ffn, w_pq, sub_keys, u_emb, v_emb, g_final):
    w_in_main = w_in[0, :, :Z_COLS].astype(BF16)
    w_in_gate = jnp.pad(w_in[0, :, Z_COLS:], ((0, 0), (0, LANES - GATE_RANK))).astype(BF16)
    wa_pad = jnp.pad(w_alpha[0], ((0, LANES - GATE_RANK), (0, 0))).astype(BF16)
    ba = b_alpha[0].reshape(1, B_KEY_WIDTH)
    causal = jnp.tril(jnp.ones((SGU_CHUNK, SGU_CHUNK), dtype=bool))
    ws_tril = jnp.where(causal, w_s[0], 0).astype(BF16)
    bs_t = b_s[0].T
    w00 = jnp.repeat(w_s[0, :, 0, 0], LANES).reshape(1, A_WIDTH)
    b00 = jnp.repeat(b_s[0, :, 0], LANES).reshape(1, A_WIDTH)
    gs = g_sgu[0].reshape(1, A_WIDTH)
    gg = g_gla[0].reshape(1, B_VAL_WIDTH)
    keys = sub_keys[0].reshape(2 * PEER_HEADS, N_KEYS, LANES).astype(BF16)
    u_bf = u_emb[0].astype(BF16)
    vt_bf = v_emb[0].T.astype(BF16)

    x_all = jnp.concatenate([x_prompt.reshape(T_PROMPT, D_MODEL), x_sample.reshape(T_SAMPLE, D_MODEL),
                             jnp.zeros((T_PAD - T_ALL, D_MODEL), F32)], axis=0)

    mem = mem_prompt.reshape(BATCH * MEM_LEN, D_MODEL)
    mk, = _nmm(mem, g_mem[0], w_xk[0].astype(BF16), tm=512, tn=1024)
    mv, = _nmm(mem, g_mem[0], w_xv[0].astype(BF16), tm=512, tn=1024)

    z, zg = _nmm(x_all, g_mix[0], w_in_main, tm=TOKEN_TILE, tn=1024, w_aux=w_in_gate)
    mix, cv_p = _sgu_prompt(z, ws_tril, bs_t, gs)
    mix, st_p = _gla_prompt(z, zg, wa_pad, ba, gg, mix)
    mix, cv_s, st_s = _sample_mixers(z, zg, state_gla[0], wa_pad, ba, gs, gg, w00, b00, mix)
    mix = lax.dynamic_update_slice(mix, jnp.zeros((T_PAD - T_ALL, D_MODEL), F32), (T_ALL, 0))
    x1 = _mmres(mix, w_out[0].astype(BF16), x_all, tm=TOKEN_TILE, tn=1024)

    qx, = _nmm(x1, g_xq[0], w_xq[0].astype(BF16), tm=TOKEN_TILE, tn=1024)
    att = _attn_prompt(qx, mk.reshape(BATCH, MEM_LEN, D_MODEL), mv.reshape(BATCH, MEM_LEN, D_MODEL))
    att_s = _attn_sample(qx[T_PROMPT:T_ALL].reshape(T_SAMPLE, X_HEADS, X_HEAD_DIM), cache_mem_k, cache_mem_v)
    att_tail = jnp.concatenate([att_s.reshape(T_SAMPLE, D_MODEL), jnp.zeros((T_PAD - T_ALL, D_MODEL), F32)], axis=0)
    att = lax.dynamic_update_slice(att, att_tail, (T_PROMPT, 0))
    x2 = _mmres(att, w_xo[0].astype(BF16), x1, tm=TOKEN_TILE, tn=1024)

    pq, hn = _nmm(x2, g_ffn[0], w_pq[0].astype(BF16), tm=TOKEN_TILE, tn=1024, emit_xn=True)
    s0, s1, a, b, th = _peer_route(pq, keys)
    peer_t = _peer_dense(hn.T, u_bf, vt_bf, s0, s1, a, b, th)

    y_p = _final(x2, peer_t, g_final, 0, T_PROMPT // LANES)
    y_s = _final(x2, peer_t, g_final, T_PROMPT // LANES, T_SAMPLE // LANES)

    return (y_p.reshape(BATCH, SEQ, D_MODEL),
            y_s.reshape(DEC_BATCH, 1, D_MODEL),
            mk.reshape(1, BATCH, MEM_LEN, X_HEADS, X_HEAD_DIM),
            mv.reshape(1, BATCH, MEM_LEN, X_HEADS, X_HEAD_DIM),
            st_p.reshape(1, BATCH, B_HEADS, B_KEY_DIM, B_VAL_DIM),
            st_s.reshape(1, DEC_BATCH, B_HEADS, B_KEY_DIM, B_VAL_DIM),
            cv_p.reshape(1, BATCH, SGU_CHUNK, A_HEADS, LANES),
            cv_s.reshape(1, DEC_BATCH, 1, A_HEADS, LANES))
```

```python
import functools

import jax
import jax.numpy as jnp
from jax import lax
from jax.experimental import pallas as pl
from jax.experimental.pallas import tpu as pltpu

F32 = jnp.float32
BF16 = jnp.bfloat16

D_MODEL = 2048
BATCH = 4
SEQ = 2048
DEC_BATCH = 128
T_PROMPT = BATCH * SEQ
T_SAMPLE = DEC_BATCH
T_ALL = T_PROMPT + T_SAMPLE

A_HEADS = 8
A_WIDTH = 1024
SGU_CHUNK = 128
B_HEADS = 4
B_KEY_DIM = 128
B_VAL_DIM = 256
B_KEY_WIDTH = 512
B_VAL_WIDTH = 1024
GATE_RANK = 16
GATE_TAU = 16.0
GLA_CHUNK = 64
GLA_SUB = 16
Z_COLS = 2 * A_WIDTH + 2 * B_KEY_WIDTH + 2 * B_VAL_WIDTH
MEM_LEN = 256
X_HEADS = 4
X_HEAD_DIM = 512
N_KEYS = 128
N_EXPERTS = N_KEYS * N_KEYS
PEER_HEADS = 8
PEER_TOPK = 16
EPS = 1e-6

LANES = 128
MXU_DIM = 256
TOKEN_TILE = 3 * MXU_DIM
T_PAD = 11 * TOKEN_TILE
EXPERT_TILE = 1024
VMEM_LIMIT = 56 * 1024 * 1024


def _params(*sem, flags=None):
    return pltpu.CompilerParams(dimension_semantics=sem, vmem_limit_bytes=VMEM_LIMIT, flags=flags)


def _gelu(x):
    return jax.nn.gelu(x)


def _log_sigmoid(x):
    return jnp.minimum(x, 0.0) - jnp.log1p(jnp.exp(-jnp.abs(x)))


def _row_to_col(row, eye):
    return jnp.sum(jnp.where(eye, row, 0.0), axis=-1, keepdims=True)


def _eye(n):
    return lax.broadcasted_iota(jnp.int32, (n, n), 0) == lax.broadcasted_iota(jnp.int32, (n, n), 1)


def _nmm_body(*refs, has_aux, emit_xn):
    x_ref, g_ref, w_ref = refs[:3]
    rest = list(refs[3:])
    w2_ref = rest.pop(0) if has_aux else None
    o_ref = rest.pop(0)
    o2_ref = rest.pop(0) if has_aux else None
    xo_ref = rest.pop(0) if emit_xn else None
    xn_ref = rest.pop(0)

    @pl.when(pl.program_id(1) == 0)
    def _():
        x = x_ref[...]
        ms = jnp.mean(x * x, axis=-1, keepdims=True)
        xn = ((x * lax.rsqrt(ms + EPS)) * g_ref[...]).astype(BF16)
        xn_ref[...] = xn
        if has_aux:
            o2_ref[...] = jnp.dot(xn, w2_ref[...], preferred_element_type=F32)
        if emit_xn:
            xo_ref[...] = xn

    o_ref[...] = jnp.dot(xn_ref[...], w_ref[...], preferred_element_type=F32).astype(o_ref.dtype)


def _nmm(x, g, w, *, tm, tn, w_aux=None, emit_xn=False, out_dtype=F32):
    m, k = x.shape
    n = w.shape[1]
    in_specs = [pl.BlockSpec((tm, k), lambda i, j: (i, 0)),
                pl.BlockSpec((1, k), lambda i, j: (0, 0)),
                pl.BlockSpec((k, tn), lambda i, j: (0, j))]
    args = [x, g.reshape(1, k), w]
    out_shape = [jax.ShapeDtypeStruct((m, n), out_dtype)]
    out_specs = [pl.BlockSpec((tm, tn), lambda i, j: (i, j))]
    if w_aux is not None:
        in_specs.append(pl.BlockSpec((k, LANES), lambda i, j: (0, 0)))
        args.append(w_aux)
        out_shape.append(jax.ShapeDtypeStruct((m, LANES), F32))
        out_specs.append(pl.BlockSpec((tm, LANES), lambda i, j: (i, 0)))
    if emit_xn:
        out_shape.append(jax.ShapeDtypeStruct((m, k), BF16))
        out_specs.append(pl.BlockSpec((tm, k), lambda i, j: (i, 0)))
    return pl.pallas_call(
        functools.partial(_nmm_body, has_aux=w_aux is not None, emit_xn=emit_xn),
        grid=(m // tm, n // tn),
        in_specs=in_specs, out_specs=out_specs, out_shape=out_shape,
        scratch_shapes=[pltpu.VMEM((tm, k), BF16)],
        compiler_params=_params("parallel", "arbitrary"),
        name="nmm",
    )(*args)


def _mmres_body(x_ref, w_ref, r_ref, o_ref):
    o_ref[...] = r_ref[...] + jnp.dot(x_ref[...].astype(BF16), w_ref[...], preferred_element_type=F32)


def _mmres(x, w, res, *, tm):
    m, k = x.shape
    n = w.shape[1]
    return pl.pallas_call(
        _mmres_body,
        grid=(m // tm,),
        in_specs=[pl.BlockSpec((tm, k), lambda i: (i, 0)),
                  pl.BlockSpec((k, n), lambda i: (0, 0), pipeline_mode=pl.Buffered(1)),
                  pl.BlockSpec((tm, n), lambda i: (i, 0))],
        out_specs=pl.BlockSpec((tm, n), lambda i: (i, 0)),
        out_shape=jax.ShapeDtypeStruct((m, n), F32),
        compiler_params=_params("parallel"),
        name="mmres",
    )(x, w, res)


def _sgu_body(u_ref, v_ref, ws_ref, bs_ref, gs_ref, mix_ref, o_ref, cv_ref):
    del mix_ref
    for h in range(A_HEADS):
        sl = slice(h * LANES, (h + 1) * LANES)
        u = _gelu(u_ref[:, sl])
        v = _gelu(v_ref[:, sl])
        ms = jnp.mean(v * v, axis=-1, keepdims=True)
        va = (v * lax.rsqrt(ms + EPS)) * gs_ref[:, sl]
        mixed = jnp.dot(ws_ref[h], va.astype(BF16), preferred_element_type=F32) + bs_ref[:, h:h + 1]
        o_ref[:, sl] = u * mixed
        cv_ref[0, :, sl] = va


def _sgu_prompt(z, ws_tril, bs_t, gs, mix):
    n_chunks = SEQ // SGU_CHUNK
    return pl.pallas_call(
        _sgu_body,
        grid=(T_PROMPT // SGU_CHUNK,),
        in_specs=[pl.BlockSpec((SGU_CHUNK, A_WIDTH), lambda i: (i, 0)),
                  pl.BlockSpec((SGU_CHUNK, A_WIDTH), lambda i: (i, 1)),
                  pl.BlockSpec((A_HEADS, SGU_CHUNK, SGU_CHUNK), lambda i: (0, 0, 0)),
                  pl.BlockSpec((SGU_CHUNK, A_HEADS), lambda i: (0, 0)),
                  pl.BlockSpec((1, A_WIDTH), lambda i: (0, 0)),
                  pl.BlockSpec(memory_space=pl.ANY)],
        out_specs=[pl.BlockSpec((SGU_CHUNK, A_WIDTH), lambda i: (i, 0)),
                   pl.BlockSpec((1, SGU_CHUNK, A_WIDTH), lambda i: (i // n_chunks, 0, 0))],
        out_shape=[jax.ShapeDtypeStruct((T_PAD, D_MODEL), F32),
                   jax.ShapeDtypeStruct((BATCH, SGU_CHUNK, A_WIDTH), F32)],
        input_output_aliases={5: 0},
        compiler_params=_params("arbitrary"),
        name="sgu_prompt",
    )(z, z, ws_tril, bs_t, gs, mix)


def _gla_body(q_ref, k_ref, v_ref, r_ref, zg_ref, wa_ref, ba_ref, gg_ref, mix_ref, o_ref, st_ref, s_ref):
    del mix_ref
    c = pl.program_id(1)

    @pl.when(c == 0)
    def _():
        s_ref[...] = jnp.zeros_like(s_ref)

    x = jnp.dot(zg_ref[...].astype(BF16), wa_ref[...], preferred_element_type=F32) + ba_ref[...]
    lg = _log_sigmoid(x) * (1.0 / GATE_TAU)
    tri = (lax.broadcasted_iota(jnp.int32, (GLA_CHUNK, GLA_CHUNK), 1)
           <= lax.broadcasted_iota(jnp.int32, (GLA_CHUNK, GLA_CHUNK), 0)).astype(BF16)
    hi = lg.astype(BF16)
    r1 = lg - hi.astype(F32)
    mid = r1.astype(BF16)
    lo = (r1 - mid.astype(F32)).astype(BF16)
    gcum = (jnp.dot(tri, hi, preferred_element_type=F32) + jnp.dot(tri, mid, preferred_element_type=F32)
            + jnp.dot(tri, lo, preferred_element_type=F32))
    eye = _eye(B_KEY_DIM)
    t_iota = lax.broadcasted_iota(jnp.int32, (GLA_SUB, B_KEY_DIM), 0)

    for h in range(B_HEADS):
        dk = slice(h * B_KEY_DIM, (h + 1) * B_KEY_DIM)
        dv = slice(h * B_VAL_DIM, (h + 1) * B_VAL_DIM)
        q = q_ref[:, dk] * (B_KEY_DIM ** -0.5)
        k = k_ref[:, dk]
        v = v_ref[:, dv]
        g = gcum[:, dk]
        vb = v.astype(BF16)
        state = s_ref[h]
        o_inter = jnp.dot((q * jnp.exp(g)).astype(BF16), state.astype(BF16), preferred_element_type=F32)
        parts = []
        for i in range(GLA_CHUNK // GLA_SUB):
            lo_r, hi_r = GLA_SUB * i, GLA_SUB * (i + 1)
            qi, ki, vi, gi = q[lo_r:hi_r], k[lo_r:hi_r], v[lo_r:hi_r], g[lo_r:hi_r]
            acc = jnp.zeros((GLA_SUB, B_VAL_DIM), F32)
            for s in range(GLA_SUB):
                dec = jnp.exp(jnp.where(t_iota >= s, gi - gi[s:s + 1, :], -jnp.inf))
                col = jnp.sum(qi * ki[s:s + 1, :] * dec, axis=-1, keepdims=True)
                acc = acc + col * vi[s:s + 1, :]
            if i > 0:
                gref = g[lo_r - 1:lo_r, :]
                qd = (qi * jnp.exp(gi - gref)).astype(BF16)
                kd = (k[:lo_r] * jnp.exp(gref - g[:lo_r])).astype(BF16)
                sc = lax.dot_general(qd, kd, (((1,), (1,)), ((), ())), preferred_element_type=F32)
                acc = acc + jnp.dot(sc.astype(BF16), vb[:lo_r], preferred_element_type=F32)
            parts.append(acc)
        o = o_inter + jnp.concatenate(parts, axis=0)

        g_last = g[GLA_CHUNK - 1:GLA_CHUNK, :]
        kdec = k * jnp.exp(g_last - g)
        upd = jnp.dot(kdec.T.astype(BF16), vb, preferred_element_type=F32)
        s_ref[h] = _row_to_col(jnp.exp(g_last), eye) * state + upd

        ms = jnp.mean(o * o, axis=-1, keepdims=True)
        on = (o * lax.rsqrt(ms + EPS)) * gg_ref[:, dv]
        r = r_ref[:, dv]
        o_ref[:, dv] = (r * jax.nn.sigmoid(r)) * on

    @pl.when(c == pl.num_programs(1) - 1)
    def _():
        st_ref[0] = s_ref[...]


def _gla_prompt(z, zg, wa_pad, ba, gg, mix):
    n_chunks = SEQ // GLA_CHUNK
    row = lambda b, c: b * n_chunks + c
    return pl.pallas_call(
        _gla_body,
        grid=(BATCH, n_chunks),
        in_specs=[pl.BlockSpec((GLA_CHUNK, B_KEY_WIDTH), lambda b, c: (row(b, c), 4)),
                  pl.BlockSpec((GLA_CHUNK, B_KEY_WIDTH), lambda b, c: (row(b, c), 5)),
                  pl.BlockSpec((GLA_CHUNK, B_VAL_WIDTH), lambda b, c: (row(b, c), 3)),
                  pl.BlockSpec((GLA_CHUNK, B_VAL_WIDTH), lambda b, c: (row(b, c), 4)),
                  pl.BlockSpec((GLA_CHUNK, LANES), lambda b, c: (row(b, c), 0)),
                  pl.BlockSpec((LANES, B_KEY_WIDTH), lambda b, c: (0, 0)),
                  pl.BlockSpec((1, B_KEY_WIDTH), lambda b, c: (0, 0)),
                  pl.BlockSpec((1, B_VAL_WIDTH), lambda b, c: (0, 0)),
                  pl.BlockSpec(memory_space=pl.ANY)],
        out_specs=[pl.BlockSpec((GLA_CHUNK, B_VAL_WIDTH), lambda b, c: (row(b, c), 1)),
                   pl.BlockSpec((1, B_HEADS, B_KEY_DIM, B_VAL_DIM), lambda b, c: (b, 0, 0, 0))],
        out_shape=[jax.ShapeDtypeStruct((T_PAD, D_MODEL), F32),
                   jax.ShapeDtypeStruct((BATCH, B_HEADS, B_KEY_DIM, B_VAL_DIM), F32)],
        scratch_shapes=[pltpu.VMEM((B_HEADS, B_KEY_DIM, B_VAL_DIM), F32)],
        input_output_aliases={8: 0},
        compiler_params=_params("arbitrary", "arbitrary"),
        name="gla_prompt",
    )(z, z, z, z, zg, wa_pad, ba, gg, mix)


SAMPLE_ROWS = 8


def _smix_body(z_ref, zg_ref, st_ref, wa_ref, ba_ref, gs_ref, gg_ref, w00_ref, b00_ref, mix_ref,
               o_ref, cv_ref, so_ref):
    del mix_ref
    u = _gelu(z_ref[:, 0:A_WIDTH])
    v = _gelu(z_ref[:, A_WIDTH:2 * A_WIDTH])
    for h in range(A_HEADS):
        sl = slice(h * LANES, (h + 1) * LANES)
        vh = v[:, sl]
        ms = jnp.mean(vh * vh, axis=-1, keepdims=True)
        va = (vh * lax.rsqrt(ms + EPS)) * gs_ref[:, sl]
        cv_ref[:, sl] = va
        o_ref[:, sl] = u[:, sl] * (w00_ref[:, sl] * va + b00_ref[:, sl])

    x = jnp.dot(zg_ref[...].astype(BF16), wa_ref[...], preferred_element_type=F32) + ba_ref[...]
    a = jnp.exp(_log_sigmoid(x) * (1.0 / GATE_TAU))
    q0, k0, v0, r0 = 2 * A_WIDTH, 2 * A_WIDTH + B_KEY_WIDTH, 2 * A_WIDTH + 2 * B_KEY_WIDTH, Z_COLS - B_VAL_WIDTH
    eye = _eye(B_KEY_DIM)
    for b in range(SAMPLE_ROWS):
        for h in range(B_HEADS):
            dk = slice(h * B_KEY_DIM, (h + 1) * B_KEY_DIM)
            a_col = _row_to_col(a[b:b + 1, dk], eye)
            k_col = _row_to_col(z_ref[b:b + 1, k0 + h * B_KEY_DIM:k0 + (h + 1) * B_KEY_DIM], eye)
            q_col = _row_to_col(z_ref[b:b + 1, q0 + h * B_KEY_DIM:q0 + (h + 1) * B_KEY_DIM], eye) * (B_KEY_DIM ** -0.5)
            vrow = z_ref[b:b + 1, v0 + h * B_VAL_DIM:v0 + (h + 1) * B_VAL_DIM]
            s_new = a_col * st_ref[b, h] + k_col * vrow
            so_ref[b, h] = s_new
            o = jnp.sum(q_col * s_new, axis=0, keepdims=True)
            ms = jnp.mean(o * o, axis=-1, keepdims=True)
            on = (o * lax.rsqrt(ms + EPS)) * gg_ref[:, h * B_VAL_DIM:(h + 1) * B_VAL_DIM]
            r = z_ref[b:b + 1, r0 + h * B_VAL_DIM:r0 + (h + 1) * B_VAL_DIM]
            o_ref[b:b + 1, A_WIDTH + h * B_VAL_DIM:A_WIDTH + (h + 1) * B_VAL_DIM] = (r * jax.nn.sigmoid(r)) * on


def _sample_mixers(z, zg, state, wa_pad, ba, gs, gg, w00, b00, mix):
    first = T_PROMPT // SAMPLE_ROWS
    return pl.pallas_call(
        _smix_body,
        grid=(T_SAMPLE // SAMPLE_ROWS,),
        in_specs=[pl.BlockSpec((SAMPLE_ROWS, Z_COLS), lambda i: (first + i, 0)),
                  pl.BlockSpec((SAMPLE_ROWS, LANES), lambda i: (first + i, 0)),
                  pl.BlockSpec((SAMPLE_ROWS, B_HEADS, B_KEY_DIM, B_VAL_DIM), lambda i: (i, 0, 0, 0)),
                  pl.BlockSpec((LANES, B_KEY_WIDTH), lambda i: (0, 0)),
                  pl.BlockSpec((1, B_KEY_WIDTH), lambda i: (0, 0)),
                  pl.BlockSpec((1, A_WIDTH), lambda i: (0, 0)),
                  pl.BlockSpec((1, B_VAL_WIDTH), lambda i: (0, 0)),
                  pl.BlockSpec((1, A_WIDTH), lambda i: (0, 0)),
                  pl.BlockSpec((1, A_WIDTH), lambda i: (0, 0)),
                  pl.BlockSpec(memory_space=pl.ANY)],
        out_specs=[pl.BlockSpec((SAMPLE_ROWS, D_MODEL), lambda i: (first + i, 0)),
                   pl.BlockSpec((SAMPLE_ROWS, A_WIDTH), lambda i: (i, 0)),
                   pl.BlockSpec((SAMPLE_ROWS, B_HEADS, B_KEY_DIM, B_VAL_DIM), lambda i: (i, 0, 0, 0))],
        out_shape=[jax.ShapeDtypeStruct((T_PAD, D_MODEL), F32),
                   jax.ShapeDtypeStruct((T_SAMPLE, A_WIDTH), F32),
                   jax.ShapeDtypeStruct((T_SAMPLE, B_HEADS, B_KEY_DIM, B_VAL_DIM), F32)],
        input_output_aliases={9: 0},
        compiler_params=_params("parallel"),
        name="sample_mixers",
    )(z, zg, state, wa_pad, ba, gs, gg, w00, b00, mix)


ATT_TILE = 512


def _attn_prompt_body(q_ref, k_ref, v_ref, att_ref, o_ref):
    del att_ref
    for h in range(X_HEADS):
        sl = slice(h * X_HEAD_DIM, (h + 1) * X_HEAD_DIM)
        q = q_ref[:, sl]
        k = k_ref[0, :, sl].astype(BF16)
        s = lax.dot_general(q, k, (((1,), (1,)), ((), ())), preferred_element_type=F32) * (X_HEAD_DIM ** -0.5)
        s = s - jnp.max(s, axis=-1, keepdims=True)
        e = jnp.exp(s)
        p = e / jnp.sum(e, axis=-1, keepdims=True)
        o_ref[:, sl] = jnp.dot(p.astype(BF16), v_ref[0, :, sl].astype(BF16), preferred_element_type=F32).astype(BF16)


def _attn_prompt(qx, mk, mv, att):
    tiles = SEQ // ATT_TILE
    return pl.pallas_call(
        _attn_prompt_body,
        grid=(BATCH, tiles),
        in_specs=[pl.BlockSpec((ATT_TILE, D_MODEL), lambda b, t: (b * tiles + t, 0)),
                  pl.BlockSpec((1, MEM_LEN, D_MODEL), lambda b, t: (b, 0, 0)),
                  pl.BlockSpec((1, MEM_LEN, D_MODEL), lambda b, t: (b, 0, 0)),
                  pl.BlockSpec(memory_space=pl.ANY)],
        out_specs=pl.BlockSpec((ATT_TILE, D_MODEL), lambda b, t: (b * tiles + t, 0)),
        out_shape=jax.ShapeDtypeStruct((T_PAD, D_MODEL), BF16),
        input_output_aliases={3: 0},
        compiler_params=_params("parallel", "parallel"),
        name="attn_prompt",
    )(qx, mk, mv, att)


ATT_SEQS = 2


def _attn_sample_body(q_ref, k_ref, v_ref, o_ref):
    for b in range(ATT_SEQS):
        q = q_ref[b:b + 1].astype(F32)
        s = jnp.sum(k_ref[0, b] * q, axis=-1, keepdims=True) * (X_HEAD_DIM ** -0.5)
        s = s - jnp.max(s, axis=0, keepdims=True)
        e = jnp.exp(s)
        p = e / jnp.sum(e, axis=0, keepdims=True)
        o_ref[b] = jnp.sum(p * v_ref[0, b], axis=0).astype(BF16)


def _attn_sample(q, ck, cv):
    kv_spec = pl.BlockSpec((1, ATT_SEQS, MEM_LEN, X_HEADS, X_HEAD_DIM), lambda i: (0, i, 0, 0, 0))
    return pl.pallas_call(
        _attn_sample_body,
        grid=(T_SAMPLE // ATT_SEQS,),
        in_specs=[pl.BlockSpec((ATT_SEQS, X_HEADS, X_HEAD_DIM), lambda i: (i, 0, 0)), kv_spec, kv_spec],
        out_specs=pl.BlockSpec((ATT_SEQS, X_HEADS, X_HEAD_DIM), lambda i: (i, 0, 0)),
        out_shape=jax.ShapeDtypeStruct((T_SAMPLE, X_HEADS, X_HEAD_DIM), BF16),
        compiler_params=_params("parallel"),
        name="attn_sample",
    )(q, ck, cv)


def _top16_distinct(s):
    rows = lax.broadcasted_iota(jnp.int32, (PEER_TOPK, LANES), 0)
    vals = jnp.full((PEER_TOPK, LANES), -jnp.inf, F32)
    cnts = jnp.zeros((PEER_TOPK, LANES), F32)
    work = s
    for kk in range(PEER_TOPK):
        m = jnp.max(work, axis=0, keepdims=True)
        eq = work == m
        cnt = jnp.sum(jnp.where(eq, 1.0, 0.0), axis=0, keepdims=True)
        vals = jnp.where(rows == kk, m, vals)
        cnts = jnp.where(rows == kk, cnt, cnts)
        work = jnp.where(eq, -jnp.inf, work)
    return vals, cnts


_CAND_ROWS = tuple((a, PEER_TOPK // (a + 1)) for a in range(1, 8))


def _peer_route_body(pq_ref, keys_ref, s0_ref, s1_ref, a_ref, b_ref, th_ref):
    rows8 = lax.broadcasted_iota(jnp.int32, (8, LANES), 0)

    def head(h, carry):
        q0 = pq_ref[:, pl.ds(pl.multiple_of(h * 2 * LANES, LANES), LANES)]
        q1 = pq_ref[:, pl.ds(pl.multiple_of(h * 2 * LANES + LANES, LANES), LANES)]
        nt = (((1,), (1,)), ((), ()))
        s0 = lax.dot_general(keys_ref[2 * h], q0, nt, preferred_element_type=F32)
        s1 = lax.dot_general(keys_ref[2 * h + 1], q1, nt, preferred_element_type=F32)
        v0, n0 = _top16_distinct(s0)
        v1, n1 = _top16_distinct(s1)
        cs = [v0[0:1] + v1]
        ws = [n0[0:1] * n1]
        for a, nb in _CAND_ROWS:
            ok = rows8 < nb
            cs.append(jnp.where(ok, v0[a:a + 1] + v1[0:8], -jnp.inf))
            ws.append(jnp.where(ok, n0[a:a + 1] * n1[0:8], 0.0))
        cs.append(v0[8:16] + v1[0:1])
        ws.append(n0[8:16] * n1[0:1])
        cand = jnp.concatenate(cs, axis=0)
        wgt = jnp.concatenate(ws, axis=0)
        starts = [0] + [16 + 8 * i for i in range(len(_CAND_ROWS))] + [16 + 8 * len(_CAND_ROWS)]
        counts = [PEER_TOPK] + [nb for _, nb in _CAND_ROWS] + [8]
        above = jnp.zeros_like(cand)
        for st, cn in zip(starts, counts):
            for rr in range(st, st + cn):
                above = above + jnp.where(cand[rr:rr + 1] > cand, wgt[rr:rr + 1], 0.0)
        theta = jnp.min(jnp.where(above <= PEER_TOPK - 1.0, cand, jnp.inf), axis=0, keepdims=True)
        top = cand[0:1]
        zsum = jnp.sum(jnp.where(cand >= theta, wgt * jnp.exp(cand - top), 0.0), axis=0, keepdims=True)
        s0_ref[h] = s0
        s1_ref[h] = s1
        a_ref[h] = jnp.exp(s0 - v0[0:1]) / zsum
        b_ref[h] = jnp.exp(s1 - v1[0:1])
        th_ref[h] = theta
        return carry

    lax.fori_loop(0, PEER_HEADS, head, 0)


def _peer_route(pq, keys):
    big = jax.ShapeDtypeStruct((PEER_HEADS, N_KEYS, T_PAD), F32)
    big_spec = pl.BlockSpec((PEER_HEADS, N_KEYS, LANES), lambda i: (0, 0, i))
    return pl.pallas_call(
        _peer_route_body,
        grid=(T_PAD // LANES,),
        in_specs=[pl.BlockSpec((LANES, D_MODEL), lambda i: (i, 0)),
                  pl.BlockSpec((2 * PEER_HEADS, N_KEYS, LANES), lambda i: (0, 0, 0))],
        out_specs=[big_spec, big_spec, big_spec, big_spec,
                   pl.BlockSpec((PEER_HEADS, 1, LANES), lambda i: (0, 0, i))],
        out_shape=[big, big, big, big, jax.ShapeDtypeStruct((PEER_HEADS, 1, T_PAD), F32)],
        compiler_params=_params("parallel"),
        name="peer_route",
    )(pq, keys)


N_EXPERT_BLOCKS = N_EXPERTS // EXPERT_TILE
N_TOKEN_BLOCKS = T_PAD // TOKEN_TILE
N_PAIRS = N_TOKEN_BLOCKS * N_EXPERT_BLOCKS
PIPE_LAG = 2
GATE_ROWS = 32
N_GROUPS = EXPERT_TILE // N_KEYS


def _pair(n, lag):
    c = jnp.clip(n - lag, 0, N_PAIRS - 1)
    return c // N_EXPERT_BLOCKS, c % N_EXPERT_BLOCKS


def _peer_dense_body(h_ref, u_ref, vt_ref, s0_ref, s1_ref, a_ref, b_ref, th_ref, o_ref, s_even, s_odd, a_even, a_odd):
    n = pl.program_id(0)

    @pl.when(n == 0)
    def _():
        for ref in (s_even, s_odd, a_even, a_odd):
            ref[...] = jnp.zeros_like(ref)

    @pl.when(jnp.logical_or(n == 0, jnp.maximum(n - PIPE_LAG, 0) % N_EXPERT_BLOCKS == 0))
    def _():
        o_ref[...] = jnp.zeros_like(o_ref)

    def run(s_w, s_r, a_w, a_r):
        def cols(nn, carry):
            t0 = pl.multiple_of(nn * MXU_DIM, MXU_DIM)
            tok = pl.ds(t0, MXU_DIM)
            o_ref[:, tok] += jnp.dot(vt_ref[...], a_r[:, tok], preferred_element_type=F32)
            s_w[:, tok] = jnp.dot(u_ref[...], h_ref[:, tok], preferred_element_type=F32)
            for g in range(N_GROUPS):
                for lt in range(MXU_DIM // LANES):
                    lanes = pl.ds(pl.multiple_of(t0 + lt * LANES, LANES), LANES)
                    for j0 in range(0, N_KEYS, GATE_ROWS):
                        keys = slice(j0, j0 + GATE_ROWS)
                        rows = slice(g * N_KEYS + j0, g * N_KEYS + j0 + GATE_ROWS)
                        gate = jnp.zeros((GATE_ROWS, LANES), F32)
                        for h in range(PEER_HEADS):
                            c = s0_ref[h, g:g + 1, lanes] + s1_ref[h, keys, lanes]
                            w = a_ref[h, g:g + 1, lanes] * b_ref[h, keys, lanes]
                            gate = gate + jnp.where(c >= th_ref[h, :, lanes], w, 0.0)
                        a_w[rows, lanes] = (_gelu(s_r[rows, lanes]) * gate).astype(BF16)
            return carry

        lax.fori_loop(0, TOKEN_TILE // MXU_DIM, cols, 0)

    @pl.when(n % 2 == 0)
    def _():
        run(s_even, s_odd, a_odd, a_even)

    @pl.when(n % 2 == 1)
    def _():
        run(s_odd, s_even, a_even, a_odd)


def _peer_dense(hn_t, u_bf, vt_bf, s0, s1, a, b, th):
    once = pl.Buffered(1)
    route_spec = pl.BlockSpec((PEER_HEADS, N_KEYS, TOKEN_TILE), lambda n: (0, 0, _pair(n, 1)[0]), pipeline_mode=once)
    row_spec = pl.BlockSpec((PEER_HEADS, EXPERT_TILE // N_KEYS, TOKEN_TILE),
                            lambda n: (0, _pair(n, 1)[1], _pair(n, 1)[0]))
    return pl.pallas_call(
        _peer_dense_body,
        grid=(N_PAIRS + PIPE_LAG,),
        in_specs=[pl.BlockSpec((D_MODEL, TOKEN_TILE), lambda n: (0, _pair(n, 0)[0]), pipeline_mode=once),
                  pl.BlockSpec((EXPERT_TILE, D_MODEL), lambda n: (_pair(n, 0)[1], 0)),
                  pl.BlockSpec((D_MODEL, EXPERT_TILE), lambda n: (0, _pair(n, PIPE_LAG)[1])),
                  row_spec, route_spec, row_spec, route_spec,
                  pl.BlockSpec((PEER_HEADS, 1, TOKEN_TILE), lambda n: (0, 0, _pair(n, 1)[0]))],
        out_specs=pl.BlockSpec((D_MODEL, TOKEN_TILE), lambda n: (0, _pair(n, PIPE_LAG)[0])),
        out_shape=jax.ShapeDtypeStruct((D_MODEL, T_PAD), F32),
        scratch_shapes=[pltpu.VMEM((EXPERT_TILE, TOKEN_TILE), F32), pltpu.VMEM((EXPERT_TILE, TOKEN_TILE), F32),
                        pltpu.VMEM((EXPERT_TILE, TOKEN_TILE), BF16), pltpu.VMEM((EXPERT_TILE, TOKEN_TILE), BF16)],
        compiler_params=_params("arbitrary"),
        name="peer_dense",
    )(hn_t, u_bf, vt_bf, s0, s1, a, b, th)


def _final_body(x_ref, pt_ref, g_ref, o_ref):
    x = x_ref[...] + pt_ref[...].T
    ms = jnp.mean(x * x, axis=-1, keepdims=True)
    o_ref[...] = (x * lax.rsqrt(ms + EPS)) * g_ref[...]


def _final(x2, peer_t, g, first_tile, n_tiles):
    return pl.pallas_call(
        _final_body,
        grid=(n_tiles,),
        in_specs=[pl.BlockSpec((LANES, D_MODEL), lambda i: (first_tile + i, 0)),
                  pl.BlockSpec((D_MODEL, LANES), lambda i: (0, first_tile + i)),
                  pl.BlockSpec((1, D_MODEL), lambda i: (0, 0))],
        out_specs=pl.BlockSpec((LANES, D_MODEL), lambda i: (i, 0)),
        out_shape=jax.ShapeDtypeStruct((n_tiles * LANES, D_MODEL), F32),
        compiler_params=_params("parallel"),
        name="final_norm",
    )(x2, peer_t, g.reshape(1, D_MODEL))


def kernel(x_prompt, x_sample, mem_prompt, cache_mem_k, cache_mem_v, state_gla, g_mix, w_in, w_alpha, b_alpha, w_s, b_s, g_sgu, g_gla, w_out, g_mem, w_xk, w_xv, g_xq, w_xq, w_xo, g_ffn, w_pq, sub_keys, u_emb, v_emb, g_final):
    w_in_main = w_in[0, :, :Z_COLS].astype(BF16)
    w_in_gate = jnp.pad(w_in[0, :, Z_COLS:], ((0, 0), (0, LANES - GATE_RANK))).astype(BF16)
    wa_pad = jnp.pad(w_alpha[0], ((0, LANES - GATE_RANK), (0, 0))).astype(BF16)
    ba = b_alpha[0].reshape(1, B_KEY_WIDTH)
    causal = jnp.tril(jnp.ones((SGU_CHUNK, SGU_CHUNK), dtype=bool))
    ws_tril = jnp.where(causal, w_s[0], 0).astype(BF16)
    bs_t = b_s[0].T
    w00 = jnp.repeat(w_s[0, :, 0, 0], LANES).reshape(1, A_WIDTH)
    b00 = jnp.repeat(b_s[0, :, 0], LANES).reshape(1, A_WIDTH)
    gs = g_sgu[0].reshape(1, A_WIDTH)
    gg = g_gla[0].reshape(1, B_VAL_WIDTH)
    keys = sub_keys[0].reshape(2 * PEER_HEADS, N_KEYS, LANES).astype(BF16)
    u_bf = u_emb[0].astype(BF16)
    vt_bf = v_emb[0].T.astype(BF16)

    x_all = jnp.concatenate([x_prompt.reshape(T_PROMPT, D_MODEL), x_sample.reshape(T_SAMPLE, D_MODEL),
                             jnp.zeros((T_PAD - T_ALL, D_MODEL), F32)], axis=0)

    mem = mem_prompt.reshape(BATCH * MEM_LEN, D_MODEL)
    mk, = _nmm(mem, g_mem[0], w_xk[0].astype(BF16), tm=512, tn=1024)
    mv, = _nmm(mem, g_mem[0], w_xv[0].astype(BF16), tm=512, tn=1024)

    z, zg = _nmm(x_all, g_mix[0], w_in_main, tm=TOKEN_TILE, tn=1024, w_aux=w_in_gate)
    mix, cv_p = _sgu_prompt(z, ws_tril, bs_t, gs, jnp.zeros((T_PAD, D_MODEL), F32))
    mix, st_p = _gla_prompt(z, zg, wa_pad, ba, gg, mix)
    mix, cv_s, st_s = _sample_mixers(z, zg, state_gla[0], wa_pad, ba, gs, gg, w00, b00, mix)
    x1 = _mmres(mix, w_out[0].astype(BF16), x_all, tm=TOKEN_TILE)

    qx, = _nmm(x1, g_xq[0], w_xq[0].astype(BF16), tm=TOKEN_TILE, tn=1024, out_dtype=BF16)
    att = _attn_prompt(qx, mk.reshape(BATCH, MEM_LEN, D_MODEL), mv.reshape(BATCH, MEM_LEN, D_MODEL),
                       jnp.zeros((T_PAD, D_MODEL), BF16))
    att_s = _attn_sample(qx[T_PROMPT:T_ALL].reshape(T_SAMPLE, X_HEADS, X_HEAD_DIM), cache_mem_k, cache_mem_v)
    att = lax.dynamic_update_slice(att, att_s.reshape(T_SAMPLE, D_MODEL), (T_PROMPT, 0))
    x2 = _mmres(att, w_xo[0].astype(BF16), x1, tm=TOKEN_TILE)

    pq, hn = _nmm(x2, g_ffn[0], w_pq[0].astype(BF16), tm=TOKEN_TILE, tn=1024, emit_xn=True, out_dtype=BF16)
    s0, s1, a, b, th = _peer_route(pq, keys)
    peer_t = _peer_dense(hn.T, u_bf, vt_bf, s0, s1, a, b, th)

    y_p = _final(x2, peer_t, g_final, 0, T_PROMPT // LANES)
    y_s = _final(x2, peer_t, g_final, T_PROMPT // LANES, T_SAMPLE // LANES)

    return (y_p.reshape(BATCH, SEQ, D_MODEL),
            y_s.reshape(DEC_BATCH, 1, D_MODEL),
            mk.reshape(1, BATCH, MEM_LEN, X_HEADS, X_HEAD_DIM),
            mv.reshape(1, BATCH, MEM_LEN, X_HEADS, X_HEAD_DIM),
            st_p.reshape(1, BATCH, B_HEADS, B_KEY_DIM, B_VAL_DIM),
            st_s.reshape(1, DEC_BATCH, B_HEADS, B_KEY_DIM, B_VAL_DIM),
            cv_p.reshape(1, BATCH, SGU_CHUNK, A_HEADS, LANES),
            cv_s.reshape(1, DEC_BATCH, 1, A_HEADS, LANES))
```

```python
import functools

import jax
import jax.numpy as jnp
from jax import lax
from jax.experimental import pallas as pl
from jax.experimental.pallas import tpu as pltpu

F32 = jnp.float32
BF16 = jnp.bfloat16

D_MODEL = 2048
BATCH = 4
SEQ = 2048
DEC_BATCH = 128
T_PROMPT = BATCH * SEQ
T_SAMPLE = DEC_BATCH
T_ALL = T_PROMPT + T_SAMPLE

A_HEADS = 8
A_WIDTH = 1024
SGU_CHUNK = 128
B_HEADS = 4
B_KEY_DIM = 128
B_VAL_DIM = 256
B_KEY_WIDTH = 512
B_VAL_WIDTH = 1024
GATE_RANK = 16
GATE_TAU = 16.0
GLA_CHUNK = 64
GLA_SUB = 16
Z_COLS = 2 * A_WIDTH + 2 * B_KEY_WIDTH + 2 * B_VAL_WIDTH
MEM_LEN = 256
X_HEADS = 4
X_HEAD_DIM = 512
N_KEYS = 128
N_EXPERTS = N_KEYS * N_KEYS
PEER_HEADS = 8
PEER_TOPK = 16
EPS = 1e-6

LANES = 128
MXU_DIM = 256
TOKEN_TILE = 3 * MXU_DIM
T_PAD = 11 * TOKEN_TILE
EXPERT_TILE = 1024
VMEM_LIMIT = 56 * 1024 * 1024


def _params(*sem, flags=None):
    return pltpu.CompilerParams(dimension_semantics=sem, vmem_limit_bytes=VMEM_LIMIT, flags=flags)


_GELU_2C = 2.0 * 0.7978845608028654
_LOG2E = 1.4426950408889634


def _gelu(x):
    p = (x * x) * (-_LOG2E * _GELU_2C * 0.044715) + (-_LOG2E * _GELU_2C)
    return x / (1.0 + jnp.exp2(x * p))


def _log_sigmoid(x):
    return jnp.minimum(x, 0.0) - jnp.log1p(jnp.exp(-jnp.abs(x)))


def _row_to_col(row, eye):
    return jnp.sum(jnp.where(eye, row, 0.0), axis=-1, keepdims=True)


def _eye(n):
    return lax.broadcasted_iota(jnp.int32, (n, n), 0) == lax.broadcasted_iota(jnp.int32, (n, n), 1)


def _nmm_body(*refs, has_aux, emit_xn):
    x_ref, g_ref, w_ref = refs[:3]
    rest = list(refs[3:])
    w2_ref = rest.pop(0) if has_aux else None
    o_ref = rest.pop(0)
    o2_ref = rest.pop(0) if has_aux else None
    xo_ref = rest.pop(0) if emit_xn else None
    xn_ref = rest.pop(0)

    @pl.when(pl.program_id(1) == 0)
    def _():
        x = x_ref[...]
        ms = jnp.mean(x * x, axis=-1, keepdims=True)
        xn = ((x * lax.rsqrt(ms + EPS)) * g_ref[...]).astype(BF16)
        xn_ref[...] = xn
        if has_aux:
            o2_ref[...] = jnp.dot(xn, w2_ref[...], preferred_element_type=F32)
        if emit_xn:
            xo_ref[...] = xn

    o_ref[...] = jnp.dot(xn_ref[...], w_ref[...], preferred_element_type=F32).astype(o_ref.dtype)


def _nmm(x, g, w, *, tm, tn, w_aux=None, emit_xn=False, out_dtype=F32):
    m, k = x.shape
    n = w.shape[1]
    in_specs = [pl.BlockSpec((tm, k), lambda i, j: (i, 0)),
                pl.BlockSpec((1, k), lambda i, j: (0, 0)),
                pl.BlockSpec((k, tn), lambda i, j: (0, j))]
    args = [x, g.reshape(1, k), w]
    out_shape = [jax.ShapeDtypeStruct((m, n), out_dtype)]
    out_specs = [pl.BlockSpec((tm, tn), lambda i, j: (i, j))]
    if w_aux is not None:
        in_specs.append(pl.BlockSpec((k, LANES), lambda i, j: (0, 0)))
        args.append(w_aux)
        out_shape.append(jax.ShapeDtypeStruct((m, LANES), F32))
        out_specs.append(pl.BlockSpec((tm, LANES), lambda i, j: (i, 0)))
    if emit_xn:
        out_shape.append(jax.ShapeDtypeStruct((m, k), BF16))
        out_specs.append(pl.BlockSpec((tm, k), lambda i, j: (i, 0)))
    return pl.pallas_call(
        functools.partial(_nmm_body, has_aux=w_aux is not None, emit_xn=emit_xn),
        grid=(m // tm, n // tn),
        in_specs=in_specs, out_specs=out_specs, out_shape=out_shape,
        scratch_shapes=[pltpu.VMEM((tm, k), BF16)],
        compiler_params=_params("parallel", "arbitrary"),
        name="nmm",
    )(*args)


def _mmres_body(x_ref, w_ref, r_ref, o_ref):
    o_ref[...] = r_ref[...] + jnp.dot(x_ref[...].astype(BF16), w_ref[...], preferred_element_type=F32)


def _mmres(x, w, res, *, tm):
    m, k = x.shape
    n = w.shape[1]
    return pl.pallas_call(
        _mmres_body,
        grid=(m // tm,),
        in_specs=[pl.BlockSpec((tm, k), lambda i: (i, 0)),
                  pl.BlockSpec((k, n), lambda i: (0, 0), pipeline_mode=pl.Buffered(1)),
                  pl.BlockSpec((tm, n), lambda i: (i, 0))],
        out_specs=pl.BlockSpec((tm, n), lambda i: (i, 0)),
        out_shape=jax.ShapeDtypeStruct((m, n), F32),
        compiler_params=_params("parallel"),
        name="mmres",
    )(x, w, res)


def _sgu_body(u_ref, v_ref, ws_ref, bs_ref, gs_ref, mix_ref, o_ref, cv_ref):
    del mix_ref
    for h in range(A_HEADS):
        sl = slice(h * LANES, (h + 1) * LANES)
        u = _gelu(u_ref[:, sl])
        v = _gelu(v_ref[:, sl])
        ms = jnp.mean(v * v, axis=-1, keepdims=True)
        va = (v * lax.rsqrt(ms + EPS)) * gs_ref[:, sl]
        mixed = jnp.dot(ws_ref[h], va.astype(BF16), preferred_element_type=F32) + bs_ref[:, h:h + 1]
        o_ref[:, sl] = u * mixed
        cv_ref[0, :, sl] = va


def _sgu_prompt(z, ws_tril, bs_t, gs, mix):
    n_chunks = SEQ // SGU_CHUNK
    return pl.pallas_call(
        _sgu_body,
        grid=(T_PROMPT // SGU_CHUNK,),
        in_specs=[pl.BlockSpec((SGU_CHUNK, A_WIDTH), lambda i: (i, 0)),
                  pl.BlockSpec((SGU_CHUNK, A_WIDTH), lambda i: (i, 1)),
                  pl.BlockSpec((A_HEADS, SGU_CHUNK, SGU_CHUNK), lambda i: (0, 0, 0)),
                  pl.BlockSpec((SGU_CHUNK, A_HEADS), lambda i: (0, 0)),
                  pl.BlockSpec((1, A_WIDTH), lambda i: (0, 0)),
                  pl.BlockSpec(memory_space=pl.ANY)],
        out_specs=[pl.BlockSpec((SGU_CHUNK, A_WIDTH), lambda i: (i, 0)),
                   pl.BlockSpec((1, SGU_CHUNK, A_WIDTH), lambda i: (i // n_chunks, 0, 0))],
        out_shape=[jax.ShapeDtypeStruct((T_PAD, D_MODEL), F32),
                   jax.ShapeDtypeStruct((BATCH, SGU_CHUNK, A_WIDTH), F32)],
        input_output_aliases={5: 0},
        compiler_params=_params("arbitrary"),
        name="sgu_prompt",
    )(z, z, ws_tril, bs_t, gs, mix)


def _gla_body(q_ref, k_ref, v_ref, r_ref, zg_ref, wa_ref, ba_ref, gg_ref, mix_ref, o_ref, st_ref, s_ref):
    del mix_ref
    c = pl.program_id(1)

    @pl.when(c == 0)
    def _():
        s_ref[...] = jnp.zeros_like(s_ref)

    x = jnp.dot(zg_ref[...].astype(BF16), wa_ref[...], preferred_element_type=F32) + ba_ref[...]
    lg = _log_sigmoid(x) * (1.0 / GATE_TAU)
    tri = (lax.broadcasted_iota(jnp.int32, (GLA_CHUNK, GLA_CHUNK), 1)
           <= lax.broadcasted_iota(jnp.int32, (GLA_CHUNK, GLA_CHUNK), 0)).astype(BF16)
    hi = lg.astype(BF16)
    r1 = lg - hi.astype(F32)
    mid = r1.astype(BF16)
    lo = (r1 - mid.astype(F32)).astype(BF16)
    gcum = (jnp.dot(tri, hi, preferred_element_type=F32) + jnp.dot(tri, mid, preferred_element_type=F32)
            + jnp.dot(tri, lo, preferred_element_type=F32))
    eye = _eye(B_KEY_DIM)
    t_iota = lax.broadcasted_iota(jnp.int32, (GLA_SUB // 2, B_KEY_DIM), 0)

    for h in range(B_HEADS):
        dk = slice(h * B_KEY_DIM, (h + 1) * B_KEY_DIM)
        dv = slice(h * B_VAL_DIM, (h + 1) * B_VAL_DIM)
        q = q_ref[:, dk] * (B_KEY_DIM ** -0.5)
        k = k_ref[:, dk]
        v = v_ref[:, dv]
        g = gcum[:, dk]
        vb = v.astype(BF16)
        state = s_ref[h]
        o_inter = jnp.dot((q * jnp.exp(g)).astype(BF16), state.astype(BF16), preferred_element_type=F32)
        parts = []
        for i in range(GLA_CHUNK // GLA_SUB):
            lo_r, hi_r = GLA_SUB * i, GLA_SUB * (i + 1)
            qi, ki, vi, gi = q[lo_r:hi_r], k[lo_r:hi_r], v[lo_r:hi_r], g[lo_r:hi_r]
            half = GLA_SUB // 2
            acc_lo = jnp.zeros((half, B_VAL_DIM), F32)
            acc_hi = jnp.zeros((half, B_VAL_DIM), F32)
            for s in range(GLA_SUB):
                k_row, v_row, g_row = ki[s:s + 1, :], vi[s:s + 1, :], gi[s:s + 1, :]
                if s < half:
                    dec = jnp.exp(jnp.where(t_iota >= s, gi[:half] - g_row, -jnp.inf))
                    col = jnp.sum(qi[:half] * k_row * dec, axis=-1, keepdims=True)
                    acc_lo = acc_lo + col * v_row
                    dec = jnp.exp(gi[half:] - g_row)
                else:
                    dec = jnp.exp(jnp.where(t_iota >= s - half, gi[half:] - g_row, -jnp.inf))
                col = jnp.sum(qi[half:] * k_row * dec, axis=-1, keepdims=True)
                acc_hi = acc_hi + col * v_row
            acc = jnp.concatenate([acc_lo, acc_hi], axis=0)
            if i > 0:
                gref = g[lo_r - 1:lo_r, :]
                qd = (qi * jnp.exp(gi - gref)).astype(BF16)
                kd = (k[:lo_r] * jnp.exp(gref - g[:lo_r])).astype(BF16)
                sc = lax.dot_general(qd, kd, (((1,), (1,)), ((), ())), preferred_element_type=F32)
                acc = acc + jnp.dot(sc.astype(BF16), vb[:lo_r], preferred_element_type=F32)
            parts.append(acc)
        o = o_inter + jnp.concatenate(parts, axis=0)

        g_last = g[GLA_CHUNK - 1:GLA_CHUNK, :]
        kdec = k * jnp.exp(g_last - g)
        upd = jnp.dot(kdec.T.astype(BF16), vb, preferred_element_type=F32)
        s_ref[h] = _row_to_col(jnp.exp(g_last), eye) * state + upd

        ms = jnp.mean(o * o, axis=-1, keepdims=True)
        on = (o * lax.rsqrt(ms + EPS)) * gg_ref[:, dv]
        r = r_ref[:, dv]
        o_ref[:, dv] = (r * jax.nn.sigmoid(r)) * on

    @pl.when(c == pl.num_programs(1) - 1)
    def _():
        st_ref[0] = s_ref[...]


def _gla_prompt(z, zg, wa_pad, ba, gg, mix):
    n_chunks = SEQ // GLA_CHUNK
    row = lambda b, c: b * n_chunks + c
    return pl.pallas_call(
        _gla_body,
        grid=(BATCH, n_chunks),
        in_specs=[pl.BlockSpec((GLA_CHUNK, B_KEY_WIDTH), lambda b, c: (row(b, c), 4)),
                  pl.BlockSpec((GLA_CHUNK, B_KEY_WIDTH), lambda b, c: (row(b, c), 5)),
                  pl.BlockSpec((GLA_CHUNK, B_VAL_WIDTH), lambda b, c: (row(b, c), 3)),
                  pl.BlockSpec((GLA_CHUNK, B_VAL_WIDTH), lambda b, c: (row(b, c), 4)),
                  pl.BlockSpec((GLA_CHUNK, LANES), lambda b, c: (row(b, c), 0)),
                  pl.BlockSpec((LANES, B_KEY_WIDTH), lambda b, c: (0, 0)),
                  pl.BlockSpec((1, B_KEY_WIDTH), lambda b, c: (0, 0)),
                  pl.BlockSpec((1, B_VAL_WIDTH), lambda b, c: (0, 0)),
                  pl.BlockSpec(memory_space=pl.ANY)],
        out_specs=[pl.BlockSpec((GLA_CHUNK, B_VAL_WIDTH), lambda b, c: (row(b, c), 1)),
                   pl.BlockSpec((1, B_HEADS, B_KEY_DIM, B_VAL_DIM), lambda b, c: (b, 0, 0, 0))],
        out_shape=[jax.ShapeDtypeStruct((T_PAD, D_MODEL), F32),
                   jax.ShapeDtypeStruct((BATCH, B_HEADS, B_KEY_DIM, B_VAL_DIM), F32)],
        scratch_shapes=[pltpu.VMEM((B_HEADS, B_KEY_DIM, B_VAL_DIM), F32)],
        input_output_aliases={8: 0},
        compiler_params=_params("arbitrary", "arbitrary"),
        name="gla_prompt",
    )(z, z, z, z, zg, wa_pad, ba, gg, mix)


SAMPLE_ROWS = 8


def _smix_body(z_ref, zg_ref, st_ref, wa_ref, ba_ref, gs_ref, gg_ref, w00_ref, b00_ref, mix_ref,
               o_ref, cv_ref, so_ref):
    del mix_ref
    u = _gelu(z_ref[:, 0:A_WIDTH])
    v = _gelu(z_ref[:, A_WIDTH:2 * A_WIDTH])
    for h in range(A_HEADS):
        sl = slice(h * LANES, (h + 1) * LANES)
        vh = v[:, sl]
        ms = jnp.mean(vh * vh, axis=-1, keepdims=True)
        va = (vh * lax.rsqrt(ms + EPS)) * gs_ref[:, sl]
        cv_ref[:, sl] = va
        o_ref[:, sl] = u[:, sl] * (w00_ref[:, sl] * va + b00_ref[:, sl])

    x = jnp.dot(zg_ref[...].astype(BF16), wa_ref[...], preferred_element_type=F32) + ba_ref[...]
    a = jnp.exp(_log_sigmoid(x) * (1.0 / GATE_TAU))
    q0, k0, v0, r0 = 2 * A_WIDTH, 2 * A_WIDTH + B_KEY_WIDTH, 2 * A_WIDTH + 2 * B_KEY_WIDTH, Z_COLS - B_VAL_WIDTH
    eye = _eye(B_KEY_DIM)
    for b in range(SAMPLE_ROWS):
        for h in range(B_HEADS):
            dk = slice(h * B_KEY_DIM, (h + 1) * B_KEY_DIM)
            a_col = _row_to_col(a[b:b + 1, dk], eye)
            k_col = _row_to_col(z_ref[b:b + 1, k0 + h * B_KEY_DIM:k0 + (h + 1) * B_KEY_DIM], eye)
            q_col = _row_to_col(z_ref[b:b + 1, q0 + h * B_KEY_DIM:q0 + (h + 1) * B_KEY_DIM], eye) * (B_KEY_DIM ** -0.5)
            vrow = z_ref[b:b + 1, v0 + h * B_VAL_DIM:v0 + (h + 1) * B_VAL_DIM]
            s_new = a_col * st_ref[b, h] + k_col * vrow
            so_ref[b, h] = s_new
            o = jnp.sum(q_col * s_new, axis=0, keepdims=True)
            ms = jnp.mean(o * o, axis=-1, keepdims=True)
            on = (o * lax.rsqrt(ms + EPS)) * gg_ref[:, h * B_VAL_DIM:(h + 1) * B_VAL_DIM]
            r = z_ref[b:b + 1, r0 + h * B_VAL_DIM:r0 + (h + 1) * B_VAL_DIM]
            o_ref[b:b + 1, A_WIDTH + h * B_VAL_DIM:A_WIDTH + (h + 1) * B_VAL_DIM] = (r * jax.nn.sigmoid(r)) * on


def _sample_mixers(z, zg, state, wa_pad, ba, gs, gg, w00, b00, mix):
    first = T_PROMPT // SAMPLE_ROWS
    return pl.pallas_call(
        _smix_body,
        grid=(T_SAMPLE // SAMPLE_ROWS,),
        in_specs=[pl.BlockSpec((SAMPLE_ROWS, Z_COLS), lambda i: (first + i, 0)),
                  pl.BlockSpec((SAMPLE_ROWS, LANES), lambda i: (first + i, 0)),
                  pl.BlockSpec((SAMPLE_ROWS, B_HEADS, B_KEY_DIM, B_VAL_DIM), lambda i: (i, 0, 0, 0)),
                  pl.BlockSpec((LANES, B_KEY_WIDTH), lambda i: (0, 0)),
                  pl.BlockSpec((1, B_KEY_WIDTH), lambda i: (0, 0)),
                  pl.BlockSpec((1, A_WIDTH), lambda i: (0, 0)),
                  pl.BlockSpec((1, B_VAL_WIDTH), lambda i: (0, 0)),
                  pl.BlockSpec((1, A_WIDTH), lambda i: (0, 0)),
                  pl.BlockSpec((1, A_WIDTH), lambda i: (0, 0)),
                  pl.BlockSpec(memory_space=pl.ANY)],
        out_specs=[pl.BlockSpec((SAMPLE_ROWS, D_MODEL), lambda i: (first + i, 0)),
                   pl.BlockSpec((SAMPLE_ROWS, A_WIDTH), lambda i: (i, 0)),
                   pl.BlockSpec((SAMPLE_ROWS, B_HEADS, B_KEY_DIM, B_VAL_DIM), lambda i: (i, 0, 0, 0))],
        out_shape=[jax.ShapeDtypeStruct((T_PAD, D_MODEL), F32),
                   jax.ShapeDtypeStruct((T_SAMPLE, A_WIDTH), F32),
                   jax.ShapeDtypeStruct((T_SAMPLE, B_HEADS, B_KEY_DIM, B_VAL_DIM), F32)],
        input_output_aliases={9: 0},
        compiler_params=_params("parallel"),
        name="sample_mixers",
    )(z, zg, state, wa_pad, ba, gs, gg, w00, b00, mix)


ATT_TILE = 512


def _attn_prompt_body(q_ref, k_ref, v_ref, att_ref, o_ref):
    del att_ref
    for h in range(X_HEADS):
        sl = slice(h * X_HEAD_DIM, (h + 1) * X_HEAD_DIM)
        q = q_ref[:, sl]
        k = k_ref[0, :, sl].astype(BF16)
        s = lax.dot_general(q, k, (((1,), (1,)), ((), ())), preferred_element_type=F32) * (X_HEAD_DIM ** -0.5)
        s = s - jnp.max(s, axis=-1, keepdims=True)
        e = jnp.exp(s)
        p = e / jnp.sum(e, axis=-1, keepdims=True)
        o_ref[:, sl] = jnp.dot(p.astype(BF16), v_ref[0, :, sl].astype(BF16), preferred_element_type=F32).astype(BF16)


def _attn_prompt(qx, mk, mv, att):
    tiles = SEQ // ATT_TILE
    return pl.pallas_call(
        _attn_prompt_body,
        grid=(BATCH, tiles),
        in_specs=[pl.BlockSpec((ATT_TILE, D_MODEL), lambda b, t: (b * tiles + t, 0)),
                  pl.BlockSpec((1, MEM_LEN, D_MODEL), lambda b, t: (b, 0, 0)),
                  pl.BlockSpec((1, MEM_LEN, D_MODEL), lambda b, t: (b, 0, 0)),
                  pl.BlockSpec(memory_space=pl.ANY)],
        out_specs=pl.BlockSpec((ATT_TILE, D_MODEL), lambda b, t: (b * tiles + t, 0)),
        out_shape=jax.ShapeDtypeStruct((T_PAD, D_MODEL), BF16),
        input_output_aliases={3: 0},
        compiler_params=_params("parallel", "parallel"),
        name="attn_prompt",
    )(qx, mk, mv, att)


ATT_SEQS = 2


def _attn_sample_body(q_ref, k_ref, v_ref, o_ref):
    for b in range(ATT_SEQS):
        q = q_ref[b:b + 1].astype(F32)
        s = jnp.sum(k_ref[0, b] * q, axis=-1, keepdims=True) * (X_HEAD_DIM ** -0.5)
        s = s - jnp.max(s, axis=0, keepdims=True)
        e = jnp.exp(s)
        p = e / jnp.sum(e, axis=0, keepdims=True)
        o_ref[b] = jnp.sum(p * v_ref[0, b], axis=0).astype(BF16)


def _attn_sample(q, ck, cv):
    kv_spec = pl.BlockSpec((1, ATT_SEQS, MEM_LEN, X_HEADS, X_HEAD_DIM), lambda i: (0, i, 0, 0, 0))
    return pl.pallas_call(
        _attn_sample_body,
        grid=(T_SAMPLE // ATT_SEQS,),
        in_specs=[pl.BlockSpec((ATT_SEQS, X_HEADS, X_HEAD_DIM), lambda i: (i, 0, 0)), kv_spec, kv_spec],
        out_specs=pl.BlockSpec((ATT_SEQS, X_HEADS, X_HEAD_DIM), lambda i: (i, 0, 0)),
        out_shape=jax.ShapeDtypeStruct((T_SAMPLE, X_HEADS, X_HEAD_DIM), BF16),
        compiler_params=_params("parallel"),
        name="attn_sample",
    )(q, ck, cv)


def _odd_even_merge_sort_pairs(n):
    pairs = []
    p = 1
    while p < n:
        k = p
        while k >= 1:
            for j in range(k % p, n - k, 2 * k):
                for i in range(min(k, n - j - k)):
                    if (i + j) // (2 * p) == (i + j + k) // (2 * p):
                        pairs.append((i + j, i + j + k))
            k //= 2
        p *= 2
    return pairs


_SORT16 = tuple(_odd_even_merge_sort_pairs(PEER_TOPK))


def _exchange(vs, i, j):
    vs[i], vs[j] = jnp.maximum(vs[i], vs[j]), jnp.minimum(vs[i], vs[j])


def _top16_sorted(s):
    vs = [s[8 * k:8 * (k + 1), :] for k in range(PEER_TOPK)]
    for i, j in _SORT16:
        _exchange(vs, i, j)
    for shift in (4, 2, 1):
        other = [pltpu.roll(v, shift, 0) for v in vs]
        vs = [jnp.maximum(vs[k], other[PEER_TOPK - 1 - k]) for k in range(PEER_TOPK)]
        for stride in (8, 4, 2, 1):
            for k in range(PEER_TOPK):
                if k & stride == 0:
                    _exchange(vs, k, k + stride)
    return vs


def _stack_rows(vs, rows8):
    out = vs[0]
    for k in range(1, 8):
        out = jnp.where(rows8 == k, vs[k], out)
    return out


_CAND_ROWS = tuple((a, PEER_TOPK // (a + 1)) for a in range(1, 8))


def _peer_route_body(pq_ref, keys_ref, s0_ref, s1_ref, a_ref, b_ref, th_ref):
    rows8 = lax.broadcasted_iota(jnp.int32, (8, LANES), 0)

    def head(h, carry):
        q0 = pq_ref[:, pl.ds(pl.multiple_of(h * 2 * LANES, LANES), LANES)]
        q1 = pq_ref[:, pl.ds(pl.multiple_of(h * 2 * LANES + LANES, LANES), LANES)]
        nt = (((1,), (1,)), ((), ()))
        s0 = lax.dot_general(keys_ref[2 * h], q0, nt, preferred_element_type=F32)
        s1 = lax.dot_general(keys_ref[2 * h + 1], q1, nt, preferred_element_type=F32)
        r0 = _top16_sorted(s0)
        r1 = _top16_sorted(s1)
        v1_lo = _stack_rows(r1[0:8], rows8)
        cs = [r0[0] + v1_lo, r0[0] + _stack_rows(r1[8:16], rows8)]
        for a, nb in _CAND_ROWS:
            cs.append(jnp.where(rows8 < nb, r0[a] + v1_lo, -jnp.inf))
        cs.append(_stack_rows(r0[8:16], rows8) + r1[0])
        cand = jnp.concatenate(cs, axis=0)
        starts = [0] + [16 + 8 * i for i in range(len(_CAND_ROWS))] + [16 + 8 * len(_CAND_ROWS)]
        counts = [PEER_TOPK] + [nb for _, nb in _CAND_ROWS] + [8]
        above = jnp.zeros_like(cand)
        for st, cn in zip(starts, counts):
            for rr in range(st, st + cn):
                above = above + jnp.where(cand[rr:rr + 1] > cand, 1.0, 0.0)
        theta = jnp.min(jnp.where(above <= PEER_TOPK - 1.0, cand, jnp.inf), axis=0, keepdims=True)
        top = cand[0:1]
        zsum = jnp.sum(jnp.where(cand >= theta, jnp.exp(cand - top), 0.0), axis=0, keepdims=True)
        s0_ref[h] = s0
        s1_ref[h] = s1
        a_ref[h] = jnp.exp(s0 - r0[0][0:1]) / zsum
        b_ref[h] = jnp.exp(s1 - r1[0][0:1])
        th_ref[h] = theta
        return carry

    lax.fori_loop(0, PEER_HEADS, head, 0)


def _peer_route(pq, keys):
    big = jax.ShapeDtypeStruct((PEER_HEADS, N_KEYS, T_PAD), F32)
    big_spec = pl.BlockSpec((PEER_HEADS, N_KEYS, LANES), lambda i: (0, 0, i))
    return pl.pallas_call(
        _peer_route_body,
        grid=(T_PAD // LANES,),
        in_specs=[pl.BlockSpec((LANES, D_MODEL), lambda i: (i, 0)),
                  pl.BlockSpec((2 * PEER_HEADS, N_KEYS, LANES), lambda i: (0, 0, 0))],
        out_specs=[big_spec, big_spec, big_spec, big_spec,
                   pl.BlockSpec((PEER_HEADS, 1, LANES), lambda i: (0, 0, i))],
        out_shape=[big, big, big, big, jax.ShapeDtypeStruct((PEER_HEADS, 1, T_PAD), F32)],
        compiler_params=_params("parallel"),
        name="peer_route",
    )(pq, keys)


N_EXPERT_BLOCKS = N_EXPERTS // EXPERT_TILE
N_TOKEN_BLOCKS = T_PAD // TOKEN_TILE
N_PAIRS = N_TOKEN_BLOCKS * N_EXPERT_BLOCKS
PIPE_LAG = 2
GATE_ROWS = 32
N_GROUPS = EXPERT_TILE // N_KEYS


def _pair(n, lag):
    c = jnp.clip(n - lag, 0, N_PAIRS - 1)
    return c // N_EXPERT_BLOCKS, c % N_EXPERT_BLOCKS


def _peer_dense_body(h_ref, u_ref, vt_ref, s0_ref, s1_ref, a_ref, b_ref, th_ref, o_ref, s_even, s_odd, a_even, a_odd):
    n = pl.program_id(0)

    @pl.when(n == 0)
    def _():
        for ref in (s_even, s_odd, a_even, a_odd):
            ref[...] = jnp.zeros_like(ref)

    @pl.when(jnp.logical_or(n == 0, jnp.maximum(n - PIPE_LAG, 0) % N_EXPERT_BLOCKS == 0))
    def _():
        o_ref[...] = jnp.zeros_like(o_ref)

    def run(s_w, s_r, a_w, a_r):
        def cols(nn, carry):
            t0 = pl.multiple_of(nn * MXU_DIM, MXU_DIM)
            tok = pl.ds(t0, MXU_DIM)
            o_ref[:, tok] += jnp.dot(vt_ref[...], a_r[:, tok], preferred_element_type=F32)
            s_w[:, tok] = jnp.dot(u_ref[...], h_ref[:, tok], preferred_element_type=F32)
            for g in range(N_GROUPS):
                for lt in range(MXU_DIM // LANES):
                    lanes = pl.ds(pl.multiple_of(t0 + lt * LANES, LANES), LANES)
                    for j0 in range(0, N_KEYS, GATE_ROWS):
                        keys = slice(j0, j0 + GATE_ROWS)
                        rows = slice(g * N_KEYS + j0, g * N_KEYS + j0 + GATE_ROWS)
                        gate = jnp.zeros((GATE_ROWS, LANES), F32)
                        for h in range(PEER_HEADS):
                            c = s0_ref[h, g:g + 1, lanes] + s1_ref[h, keys, lanes]
                            w = a_ref[h, g:g + 1, lanes] * b_ref[h, keys, lanes]
                            gate = gate + jnp.where(c >= th_ref[h, :, lanes], w, 0.0)
                        a_w[rows, lanes] = (_gelu(s_r[rows, lanes]) * gate).astype(BF16)
            return carry

        lax.fori_loop(0, TOKEN_TILE // MXU_DIM, cols, 0)

    @pl.when(n % 2 == 0)
    def _():
        run(s_even, s_odd, a_odd, a_even)

    @pl.when(n % 2 == 1)
    def _():
        run(s_odd, s_even, a_even, a_odd)


def _peer_dense(hn_t, u_bf, vt_bf, s0, s1, a, b, th):
    once = pl.Buffered(1)
    route_spec = pl.BlockSpec((PEER_HEADS, N_KEYS, TOKEN_TILE), lambda n: (0, 0, _pair(n, 1)[0]), pipeline_mode=once)
    row_spec = pl.BlockSpec((PEER_HEADS, EXPERT_TILE // N_KEYS, TOKEN_TILE),
                            lambda n: (0, _pair(n, 1)[1], _pair(n, 1)[0]))
    return pl.pallas_call(
        _peer_dense_body,
        grid=(N_PAIRS + PIPE_LAG,),
        in_specs=[pl.BlockSpec((D_MODEL, TOKEN_TILE), lambda n: (0, _pair(n, 0)[0]), pipeline_mode=once),
                  pl.BlockSpec((EXPERT_TILE, D_MODEL), lambda n: (_pair(n, 0)[1], 0)),
                  pl.BlockSpec((D_MODEL, EXPERT_TILE), lambda n: (0, _pair(n, PIPE_LAG)[1])),
                  row_spec, route_spec, row_spec, route_spec,
                  pl.BlockSpec((PEER_HEADS, 1, TOKEN_TILE), lambda n: (0, 0, _pair(n, 1)[0]))],
        out_specs=pl.BlockSpec((D_MODEL, TOKEN_TILE), lambda n: (0, _pair(n, PIPE_LAG)[0])),
        out_shape=jax.ShapeDtypeStruct((D_MODEL, T_PAD), F32),
        scratch_shapes=[pltpu.VMEM((EXPERT_TILE, TOKEN_TILE), F32), pltpu.VMEM((EXPERT_TILE, TOKEN_TILE), F32),
                        pltpu.VMEM((EXPERT_TILE, TOKEN_TILE), BF16), pltpu.VMEM((EXPERT_TILE, TOKEN_TILE), BF16)],
        compiler_params=_params("arbitrary"),
        name="peer_dense",
    )(hn_t, u_bf, vt_bf, s0, s1, a, b, th)


def _final_body(x_ref, pt_ref, g_ref, o_ref):
    x = x_ref[...] + pt_ref[...].T
    ms = jnp.mean(x * x, axis=-1, keepdims=True)
    o_ref[...] = (x * lax.rsqrt(ms + EPS)) * g_ref[...]


def _final(x2, peer_t, g, first_tile, n_tiles):
    return pl.pallas_call(
        _final_body,
        grid=(n_tiles,),
        in_specs=[pl.BlockSpec((LANES, D_MODEL), lambda i: (first_tile + i, 0)),
                  pl.BlockSpec((D_MODEL, LANES), lambda i: (0, first_tile + i)),
                  pl.BlockSpec((1, D_MODEL), lambda i: (0, 0))],
        out_specs=pl.BlockSpec((LANES, D_MODEL), lambda i: (i, 0)),
        out_shape=jax.ShapeDtypeStruct((n_tiles * LANES, D_MODEL), F32),
        compiler_params=_params("parallel"),
        name="final_norm",
    )(x2, peer_t, g.reshape(1, D_MODEL))


def kernel(x_prompt, x_sample, mem_prompt, cache_mem_k, cache_mem_v, state_gla, g_mix, w_in, w_alpha, b_alpha, w_s, b_s, g_sgu, g_gla, w_out, g_mem, w_xk, w_xv, g_xq, w_xq, w_xo, g_ffn, w_pq, sub_keys, u_emb, v_emb, g_final):
    w_in_main = w_in[0, :, :Z_COLS].astype(BF16)
    w_in_gate = jnp.pad(w_in[0, :, Z_COLS:], ((0, 0), (0, LANES - GATE_RANK))).astype(BF16)
    wa_pad = jnp.pad(w_alpha[0], ((0, LANES - GATE_RANK), (0, 0))).astype(BF16)
    ba = b_alpha[0].reshape(1, B_KEY_WIDTH)
    causal = jnp.tril(jnp.ones((SGU_CHUNK, SGU_CHUNK), dtype=bool))
    ws_tril = jnp.where(causal, w_s[0], 0).astype(BF16)
    bs_t = b_s[0].T
    w00 = jnp.repeat(w_s[0, :, 0, 0], LANES).reshape(1, A_WIDTH)
    b00 = jnp.repeat(b_s[0, :, 0], LANES).reshape(1, A_WIDTH)
    gs = g_sgu[0].reshape(1, A_WIDTH)
    gg = g_gla[0].reshape(1, B_VAL_WIDTH)
    keys = sub_keys[0].reshape(2 * PEER_HEADS, N_KEYS, LANES).astype(BF16)
    u_bf = u_emb[0].astype(BF16)
    vt_bf = v_emb[0].T.astype(BF16)

    x_all = jnp.concatenate([x_prompt.reshape(T_PROMPT, D_MODEL), x_sample.reshape(T_SAMPLE, D_MODEL),
                             jnp.zeros((T_PAD - T_ALL, D_MODEL), F32)], axis=0)

    mem = mem_prompt.reshape(BATCH * MEM_LEN, D_MODEL)
    mk, = _nmm(mem, g_mem[0], w_xk[0].astype(BF16), tm=512, tn=1024)
    mv, = _nmm(mem, g_mem[0], w_xv[0].astype(BF16), tm=512, tn=1024)

    z, zg = _nmm(x_all, g_mix[0], w_in_main, tm=TOKEN_TILE, tn=1280, w_aux=w_in_gate)
    mix, cv_p = _sgu_prompt(z, ws_tril, bs_t, gs, jnp.zeros((T_PAD, D_MODEL), F32))
    mix, st_p = _gla_prompt(z, zg, wa_pad, ba, gg, mix)
    mix, cv_s, st_s = _sample_mixers(z, zg, state_gla[0], wa_pad, ba, gs, gg, w00, b00, mix)
    x1 = _mmres(mix, w_out[0].astype(BF16), x_all, tm=TOKEN_TILE)

    qx, = _nmm(x1, g_xq[0], w_xq[0].astype(BF16), tm=TOKEN_TILE, tn=1024, out_dtype=BF16)
    att = _attn_prompt(qx, mk.reshape(BATCH, MEM_LEN, D_MODEL), mv.reshape(BATCH, MEM_LEN, D_MODEL),
                       jnp.zeros((T_PAD, D_MODEL), BF16))
    att_s = _attn_sample(qx[T_PROMPT:T_ALL].reshape(T_SAMPLE, X_HEADS, X_HEAD_DIM), cache_mem_k, cache_mem_v)
    att = lax.dynamic_update_slice(att, att_s.reshape(T_SAMPLE, D_MODEL), (T_PROMPT, 0))
    x2 = _mmres(att, w_xo[0].astype(BF16), x1, tm=TOKEN_TILE)

    pq, hn = _nmm(x2, g_ffn[0], w_pq[0].astype(BF16), tm=TOKEN_TILE, tn=1024, emit_xn=True, out_dtype=BF16)
    s0, s1, a, b, th = _peer_route(pq, keys)
    peer_t = _peer_dense(hn.T, u_bf, vt_bf, s0, s1, a, b, th)

    y_p = _final(x2, peer_t, g_final, 0, T_PROMPT // LANES)
    y_s = _final(x2, peer_t, g_final, T_PROMPT // LANES, T_SAMPLE // LANES)

    return (y_p.reshape(BATCH, SEQ, D_MODEL),
            y_s.reshape(DEC_BATCH, 1, D_MODEL),
            mk.reshape(1, BATCH, MEM_LEN, X_HEADS, X_HEAD_DIM),
            mv.reshape(1, BATCH, MEM_LEN, X_HEADS, X_HEAD_DIM),
            st_p.reshape(1, BATCH, B_HEADS, B_KEY_DIM, B_VAL_DIM),
            st_s.reshape(1, DEC_BATCH, B_HEADS, B_KEY_DIM, B_VAL_DIM),
            cv_p.reshape(1, BATCH, SGU_CHUNK, A_HEADS, LANES),
            cv_s.reshape(1, DEC_BATCH, 1, A_HEADS, LANES))
```

```python
import functools

import jax
import jax.numpy as jnp
from jax import lax
from jax.experimental import pallas as pl
from jax.experimental.pallas import tpu as pltpu

F32 = jnp.float32
BF16 = jnp.bfloat16

D_MODEL = 2048
BATCH = 4
SEQ = 2048
DEC_BATCH = 128
T_PROMPT = BATCH * SEQ
T_SAMPLE = DEC_BATCH
T_ALL = T_PROMPT + T_SAMPLE

A_HEADS = 8
A_WIDTH = 1024
SGU_CHUNK = 128
B_HEADS = 4
B_KEY_DIM = 128
B_VAL_DIM = 256
B_KEY_WIDTH = 512
B_VAL_WIDTH = 1024
GATE_RANK = 16
GATE_TAU = 16.0
GLA_CHUNK = 64
GLA_SUB = 16
Z_COLS = 2 * A_WIDTH + 2 * B_KEY_WIDTH + 2 * B_VAL_WIDTH
MEM_LEN = 256
X_HEADS = 4
X_HEAD_DIM = 512
N_KEYS = 128
N_EXPERTS = N_KEYS * N_KEYS
PEER_HEADS = 8
PEER_TOPK = 16
EPS = 1e-6

LANES = 128
MXU_DIM = 256
TOKEN_TILE = 3 * MXU_DIM
T_PAD = 11 * TOKEN_TILE
EXPERT_TILE = 1024
VMEM_LIMIT = 56 * 1024 * 1024


def _params(*sem, flags=None):
    return pltpu.CompilerParams(dimension_semantics=sem, vmem_limit_bytes=VMEM_LIMIT, flags=flags)


_GELU_2C = 2.0 * 0.7978845608028654
_LOG2E = 1.4426950408889634


def _gelu(x):
    p = (x * x) * (-_LOG2E * _GELU_2C * 0.044715) + (-_LOG2E * _GELU_2C)
    return x / (1.0 + jnp.exp2(x * p))


def _log_sigmoid(x):
    return jnp.minimum(x, 0.0) - jnp.log1p(jnp.exp(-jnp.abs(x)))


def _row_to_col(row, eye):
    return jnp.sum(jnp.where(eye, row, 0.0), axis=-1, keepdims=True)


def _eye(n):
    return lax.broadcasted_iota(jnp.int32, (n, n), 0) == lax.broadcasted_iota(jnp.int32, (n, n), 1)


def _nmm_body(*refs, has_aux, emit_xn):
    x_ref, g_ref, w_ref = refs[:3]
    rest = list(refs[3:])
    w2_ref = rest.pop(0) if has_aux else None
    o_ref = rest.pop(0)
    o2_ref = rest.pop(0) if has_aux else None
    xo_ref = rest.pop(0) if emit_xn else None
    xn_ref = rest.pop(0)

    @pl.when(pl.program_id(1) == 0)
    def _():
        x = x_ref[...]
        ms = jnp.mean(x * x, axis=-1, keepdims=True)
        xn = ((x * lax.rsqrt(ms + EPS)) * g_ref[...]).astype(BF16)
        xn_ref[...] = xn
        if has_aux:
            o2_ref[...] = jnp.dot(xn, w2_ref[...], preferred_element_type=F32)
        if emit_xn:
            xo_ref[...] = xn

    o_ref[...] = jnp.dot(xn_ref[...], w_ref[...], preferred_element_type=F32).astype(o_ref.dtype)


def _nmm(x, g, w, *, tm, tn, w_aux=None, emit_xn=False, out_dtype=F32):
    m, k = x.shape
    n = w.shape[1]
    in_specs = [pl.BlockSpec((tm, k), lambda i, j: (i, 0)),
                pl.BlockSpec((1, k), lambda i, j: (0, 0)),
                pl.BlockSpec((k, tn), lambda i, j: (0, j))]
    args = [x, g.reshape(1, k), w]
    out_shape = [jax.ShapeDtypeStruct((m, n), out_dtype)]
    out_specs = [pl.BlockSpec((tm, tn), lambda i, j: (i, j))]
    if w_aux is not None:
        in_specs.append(pl.BlockSpec((k, LANES), lambda i, j: (0, 0)))
        args.append(w_aux)
        out_shape.append(jax.ShapeDtypeStruct((m, LANES), F32))
        out_specs.append(pl.BlockSpec((tm, LANES), lambda i, j: (i, 0)))
    if emit_xn:
        out_shape.append(jax.ShapeDtypeStruct((m, k), BF16))
        out_specs.append(pl.BlockSpec((tm, k), lambda i, j: (i, 0)))
    return pl.pallas_call(
        functools.partial(_nmm_body, has_aux=w_aux is not None, emit_xn=emit_xn),
        grid=(m // tm, n // tn),
        in_specs=in_specs, out_specs=out_specs, out_shape=out_shape,
        scratch_shapes=[pltpu.VMEM((tm, k), BF16)],
        compiler_params=_params("parallel", "arbitrary"),
        name="nmm",
    )(*args)


def _mmres_body(x_ref, w_ref, r_ref, o_ref):
    o_ref[...] = r_ref[...] + jnp.dot(x_ref[...].astype(BF16), w_ref[...], preferred_element_type=F32)


def _mmres(x, w, res, *, tm):
    m, k = x.shape
    n = w.shape[1]
    return pl.pallas_call(
        _mmres_body,
        grid=(m // tm,),
        in_specs=[pl.BlockSpec((tm, k), lambda i: (i, 0)),
                  pl.BlockSpec((k, n), lambda i: (0, 0), pipeline_mode=pl.Buffered(1)),
                  pl.BlockSpec((tm, n), lambda i: (i, 0))],
        out_specs=pl.BlockSpec((tm, n), lambda i: (i, 0)),
        out_shape=jax.ShapeDtypeStruct((m, n), F32),
        compiler_params=_params("parallel"),
        name="mmres",
    )(x, w, res)


SGU_STEP_CHUNKS = 2


def _sgu_body(u_ref, v_ref, ws_ref, bs_ref, gs_ref, mix_ref, o_ref, cv_ref):
    del mix_ref
    for c in range(SGU_STEP_CHUNKS):
        rows = slice(c * SGU_CHUNK, (c + 1) * SGU_CHUNK)
        for h in range(A_HEADS):
            sl = slice(h * LANES, (h + 1) * LANES)
            u = _gelu(u_ref[rows, sl])
            v = _gelu(v_ref[rows, sl])
            ms = jnp.mean(v * v, axis=-1, keepdims=True)
            va = (v * lax.rsqrt(ms + EPS)) * gs_ref[:, sl]
            mixed = jnp.dot(ws_ref[h], va.astype(BF16), preferred_element_type=F32) + bs_ref[:, h:h + 1]
            o_ref[rows, sl] = u * mixed
            if c == SGU_STEP_CHUNKS - 1:
                cv_ref[0, :, sl] = va


def _sgu_prompt(z, ws_tril, bs_t, gs, mix):
    step_rows = SGU_STEP_CHUNKS * SGU_CHUNK
    n_steps = SEQ // step_rows
    return pl.pallas_call(
        _sgu_body,
        grid=(T_PROMPT // step_rows,),
        in_specs=[pl.BlockSpec((step_rows, A_WIDTH), lambda i: (i, 0)),
                  pl.BlockSpec((step_rows, A_WIDTH), lambda i: (i, 1)),
                  pl.BlockSpec((A_HEADS, SGU_CHUNK, SGU_CHUNK), lambda i: (0, 0, 0)),
                  pl.BlockSpec((SGU_CHUNK, A_HEADS), lambda i: (0, 0)),
                  pl.BlockSpec((1, A_WIDTH), lambda i: (0, 0)),
                  pl.BlockSpec(memory_space=pl.ANY)],
        out_specs=[pl.BlockSpec((step_rows, A_WIDTH), lambda i: (i, 0)),
                   pl.BlockSpec((1, SGU_CHUNK, A_WIDTH), lambda i: (i // n_steps, 0, 0))],
        out_shape=[jax.ShapeDtypeStruct((T_PAD, D_MODEL), F32),
                   jax.ShapeDtypeStruct((BATCH, SGU_CHUNK, A_WIDTH), F32)],
        input_output_aliases={5: 0},
        compiler_params=_params("arbitrary"),
        name="sgu_prompt",
    )(z, z, ws_tril, bs_t, gs, mix)


def _gla_body(q_ref, k_ref, v_ref, r_ref, zg_ref, wa_ref, ba_ref, gg_ref, mix_ref, o_ref, st_ref, s_ref, g_ref):
    del mix_ref
    c = pl.program_id(1)

    @pl.when(c == 0)
    def _():
        s_ref[...] = jnp.zeros_like(s_ref)

    x = jnp.dot(zg_ref[...].astype(BF16), wa_ref[...], preferred_element_type=F32) + ba_ref[...]
    lg = _log_sigmoid(x) * (1.0 / GATE_TAU)
    tri = (lax.broadcasted_iota(jnp.int32, (GLA_CHUNK, GLA_CHUNK), 1)
           <= lax.broadcasted_iota(jnp.int32, (GLA_CHUNK, GLA_CHUNK), 0)).astype(BF16)
    hi = lg.astype(BF16)
    r1 = lg - hi.astype(F32)
    mid = r1.astype(BF16)
    lo = (r1 - mid.astype(F32)).astype(BF16)
    g_ref[...] = (jnp.dot(tri, hi, preferred_element_type=F32) + jnp.dot(tri, mid, preferred_element_type=F32)
                  + jnp.dot(tri, lo, preferred_element_type=F32))
    eye = _eye(B_KEY_DIM)
    half = GLA_SUB // 2
    t_iota = lax.broadcasted_iota(jnp.int32, (half, B_KEY_DIM), 0)
    scale = B_KEY_DIM ** -0.5

    for h in range(B_HEADS):
        dk = slice(h * B_KEY_DIM, (h + 1) * B_KEY_DIM)
        dv = slice(h * B_VAL_DIM, (h + 1) * B_VAL_DIM)
        k = k_ref[:, dk]
        g = g_ref[:, dk]
        vb = v_ref[:, dv].astype(BF16)
        state = s_ref[h]
        o_inter = jnp.dot((q_ref[:, dk] * scale * jnp.exp(g)).astype(BF16), state.astype(BF16),
                          preferred_element_type=F32)
        parts = []
        for i in range(GLA_CHUNK // GLA_SUB):
            lo_r, mid_r, hi_r = GLA_SUB * i, GLA_SUB * i + half, GLA_SUB * (i + 1)
            q_lo, q_hi = q_ref[lo_r:mid_r, dk] * scale, q_ref[mid_r:hi_r, dk] * scale
            g_lo, g_hi = g_ref[lo_r:mid_r, dk], g_ref[mid_r:hi_r, dk]
            acc_lo = jnp.zeros((half, B_VAL_DIM), F32)
            acc_hi = jnp.zeros((half, B_VAL_DIM), F32)
            for s in range(GLA_SUB):
                r = lo_r + s
                k_row, v_row, g_row = k_ref[r:r + 1, dk], v_ref[r:r + 1, dv], g_ref[r:r + 1, dk]
                if s < half:
                    dec = jnp.exp(jnp.where(t_iota >= s, g_lo - g_row, -jnp.inf))
                    col = jnp.sum(q_lo * k_row * dec, axis=-1, keepdims=True)
                    acc_lo = acc_lo + col * v_row
                    dec = jnp.exp(g_hi - g_row)
                else:
                    dec = jnp.exp(jnp.where(t_iota >= s - half, g_hi - g_row, -jnp.inf))
                col = jnp.sum(q_hi * k_row * dec, axis=-1, keepdims=True)
                acc_hi = acc_hi + col * v_row
            acc = jnp.concatenate([acc_lo, acc_hi], axis=0)
            if i > 0:
                gref = g_ref[lo_r - 1:lo_r, dk]
                qd = (jnp.concatenate([q_lo, q_hi], axis=0) * jnp.exp(g[lo_r:hi_r] - gref)).astype(BF16)
                kd = (k[:lo_r] * jnp.exp(gref - g[:lo_r])).astype(BF16)
                sc = lax.dot_general(qd, kd, (((1,), (1,)), ((), ())), preferred_element_type=F32)
                acc = acc + jnp.dot(sc.astype(BF16), vb[:lo_r], preferred_element_type=F32)
            parts.append(acc)
        o = o_inter + jnp.concatenate(parts, axis=0)

        g_last = g_ref[GLA_CHUNK - 1:GLA_CHUNK, dk]
        kdec = k * jnp.exp(g_last - g)
        upd = jnp.dot(kdec.T.astype(BF16), vb, preferred_element_type=F32)
        s_ref[h] = _row_to_col(jnp.exp(g_last), eye) * state + upd

        ms = jnp.mean(o * o, axis=-1, keepdims=True)
        on = (o * lax.rsqrt(ms + EPS)) * gg_ref[:, dv]
        r = r_ref[:, dv]
        o_ref[:, dv] = (r * jax.nn.sigmoid(r)) * on

    @pl.when(c == pl.num_programs(1) - 1)
    def _():
        st_ref[0] = s_ref[...]


def _gla_prompt(z, zg, wa_pad, ba, gg, mix):
    n_chunks = SEQ // GLA_CHUNK
    row = lambda b, c: b * n_chunks + c
    return pl.pallas_call(
        _gla_body,
        grid=(BATCH, n_chunks),
        in_specs=[pl.BlockSpec((GLA_CHUNK, B_KEY_WIDTH), lambda b, c: (row(b, c), 4)),
                  pl.BlockSpec((GLA_CHUNK, B_KEY_WIDTH), lambda b, c: (row(b, c), 5)),
                  pl.BlockSpec((GLA_CHUNK, B_VAL_WIDTH), lambda b, c: (row(b, c), 3)),
                  pl.BlockSpec((GLA_CHUNK, B_VAL_WIDTH), lambda b, c: (row(b, c), 4)),
                  pl.BlockSpec((GLA_CHUNK, LANES), lambda b, c: (row(b, c), 0)),
                  pl.BlockSpec((LANES, B_KEY_WIDTH), lambda b, c: (0, 0)),
                  pl.BlockSpec((1, B_KEY_WIDTH), lambda b, c: (0, 0)),
                  pl.BlockSpec((1, B_VAL_WIDTH), lambda b, c: (0, 0)),
                  pl.BlockSpec(memory_space=pl.ANY)],
        out_specs=[pl.BlockSpec((GLA_CHUNK, B_VAL_WIDTH), lambda b, c: (row(b, c), 1)),
                   pl.BlockSpec((1, B_HEADS, B_KEY_DIM, B_VAL_DIM), lambda b, c: (b, 0, 0, 0))],
        out_shape=[jax.ShapeDtypeStruct((T_PAD, D_MODEL), F32),
                   jax.ShapeDtypeStruct((BATCH, B_HEADS, B_KEY_DIM, B_VAL_DIM), F32)],
        scratch_shapes=[pltpu.VMEM((B_HEADS, B_KEY_DIM, B_VAL_DIM), F32),
                        pltpu.VMEM((GLA_CHUNK, B_KEY_WIDTH), F32)],
        input_output_aliases={8: 0},
        compiler_params=_params("arbitrary", "arbitrary"),
        name="gla_prompt",
    )(z, z, z, z, zg, wa_pad, ba, gg, mix)


SAMPLE_ROWS = 8


def _smix_body(z_ref, zg_ref, st_ref, wa_ref, ba_ref, gs_ref, gg_ref, w00_ref, b00_ref, mix_ref,
               o_ref, cv_ref, so_ref):
    del mix_ref
    u = _gelu(z_ref[:, 0:A_WIDTH])
    v = _gelu(z_ref[:, A_WIDTH:2 * A_WIDTH])
    for h in range(A_HEADS):
        sl = slice(h * LANES, (h + 1) * LANES)
        vh = v[:, sl]
        ms = jnp.mean(vh * vh, axis=-1, keepdims=True)
        va = (vh * lax.rsqrt(ms + EPS)) * gs_ref[:, sl]
        cv_ref[:, sl] = va
        o_ref[:, sl] = u[:, sl] * (w00_ref[:, sl] * va + b00_ref[:, sl])

    x = jnp.dot(zg_ref[...].astype(BF16), wa_ref[...], preferred_element_type=F32) + ba_ref[...]
    a = jnp.exp(_log_sigmoid(x) * (1.0 / GATE_TAU))
    q0, k0, v0, r0 = 2 * A_WIDTH, 2 * A_WIDTH + B_KEY_WIDTH, 2 * A_WIDTH + 2 * B_KEY_WIDTH, Z_COLS - B_VAL_WIDTH
    eye = _eye(B_KEY_DIM)
    for b in range(SAMPLE_ROWS):
        for h in range(B_HEADS):
            dk = slice(h * B_KEY_DIM, (h + 1) * B_KEY_DIM)
            a_col = _row_to_col(a[b:b + 1, dk], eye)
            k_col = _row_to_col(z_ref[b:b + 1, k0 + h * B_KEY_DIM:k0 + (h + 1) * B_KEY_DIM], eye)
            q_col = _row_to_col(z_ref[b:b + 1, q0 + h * B_KEY_DIM:q0 + (h + 1) * B_KEY_DIM], eye) * (B_KEY_DIM ** -0.5)
            vrow = z_ref[b:b + 1, v0 + h * B_VAL_DIM:v0 + (h + 1) * B_VAL_DIM]
            s_new = a_col * st_ref[b, h] + k_col * vrow
            so_ref[b, h] = s_new
            o = jnp.sum(q_col * s_new, axis=0, keepdims=True)
            ms = jnp.mean(o * o, axis=-1, keepdims=True)
            on = (o * lax.rsqrt(ms + EPS)) * gg_ref[:, h * B_VAL_DIM:(h + 1) * B_VAL_DIM]
            r = z_ref[b:b + 1, r0 + h * B_VAL_DIM:r0 + (h + 1) * B_VAL_DIM]
            o_ref[b:b + 1, A_WIDTH + h * B_VAL_DIM:A_WIDTH + (h + 1) * B_VAL_DIM] = (r * jax.nn.sigmoid(r)) * on


def _sample_mixers(z, zg, state, wa_pad, ba, gs, gg, w00, b00, mix):
    first = T_PROMPT // SAMPLE_ROWS
    return pl.pallas_call(
        _smix_body,
        grid=(T_SAMPLE // SAMPLE_ROWS,),
        in_specs=[pl.BlockSpec((SAMPLE_ROWS, Z_COLS), lambda i: (first + i, 0)),
                  pl.BlockSpec((SAMPLE_ROWS, LANES), lambda i: (first + i, 0)),
                  pl.BlockSpec((SAMPLE_ROWS, B_HEADS, B_KEY_DIM, B_VAL_DIM), lambda i: (i, 0, 0, 0)),
                  pl.BlockSpec((LANES, B_KEY_WIDTH), lambda i: (0, 0)),
                  pl.BlockSpec((1, B_KEY_WIDTH), lambda i: (0, 0)),
                  pl.BlockSpec((1, A_WIDTH), lambda i: (0, 0)),
                  pl.BlockSpec((1, B_VAL_WIDTH), lambda i: (0, 0)),
                  pl.BlockSpec((1, A_WIDTH), lambda i: (0, 0)),
                  pl.BlockSpec((1, A_WIDTH), lambda i: (0, 0)),
                  pl.BlockSpec(memory_space=pl.ANY)],
        out_specs=[pl.BlockSpec((SAMPLE_ROWS, D_MODEL), lambda i: (first + i, 0)),
                   pl.BlockSpec((SAMPLE_ROWS, A_WIDTH), lambda i: (i, 0)),
                   pl.BlockSpec((SAMPLE_ROWS, B_HEADS, B_KEY_DIM, B_VAL_DIM), lambda i: (i, 0, 0, 0))],
        out_shape=[jax.ShapeDtypeStruct((T_PAD, D_MODEL), F32),
                   jax.ShapeDtypeStruct((T_SAMPLE, A_WIDTH), F32),
                   jax.ShapeDtypeStruct((T_SAMPLE, B_HEADS, B_KEY_DIM, B_VAL_DIM), F32)],
        input_output_aliases={9: 0},
        compiler_params=_params("parallel"),
        name="sample_mixers",
    )(z, zg, state, wa_pad, ba, gs, gg, w00, b00, mix)


ATT_TILE = 512


def _attn_prompt_body(q_ref, k_ref, v_ref, att_ref, o_ref):
    del att_ref
    for h in range(X_HEADS):
        sl = slice(h * X_HEAD_DIM, (h + 1) * X_HEAD_DIM)
        q = q_ref[:, sl]
        k = k_ref[0, :, sl].astype(BF16)
        s = lax.dot_general(q, k, (((1,), (1,)), ((), ())), preferred_element_type=F32) * (X_HEAD_DIM ** -0.5)
        s = s - jnp.max(s, axis=-1, keepdims=True)
        e = jnp.exp(s)
        p = e / jnp.sum(e, axis=-1, keepdims=True)
        o_ref[:, sl] = jnp.dot(p.astype(BF16), v_ref[0, :, sl].astype(BF16), preferred_element_type=F32).astype(BF16)


def _attn_prompt(qx, mk, mv, att):
    tiles = SEQ // ATT_TILE
    return pl.pallas_call(
        _attn_prompt_body,
        grid=(BATCH, tiles),
        in_specs=[pl.BlockSpec((ATT_TILE, D_MODEL), lambda b, t: (b * tiles + t, 0)),
                  pl.BlockSpec((1, MEM_LEN, D_MODEL), lambda b, t: (b, 0, 0)),
                  pl.BlockSpec((1, MEM_LEN, D_MODEL), lambda b, t: (b, 0, 0)),
                  pl.BlockSpec(memory_space=pl.ANY)],
        out_specs=pl.BlockSpec((ATT_TILE, D_MODEL), lambda b, t: (b * tiles + t, 0)),
        out_shape=jax.ShapeDtypeStruct((T_PAD, D_MODEL), BF16),
        input_output_aliases={3: 0},
        compiler_params=_params("parallel", "parallel"),
        name="attn_prompt",
    )(qx, mk, mv, att)


ATT_SEQS = 2


def _attn_sample_body(q_ref, k_ref, v_ref, o_ref):
    for b in range(ATT_SEQS):
        q = q_ref[b:b + 1].astype(F32)
        s = jnp.sum(k_ref[0, b] * q, axis=-1, keepdims=True) * (X_HEAD_DIM ** -0.5)
        s = s - jnp.max(s, axis=0, keepdims=True)
        e = jnp.exp(s)
        p = e / jnp.sum(e, axis=0, keepdims=True)
        o_ref[b] = jnp.sum(p * v_ref[0, b], axis=0).astype(BF16)


def _attn_sample(q, ck, cv):
    kv_spec = pl.BlockSpec((1, ATT_SEQS, MEM_LEN, X_HEADS, X_HEAD_DIM), lambda i: (0, i, 0, 0, 0))
    return pl.pallas_call(
        _attn_sample_body,
        grid=(T_SAMPLE // ATT_SEQS,),
        in_specs=[pl.BlockSpec((ATT_SEQS, X_HEADS, X_HEAD_DIM), lambda i: (i, 0, 0)), kv_spec, kv_spec],
        out_specs=pl.BlockSpec((ATT_SEQS, X_HEADS, X_HEAD_DIM), lambda i: (i, 0, 0)),
        out_shape=jax.ShapeDtypeStruct((T_SAMPLE, X_HEADS, X_HEAD_DIM), BF16),
        compiler_params=_params("parallel"),
        name="attn_sample",
    )(q, ck, cv)


def _odd_even_merge_sort_pairs(n):
    pairs = []
    p = 1
    while p < n:
        k = p
        while k >= 1:
            for j in range(k % p, n - k, 2 * k):
                for i in range(min(k, n - j - k)):
                    if (i + j) // (2 * p) == (i + j + k) // (2 * p):
                        pairs.append((i + j, i + j + k))
            k //= 2
        p *= 2
    return pairs


_SORT16 = tuple(_odd_even_merge_sort_pairs(PEER_TOPK))


def _exchange(vs, i, j):
    vs[i], vs[j] = jnp.maximum(vs[i], vs[j]), jnp.minimum(vs[i], vs[j])


def _top16_sorted(s):
    vs = [s[8 * k:8 * (k + 1), :] for k in range(PEER_TOPK)]
    for i, j in _SORT16:
        _exchange(vs, i, j)
    for shift in (4, 2, 1):
        vs = _merge_top16(vs, [pltpu.roll(v, shift, 0) for v in vs])
    return vs


def _merge_top16(a, b):
    c = list(a)
    for k, bk in enumerate(b):
        c[PEER_TOPK - 1 - k] = jnp.maximum(a[PEER_TOPK - 1 - k], bk)
    for stride in (8, 4, 2, 1):
        for k in range(PEER_TOPK):
            if k & stride == 0:
                _exchange(c, k, k + stride)
    return c


def _stack_rows(vs, rows8):
    out = vs[0]
    for k in range(1, 8):
        out = jnp.where(rows8 == k, vs[k], out)
    return out


_CAND_ROWS = tuple((a, PEER_TOPK // (a + 1)) for a in range(1, 8))


def _peer_route_body(pq_ref, keys_ref, s0_ref, s1_ref, a_ref, b_ref, th_ref):
    rows8 = lax.broadcasted_iota(jnp.int32, (8, LANES), 0)

    def head(h, carry):
        q0 = pq_ref[:, pl.ds(pl.multiple_of(h * 2 * LANES, LANES), LANES)]
        q1 = pq_ref[:, pl.ds(pl.multiple_of(h * 2 * LANES + LANES, LANES), LANES)]
        nt = (((1,), (1,)), ((), ()))
        s0 = lax.dot_general(keys_ref[2 * h], q0, nt, preferred_element_type=F32)
        s1 = lax.dot_general(keys_ref[2 * h + 1], q1, nt, preferred_element_type=F32)
        r0 = _top16_sorted(s0)
        r1 = _top16_sorted(s1)
        best = [r0[0] + r1[b] for b in range(PEER_TOPK)]
        for a, nb in _CAND_ROWS:
            best = _merge_top16(best, [r0[a] + r1[b] for b in range(nb)])
        best = _merge_top16(best, [r0[a] + r1[0] for a in range(8, PEER_TOPK)])
        theta = best[PEER_TOPK - 1][0:1]
        top = best[0][0:1]
        v1_lo = _stack_rows(r1[0:8], rows8)
        cs = [r0[0] + v1_lo, r0[0] + _stack_rows(r1[8:16], rows8)]
        for a, nb in _CAND_ROWS:
            cs.append(jnp.where(rows8 < nb, r0[a] + v1_lo, -jnp.inf))
        cs.append(_stack_rows(r0[8:16], rows8) + r1[0])
        cand = jnp.concatenate(cs, axis=0)
        zsum = jnp.sum(jnp.where(cand >= theta, jnp.exp(cand - top), 0.0), axis=0, keepdims=True)
        s0_ref[h] = s0
        s1_ref[h] = s1
        a_ref[h] = jnp.exp(s0 - r0[0][0:1]) / zsum
        b_ref[h] = jnp.exp(s1 - r1[0][0:1])
        th_ref[h] = theta
        return carry

    lax.fori_loop(0, PEER_HEADS, head, 0)


def _peer_route(pq, keys):
    big = jax.ShapeDtypeStruct((PEER_HEADS, N_KEYS, T_PAD), F32)
    big_spec = pl.BlockSpec((PEER_HEADS, N_KEYS, LANES), lambda i: (0, 0, i))
    return pl.pallas_call(
        _peer_route_body,
        grid=(T_PAD // LANES,),
        in_specs=[pl.BlockSpec((LANES, D_MODEL), lambda i: (i, 0)),
                  pl.BlockSpec((2 * PEER_HEADS, N_KEYS, LANES), lambda i: (0, 0, 0))],
        out_specs=[big_spec, big_spec, big_spec, big_spec,
                   pl.BlockSpec((PEER_HEADS, 1, LANES), lambda i: (0, 0, i))],
        out_shape=[big, big, big, big, jax.ShapeDtypeStruct((PEER_HEADS, 1, T_PAD), F32)],
        compiler_params=_params("parallel"),
        name="peer_route",
    )(pq, keys)


N_EXPERT_BLOCKS = N_EXPERTS // EXPERT_TILE
N_TOKEN_BLOCKS = T_PAD // TOKEN_TILE
N_PAIRS = N_TOKEN_BLOCKS * N_EXPERT_BLOCKS
PIPE_LAG = 2
GATE_ROWS = 32
N_GROUPS = EXPERT_TILE // N_KEYS


def _pair(n, lag):
    c = jnp.clip(n - lag, 0, N_PAIRS - 1)
    return c // N_EXPERT_BLOCKS, c % N_EXPERT_BLOCKS


def _peer_dense_body(h_ref, u_ref, vt_ref, s0_ref, s1_ref, a_ref, b_ref, th_ref, o_ref, s_even, s_odd, a_even, a_odd):
    n = pl.program_id(0)

    @pl.when(n == 0)
    def _():
        for ref in (s_even, s_odd, a_even, a_odd):
            ref[...] = jnp.zeros_like(ref)

    @pl.when(jnp.logical_or(n == 0, jnp.maximum(n - PIPE_LAG, 0) % N_EXPERT_BLOCKS == 0))
    def _():
        o_ref[...] = jnp.zeros_like(o_ref)

    def run(s_w, s_r, a_w, a_r):
        def cols(nn, carry):
            t0 = pl.multiple_of(nn * MXU_DIM, MXU_DIM)
            tok = pl.ds(t0, MXU_DIM)
            o_ref[:, tok] += jnp.dot(vt_ref[...], a_r[:, tok], preferred_element_type=F32)
            s_w[:, tok] = jnp.dot(u_ref[...], h_ref[:, tok], preferred_element_type=F32)
            for g in range(N_GROUPS):
                for lt in range(MXU_DIM // LANES):
                    lanes = pl.ds(pl.multiple_of(t0 + lt * LANES, LANES), LANES)
                    for j0 in range(0, N_KEYS, GATE_ROWS):
                        keys = slice(j0, j0 + GATE_ROWS)
                        rows = slice(g * N_KEYS + j0, g * N_KEYS + j0 + GATE_ROWS)
                        gate = jnp.zeros((GATE_ROWS, LANES), F32)
                        for h in range(PEER_HEADS):
                            c = s0_ref[h, g:g + 1, lanes] + s1_ref[h, keys, lanes]
                            w = a_ref[h, g:g + 1, lanes] * b_ref[h, keys, lanes]
                            gate = gate + jnp.where(c >= th_ref[h, :, lanes], w, 0.0)
                        a_w[rows, lanes] = (_gelu(s_r[rows, lanes]) * gate).astype(BF16)
            return carry

        lax.fori_loop(0, TOKEN_TILE // MXU_DIM, cols, 0)

    @pl.when(n % 2 == 0)
    def _():
        run(s_even, s_odd, a_odd, a_even)

    @pl.when(n % 2 == 1)
    def _():
        run(s_odd, s_even, a_even, a_odd)


def _peer_dense(hn_t, u_bf, vt_bf, s0, s1, a, b, th):
    once = pl.Buffered(1)
    route_spec = pl.BlockSpec((PEER_HEADS, N_KEYS, TOKEN_TILE), lambda n: (0, 0, _pair(n, 1)[0]), pipeline_mode=once)
    row_spec = pl.BlockSpec((PEER_HEADS, EXPERT_TILE // N_KEYS, TOKEN_TILE),
                            lambda n: (0, _pair(n, 1)[1], _pair(n, 1)[0]))
    return pl.pallas_call(
        _peer_dense_body,
        grid=(N_PAIRS + PIPE_LAG,),
        in_specs=[pl.BlockSpec((D_MODEL, TOKEN_TILE), lambda n: (0, _pair(n, 0)[0]), pipeline_mode=once),
                  pl.BlockSpec((EXPERT_TILE, D_MODEL), lambda n: (_pair(n, 0)[1], 0)),
                  pl.BlockSpec((D_MODEL, EXPERT_TILE), lambda n: (0, _pair(n, PIPE_LAG)[1])),
                  row_spec, route_spec, row_spec, route_spec,
                  pl.BlockSpec((PEER_HEADS, 1, TOKEN_TILE), lambda n: (0, 0, _pair(n, 1)[0]))],
        out_specs=pl.BlockSpec((D_MODEL, TOKEN_TILE), lambda n: (0, _pair(n, PIPE_LAG)[0])),
        out_shape=jax.ShapeDtypeStruct((D_MODEL, T_PAD), F32),
        scratch_shapes=[pltpu.VMEM((EXPERT_TILE, TOKEN_TILE), F32), pltpu.VMEM((EXPERT_TILE, TOKEN_TILE), F32),
                        pltpu.VMEM((EXPERT_TILE, TOKEN_TILE), BF16), pltpu.VMEM((EXPERT_TILE, TOKEN_TILE), BF16)],
        compiler_params=_params("arbitrary"),
        name="peer_dense",
    )(hn_t, u_bf, vt_bf, s0, s1, a, b, th)


def _final_body(x_ref, pt_ref, g_ref, o_ref):
    x = x_ref[...] + pt_ref[...].T
    ms = jnp.mean(x * x, axis=-1, keepdims=True)
    o_ref[...] = (x * lax.rsqrt(ms + EPS)) * g_ref[...]


def _final(x2, peer_t, g, first_row, n_rows, tile):
    first_tile = first_row // tile
    return pl.pallas_call(
        _final_body,
        grid=(n_rows // tile,),
        in_specs=[pl.BlockSpec((tile, D_MODEL), lambda i: (first_tile + i, 0)),
                  pl.BlockSpec((D_MODEL, tile), lambda i: (0, first_tile + i)),
                  pl.BlockSpec((1, D_MODEL), lambda i: (0, 0))],
        out_specs=pl.BlockSpec((tile, D_MODEL), lambda i: (i, 0)),
        out_shape=jax.ShapeDtypeStruct((n_rows, D_MODEL), F32),
        compiler_params=_params("parallel"),
        name="final_norm",
    )(x2, peer_t, g.reshape(1, D_MODEL))


def kernel(x_prompt, x_sample, mem_prompt, cache_mem_k, cache_mem_v, state_gla, g_mix, w_in, w_alpha, b_alpha, w_s, b_s, g_sgu, g_gla, w_out, g_mem, w_xk, w_xv, g_xq, w_xq, w_xo, g_ffn, w_pq, sub_keys, u_emb, v_emb, g_final):
    w_in_main = w_in[0, :, :Z_COLS].astype(BF16)
    w_in_gate = jnp.pad(w_in[0, :, Z_COLS:], ((0, 0), (0, LANES - GATE_RANK))).astype(BF16)
    wa_pad = jnp.pad(w_alpha[0], ((0, LANES - GATE_RANK), (0, 0))).astype(BF16)
    ba = b_alpha[0].reshape(1, B_KEY_WIDTH)
    causal = jnp.tril(jnp.ones((SGU_CHUNK, SGU_CHUNK), dtype=bool))
    ws_tril = jnp.where(causal, w_s[0], 0).astype(BF16)
    bs_t = b_s[0].T
    w00 = jnp.repeat(w_s[0, :, 0, 0], LANES).reshape(1, A_WIDTH)
    b00 = jnp.repeat(b_s[0, :, 0], LANES).reshape(1, A_WIDTH)
    gs = g_sgu[0].reshape(1, A_WIDTH)
    gg = g_gla[0].reshape(1, B_VAL_WIDTH)
    keys = sub_keys[0].reshape(2 * PEER_HEADS, N_KEYS, LANES).astype(BF16)
    u_bf = u_emb[0].astype(BF16)
    vt_bf = v_emb[0].T.astype(BF16)

    x_all = jnp.concatenate([x_prompt.reshape(T_PROMPT, D_MODEL), x_sample.reshape(T_SAMPLE, D_MODEL),
                             jnp.zeros((T_PAD - T_ALL, D_MODEL), F32)], axis=0)

    mem = mem_prompt.reshape(BATCH * MEM_LEN, D_MODEL)
    mk, = _nmm(mem, g_mem[0], w_xk[0].astype(BF16), tm=512, tn=1024)
    mv, = _nmm(mem, g_mem[0], w_xv[0].astype(BF16), tm=512, tn=1024)

    z, zg = _nmm(x_all, g_mix[0], w_in_main, tm=TOKEN_TILE, tn=1280, w_aux=w_in_gate)
    mix, cv_p = _sgu_prompt(z, ws_tril, bs_t, gs, jnp.zeros((T_PAD, D_MODEL), F32))
    mix, st_p = _gla_prompt(z, zg, wa_pad, ba, gg, mix)
    mix, cv_s, st_s = _sample_mixers(z, zg, state_gla[0], wa_pad, ba, gs, gg, w00, b00, mix)
    x1 = _mmres(mix, w_out[0].astype(BF16), x_all, tm=TOKEN_TILE)

    qx, = _nmm(x1, g_xq[0], w_xq[0].astype(BF16), tm=TOKEN_TILE, tn=1024, out_dtype=BF16)
    att = _attn_prompt(qx, mk.reshape(BATCH, MEM_LEN, D_MODEL), mv.reshape(BATCH, MEM_LEN, D_MODEL),
                       jnp.zeros((T_PAD, D_MODEL), BF16))
    att_s = _attn_sample(qx[T_PROMPT:T_ALL].reshape(T_SAMPLE, X_HEADS, X_HEAD_DIM), cache_mem_k, cache_mem_v)
    att = lax.dynamic_update_slice(att, att_s.reshape(T_SAMPLE, D_MODEL), (T_PROMPT, 0))
    x2 = _mmres(att, w_xo[0].astype(BF16), x1, tm=TOKEN_TILE)

    pq, hn = _nmm(x2, g_ffn[0], w_pq[0].astype(BF16), tm=TOKEN_TILE, tn=1024, emit_xn=True, out_dtype=BF16)
    s0, s1, a, b, th = _peer_route(pq, keys)
    peer_t = _peer_dense(hn.T, u_bf, vt_bf, s0, s1, a, b, th)

    y_p = _final(x2, peer_t, g_final, 0, T_PROMPT, 2 * LANES)
    y_s = _final(x2, peer_t, g_final, T_PROMPT, T_SAMPLE, LANES)

    return (y_p.reshape(BATCH, SEQ, D_MODEL),
            y_s.reshape(DEC_BATCH, 1, D_MODEL),
            mk.reshape(1, BATCH, MEM_LEN, X_HEADS, X_HEAD_DIM),
            mv.reshape(1, BATCH, MEM_LEN, X_HEADS, X_HEAD_DIM),
            st_p.reshape(1, BATCH, B_HEADS, B_KEY_DIM, B_VAL_DIM),
            st_s.reshape(1, DEC_BATCH, B_HEADS, B_KEY_DIM, B_VAL_DIM),
            cv_p.reshape(1, BATCH, SGU_CHUNK, A_HEADS, LANES),
            cv_s.reshape(1, DEC_BATCH, 1, A_HEADS, LANES))
```

```python
import functools

import jax
import jax.numpy as jnp
from jax import lax
from jax.experimental import pallas as pl
from jax.experimental.pallas import tpu as pltpu

F32 = jnp.float32
BF16 = jnp.bfloat16

D_MODEL = 2048
BATCH = 4
SEQ = 2048
DEC_BATCH = 128
T_PROMPT = BATCH * SEQ
T_SAMPLE = DEC_BATCH
T_ALL = T_PROMPT + T_SAMPLE

A_HEADS = 8
A_WIDTH = 1024
SGU_CHUNK = 128
B_HEADS = 4
B_KEY_DIM = 128
B_VAL_DIM = 256
B_KEY_WIDTH = 512
B_VAL_WIDTH = 1024
GATE_RANK = 16
GATE_TAU = 16.0
GLA_CHUNK = 64
GLA_SUB = 16
Z_COLS = 2 * A_WIDTH + 2 * B_KEY_WIDTH + 2 * B_VAL_WIDTH
MEM_LEN = 256
X_HEADS = 4
X_HEAD_DIM = 512
N_KEYS = 128
N_EXPERTS = N_KEYS * N_KEYS
PEER_HEADS = 8
PEER_TOPK = 16
EPS = 1e-6

LANES = 128
MXU_DIM = 256
TOKEN_TILE = 3 * MXU_DIM
T_PAD = 11 * TOKEN_TILE
EXPERT_TILE = 1024
VMEM_LIMIT = 56 * 1024 * 1024
PEER_VMEM_LIMIT = 60 * 1024 * 1024


def _params(*sem, vmem=VMEM_LIMIT):
    return pltpu.CompilerParams(dimension_semantics=sem, vmem_limit_bytes=vmem)


_GELU_2C = 2.0 * 0.7978845608028654
_LOG2E = 1.4426950408889634


def _gelu(x):
    p = (x * x) * (-_LOG2E * _GELU_2C * 0.044715) + (-_LOG2E * _GELU_2C)
    return x / (1.0 + jnp.exp2(x * p))


def _log_sigmoid(x):
    return jnp.minimum(x, 0.0) - jnp.log1p(jnp.exp(-jnp.abs(x)))


def _row_to_col(row, eye):
    return jnp.sum(jnp.where(eye, row, 0.0), axis=-1, keepdims=True)


def _eye(n):
    return lax.broadcasted_iota(jnp.int32, (n, n), 0) == lax.broadcasted_iota(jnp.int32, (n, n), 1)


def _nmm_body(*refs, has_aux, emit_xn):
    x_ref, g_ref, w_ref = refs[:3]
    rest = list(refs[3:])
    w2_ref = rest.pop(0) if has_aux else None
    o_ref = rest.pop(0)
    o2_ref = rest.pop(0) if has_aux else None
    xo_ref = rest.pop(0) if emit_xn else None
    xn_ref = rest.pop(0)

    @pl.when(pl.program_id(1) == 0)
    def _():
        x = x_ref[...]
        ms = jnp.mean(x * x, axis=-1, keepdims=True)
        xn = ((x * lax.rsqrt(ms + EPS)) * g_ref[...]).astype(BF16)
        xn_ref[...] = xn
        if has_aux:
            o2_ref[...] = jnp.dot(xn, w2_ref[...], preferred_element_type=F32)
        if emit_xn:
            xo_ref[...] = xn

    o_ref[...] = jnp.dot(xn_ref[...], w_ref[...], preferred_element_type=F32).astype(o_ref.dtype)


def _nmm(x, g, w, *, tm, tn, w_aux=None, emit_xn=False, out_dtype=F32):
    m, k = x.shape
    n = w.shape[1]
    w_mode = pl.Buffered(1) if tn == n else None
    in_specs = [pl.BlockSpec((tm, k), lambda i, j: (i, 0)),
                pl.BlockSpec((1, k), lambda i, j: (0, 0)),
                pl.BlockSpec((k, tn), lambda i, j: (0, j), pipeline_mode=w_mode)]
    args = [x, g.reshape(1, k), w]
    out_shape = [jax.ShapeDtypeStruct((m, n), out_dtype)]
    out_specs = [pl.BlockSpec((tm, tn), lambda i, j: (i, j))]
    if w_aux is not None:
        in_specs.append(pl.BlockSpec((k, LANES), lambda i, j: (0, 0)))
        args.append(w_aux)
        out_shape.append(jax.ShapeDtypeStruct((m, LANES), F32))
        out_specs.append(pl.BlockSpec((tm, LANES), lambda i, j: (i, 0)))
    if emit_xn:
        out_shape.append(jax.ShapeDtypeStruct((m, k), BF16))
        out_specs.append(pl.BlockSpec((tm, k), lambda i, j: (i, 0)))
    return pl.pallas_call(
        functools.partial(_nmm_body, has_aux=w_aux is not None, emit_xn=emit_xn),
        grid=(m // tm, n // tn),
        in_specs=in_specs, out_specs=out_specs, out_shape=out_shape,
        scratch_shapes=[pltpu.VMEM((tm, k), BF16)],
        compiler_params=_params("parallel", "arbitrary"),
        name="nmm",
    )(*args)


def _mmres_body(x_ref, w_ref, r_ref, o_ref):
    o_ref[...] = r_ref[...] + jnp.dot(x_ref[...].astype(BF16), w_ref[...], preferred_element_type=F32)


def _mmres(x, w, res, *, tm):
    m, k = x.shape
    n = w.shape[1]
    return pl.pallas_call(
        _mmres_body,
        grid=(m // tm,),
        in_specs=[pl.BlockSpec((tm, k), lambda i: (i, 0)),
                  pl.BlockSpec((k, n), lambda i: (0, 0), pipeline_mode=pl.Buffered(1)),
                  pl.BlockSpec((tm, n), lambda i: (i, 0))],
        out_specs=pl.BlockSpec((tm, n), lambda i: (i, 0)),
        out_shape=jax.ShapeDtypeStruct((m, n), F32),
        compiler_params=_params("parallel"),
        name="mmres",
    )(x, w, res)


SGU_STEP_CHUNKS = 2


def _sgu_body(u_ref, v_ref, ws_ref, bs_ref, gs_ref, mix_ref, o_ref, cv_ref):
    del mix_ref
    for c in range(SGU_STEP_CHUNKS):
        rows = slice(c * SGU_CHUNK, (c + 1) * SGU_CHUNK)
        for h in range(A_HEADS):
            sl = slice(h * LANES, (h + 1) * LANES)
            u = _gelu(u_ref[rows, sl])
            v = _gelu(v_ref[rows, sl])
            ms = jnp.mean(v * v, axis=-1, keepdims=True)
            va = (v * lax.rsqrt(ms + EPS)) * gs_ref[:, sl]
            mixed = jnp.dot(ws_ref[h], va.astype(BF16), preferred_element_type=F32) + bs_ref[:, h:h + 1]
            o_ref[rows, sl] = u * mixed
            if c == SGU_STEP_CHUNKS - 1:
                cv_ref[0, :, sl] = va


def _sgu_prompt(z, ws_tril, bs_t, gs, mix):
    step_rows = SGU_STEP_CHUNKS * SGU_CHUNK
    n_steps = SEQ // step_rows
    return pl.pallas_call(
        _sgu_body,
        grid=(T_PROMPT // step_rows,),
        in_specs=[pl.BlockSpec((step_rows, A_WIDTH), lambda i: (i, 0)),
                  pl.BlockSpec((step_rows, A_WIDTH), lambda i: (i, 1)),
                  pl.BlockSpec((A_HEADS, SGU_CHUNK, SGU_CHUNK), lambda i: (0, 0, 0)),
                  pl.BlockSpec((SGU_CHUNK, A_HEADS), lambda i: (0, 0)),
                  pl.BlockSpec((1, A_WIDTH), lambda i: (0, 0)),
                  pl.BlockSpec(memory_space=pl.ANY)],
        out_specs=[pl.BlockSpec((step_rows, A_WIDTH), lambda i: (i, 0)),
                   pl.BlockSpec((1, SGU_CHUNK, A_WIDTH), lambda i: (i // n_steps, 0, 0))],
        out_shape=[jax.ShapeDtypeStruct((T_PAD, D_MODEL), F32),
                   jax.ShapeDtypeStruct((BATCH, SGU_CHUNK, A_WIDTH), F32)],
        input_output_aliases={5: 0},
        compiler_params=_params("arbitrary"),
        name="sgu_prompt",
    )(z, z, ws_tril, bs_t, gs, mix)


def _gla_body(q_ref, k_ref, v_ref, r_ref, zg_ref, wa_ref, ba_ref, gg_ref, mix_ref, o_ref, st_ref, s_ref, g_ref):
    del mix_ref
    c = pl.program_id(1)

    @pl.when(c == 0)
    def _():
        s_ref[...] = jnp.zeros_like(s_ref)

    x = jnp.dot(zg_ref[...].astype(BF16), wa_ref[...], preferred_element_type=F32) + ba_ref[...]
    lg = _log_sigmoid(x) * (1.0 / GATE_TAU)
    tri = (lax.broadcasted_iota(jnp.int32, (GLA_CHUNK, GLA_CHUNK), 1)
           <= lax.broadcasted_iota(jnp.int32, (GLA_CHUNK, GLA_CHUNK), 0)).astype(BF16)
    hi = lg.astype(BF16)
    r1 = lg - hi.astype(F32)
    mid = r1.astype(BF16)
    lo = (r1 - mid.astype(F32)).astype(BF16)
    g_ref[...] = (jnp.dot(tri, hi, preferred_element_type=F32) + jnp.dot(tri, mid, preferred_element_type=F32)
                  + jnp.dot(tri, lo, preferred_element_type=F32))
    eye = _eye(B_KEY_DIM)
    half = GLA_SUB // 2
    t_iota = lax.broadcasted_iota(jnp.int32, (half, B_KEY_DIM), 0)
    scale = B_KEY_DIM ** -0.5

    for h in range(B_HEADS):
        dk = slice(h * B_KEY_DIM, (h + 1) * B_KEY_DIM)
        dv = slice(h * B_VAL_DIM, (h + 1) * B_VAL_DIM)
        k = k_ref[:, dk]
        g = g_ref[:, dk]
        vb = v_ref[:, dv].astype(BF16)
        state = s_ref[h]
        o_inter = jnp.dot((q_ref[:, dk] * scale * jnp.exp(g)).astype(BF16), state.astype(BF16),
                          preferred_element_type=F32)
        parts = []
        for i in range(GLA_CHUNK // GLA_SUB):
            lo_r, mid_r, hi_r = GLA_SUB * i, GLA_SUB * i + half, GLA_SUB * (i + 1)
            q_lo, q_hi = q_ref[lo_r:mid_r, dk] * scale, q_ref[mid_r:hi_r, dk] * scale
            g_lo, g_hi = g_ref[lo_r:mid_r, dk], g_ref[mid_r:hi_r, dk]
            acc_lo = jnp.zeros((half, B_VAL_DIM), F32)
            acc_hi = jnp.zeros((half, B_VAL_DIM), F32)
            for s in range(GLA_SUB):
                r = lo_r + s
                k_row, v_row, g_row = k_ref[r:r + 1, dk], v_ref[r:r + 1, dv], g_ref[r:r + 1, dk]
                if s < half:
                    dec = jnp.exp(jnp.where(t_iota >= s, g_lo - g_row, -jnp.inf))
                    col = jnp.sum(q_lo * k_row * dec, axis=-1, keepdims=True)
                    acc_lo = acc_lo + col * v_row
                    dec = jnp.exp(g_hi - g_row)
                else:
                    dec = jnp.exp(jnp.where(t_iota >= s - half, g_hi - g_row, -jnp.inf))
                col = jnp.sum(q_hi * k_row * dec, axis=-1, keepdims=True)
                acc_hi = acc_hi + col * v_row
            acc = jnp.concatenate([acc_lo, acc_hi], axis=0)
            if i > 0:
                gref = g_ref[lo_r - 1:lo_r, dk]
                qd = (jnp.concatenate([q_lo, q_hi], axis=0) * jnp.exp(g[lo_r:hi_r] - gref)).astype(BF16)
                kd = (k[:lo_r] * jnp.exp(gref - g[:lo_r])).astype(BF16)
                sc = lax.dot_general(qd, kd, (((1,), (1,)), ((), ())), preferred_element_type=F32)
                acc = acc + jnp.dot(sc.astype(BF16), vb[:lo_r], preferred_element_type=F32)
            parts.append(acc)
        o = o_inter + jnp.concatenate(parts, axis=0)

        g_last = g_ref[GLA_CHUNK - 1:GLA_CHUNK, dk]
        kdec = k * jnp.exp(g_last - g)
        upd = jnp.dot(kdec.T.astype(BF16), vb, preferred_element_type=F32)
        s_ref[h] = _row_to_col(jnp.exp(g_last), eye) * state + upd

        ms = jnp.mean(o * o, axis=-1, keepdims=True)
        on = (o * lax.rsqrt(ms + EPS)) * gg_ref[:, dv]
        r = r_ref[:, dv]
        o_ref[:, dv] = (r * jax.nn.sigmoid(r)) * on

    @pl.when(c == pl.num_programs(1) - 1)
    def _():
        st_ref[0] = s_ref[...]


def _gla_prompt(z, zg, wa_pad, ba, gg, mix):
    n_chunks = SEQ // GLA_CHUNK
    row = lambda b, c: b * n_chunks + c
    return pl.pallas_call(
        _gla_body,
        grid=(BATCH, n_chunks),
        in_specs=[pl.BlockSpec((GLA_CHUNK, B_KEY_WIDTH), lambda b, c: (row(b, c), 4)),
                  pl.BlockSpec((GLA_CHUNK, B_KEY_WIDTH), lambda b, c: (row(b, c), 5)),
                  pl.BlockSpec((GLA_CHUNK, B_VAL_WIDTH), lambda b, c: (row(b, c), 3)),
                  pl.BlockSpec((GLA_CHUNK, B_VAL_WIDTH), lambda b, c: (row(b, c), 4)),
                  pl.BlockSpec((GLA_CHUNK, LANES), lambda b, c: (row(b, c), 0)),
                  pl.BlockSpec((LANES, B_KEY_WIDTH), lambda b, c: (0, 0)),
                  pl.BlockSpec((1, B_KEY_WIDTH), lambda b, c: (0, 0)),
                  pl.BlockSpec((1, B_VAL_WIDTH), lambda b, c: (0, 0)),
                  pl.BlockSpec(memory_space=pl.ANY)],
        out_specs=[pl.BlockSpec((GLA_CHUNK, B_VAL_WIDTH), lambda b, c: (row(b, c), 1)),
                   pl.BlockSpec((1, B_HEADS, B_KEY_DIM, B_VAL_DIM), lambda b, c: (b, 0, 0, 0))],
        out_shape=[jax.ShapeDtypeStruct((T_PAD, D_MODEL), F32),
                   jax.ShapeDtypeStruct((BATCH, B_HEADS, B_KEY_DIM, B_VAL_DIM), F32)],
        scratch_shapes=[pltpu.VMEM((B_HEADS, B_KEY_DIM, B_VAL_DIM), F32),
                        pltpu.VMEM((GLA_CHUNK, B_KEY_WIDTH), F32)],
        input_output_aliases={8: 0},
        compiler_params=_params("arbitrary", "arbitrary"),
        name="gla_prompt",
    )(z, z, z, z, zg, wa_pad, ba, gg, mix)


SAMPLE_ROWS = 8


def _smix_body(z_ref, zg_ref, st_ref, wa_ref, ba_ref, gs_ref, gg_ref, w00_ref, b00_ref, mix_ref,
               o_ref, cv_ref, so_ref):
    del mix_ref
    u = _gelu(z_ref[:, 0:A_WIDTH])
    v = _gelu(z_ref[:, A_WIDTH:2 * A_WIDTH])
    for h in range(A_HEADS):
        sl = slice(h * LANES, (h + 1) * LANES)
        vh = v[:, sl]
        ms = jnp.mean(vh * vh, axis=-1, keepdims=True)
        va = (vh * lax.rsqrt(ms + EPS)) * gs_ref[:, sl]
        cv_ref[:, sl] = va
        o_ref[:, sl] = u[:, sl] * (w00_ref[:, sl] * va + b00_ref[:, sl])

    x = jnp.dot(zg_ref[...].astype(BF16), wa_ref[...], preferred_element_type=F32) + ba_ref[...]
    a = jnp.exp(_log_sigmoid(x) * (1.0 / GATE_TAU))
    q0, k0, v0, r0 = 2 * A_WIDTH, 2 * A_WIDTH + B_KEY_WIDTH, 2 * A_WIDTH + 2 * B_KEY_WIDTH, Z_COLS - B_VAL_WIDTH
    eye = _eye(B_KEY_DIM)
    for b in range(SAMPLE_ROWS):
        for h in range(B_HEADS):
            dk = slice(h * B_KEY_DIM, (h + 1) * B_KEY_DIM)
            a_col = _row_to_col(a[b:b + 1, dk], eye)
            k_col = _row_to_col(z_ref[b:b + 1, k0 + h * B_KEY_DIM:k0 + (h + 1) * B_KEY_DIM], eye)
            q_col = _row_to_col(z_ref[b:b + 1, q0 + h * B_KEY_DIM:q0 + (h + 1) * B_KEY_DIM], eye) * (B_KEY_DIM ** -0.5)
            vrow = z_ref[b:b + 1, v0 + h * B_VAL_DIM:v0 + (h + 1) * B_VAL_DIM]
            s_new = a_col * st_ref[b, h] + k_col * vrow
            so_ref[b, h] = s_new
            o = jnp.sum(q_col * s_new, axis=0, keepdims=True)
            ms = jnp.mean(o * o, axis=-1, keepdims=True)
            on = (o * lax.rsqrt(ms + EPS)) * gg_ref[:, h * B_VAL_DIM:(h + 1) * B_VAL_DIM]
            r = z_ref[b:b + 1, r0 + h * B_VAL_DIM:r0 + (h + 1) * B_VAL_DIM]
            o_ref[b:b + 1, A_WIDTH + h * B_VAL_DIM:A_WIDTH + (h + 1) * B_VAL_DIM] = (r * jax.nn.sigmoid(r)) * on


def _sample_mixers(z, zg, state, wa_pad, ba, gs, gg, w00, b00, mix):
    first = T_PROMPT // SAMPLE_ROWS
    return pl.pallas_call(
        _smix_body,
        grid=(T_SAMPLE // SAMPLE_ROWS,),
        in_specs=[pl.BlockSpec((SAMPLE_ROWS, Z_COLS), lambda i: (first + i, 0)),
                  pl.BlockSpec((SAMPLE_ROWS, LANES), lambda i: (first + i, 0)),
                  pl.BlockSpec((SAMPLE_ROWS, B_HEADS, B_KEY_DIM, B_VAL_DIM), lambda i: (i, 0, 0, 0)),
                  pl.BlockSpec((LANES, B_KEY_WIDTH), lambda i: (0, 0)),
                  pl.BlockSpec((1, B_KEY_WIDTH), lambda i: (0, 0)),
                  pl.BlockSpec((1, A_WIDTH), lambda i: (0, 0)),
                  pl.BlockSpec((1, B_VAL_WIDTH), lambda i: (0, 0)),
                  pl.BlockSpec((1, A_WIDTH), lambda i: (0, 0)),
                  pl.BlockSpec((1, A_WIDTH), lambda i: (0, 0)),
                  pl.BlockSpec(memory_space=pl.ANY)],
        out_specs=[pl.BlockSpec((SAMPLE_ROWS, D_MODEL), lambda i: (first + i, 0)),
                   pl.BlockSpec((SAMPLE_ROWS, A_WIDTH), lambda i: (i, 0)),
                   pl.BlockSpec((SAMPLE_ROWS, B_HEADS, B_KEY_DIM, B_VAL_DIM), lambda i: (i, 0, 0, 0))],
        out_shape=[jax.ShapeDtypeStruct((T_PAD, D_MODEL), F32),
                   jax.ShapeDtypeStruct((T_SAMPLE, A_WIDTH), F32),
                   jax.ShapeDtypeStruct((T_SAMPLE, B_HEADS, B_KEY_DIM, B_VAL_DIM), F32)],
        input_output_aliases={9: 0},
        compiler_params=_params("parallel"),
        name="sample_mixers",
    )(z, zg, state, wa_pad, ba, gs, gg, w00, b00, mix)


ATT_TILE = 512


def _attn_prompt_body(q_ref, k_ref, v_ref, att_ref, o_ref):
    del att_ref
    for h in range(X_HEADS):
        sl = slice(h * X_HEAD_DIM, (h + 1) * X_HEAD_DIM)
        q = q_ref[:, sl]
        k = k_ref[0, :, sl].astype(BF16)
        s = lax.dot_general(q, k, (((1,), (1,)), ((), ())), preferred_element_type=F32) * (X_HEAD_DIM ** -0.5)
        s = s - jnp.max(s, axis=-1, keepdims=True)
        e = jnp.exp(s)
        p = e / jnp.sum(e, axis=-1, keepdims=True)
        o_ref[:, sl] = jnp.dot(p.astype(BF16), v_ref[0, :, sl].astype(BF16), preferred_element_type=F32).astype(BF16)


def _attn_prompt(qx, mk, mv, att):
    tiles = SEQ // ATT_TILE
    return pl.pallas_call(
        _attn_prompt_body,
        grid=(BATCH, tiles),
        in_specs=[pl.BlockSpec((ATT_TILE, D_MODEL), lambda b, t: (b * tiles + t, 0)),
                  pl.BlockSpec((1, MEM_LEN, D_MODEL), lambda b, t: (b, 0, 0)),
                  pl.BlockSpec((1, MEM_LEN, D_MODEL), lambda b, t: (b, 0, 0)),
                  pl.BlockSpec(memory_space=pl.ANY)],
        out_specs=pl.BlockSpec((ATT_TILE, D_MODEL), lambda b, t: (b * tiles + t, 0)),
        out_shape=jax.ShapeDtypeStruct((T_PAD, D_MODEL), BF16),
        input_output_aliases={3: 0},
        compiler_params=_params("parallel", "parallel"),
        name="attn_prompt",
    )(qx, mk, mv, att)


ATT_SEQS = 2


def _attn_sample_body(q_ref, k_ref, v_ref, o_ref):
    for b in range(ATT_SEQS):
        q = q_ref[b:b + 1].astype(F32)
        s = jnp.sum(k_ref[0, b] * q, axis=-1, keepdims=True) * (X_HEAD_DIM ** -0.5)
        s = s - jnp.max(s, axis=0, keepdims=True)
        e = jnp.exp(s)
        p = e / jnp.sum(e, axis=0, keepdims=True)
        o_ref[b] = jnp.sum(p * v_ref[0, b], axis=0).astype(BF16)


def _attn_sample(q, ck, cv):
    kv_spec = pl.BlockSpec((1, ATT_SEQS, MEM_LEN, X_HEADS, X_HEAD_DIM), lambda i: (0, i, 0, 0, 0))
    return pl.pallas_call(
        _attn_sample_body,
        grid=(T_SAMPLE // ATT_SEQS,),
        in_specs=[pl.BlockSpec((ATT_SEQS, X_HEADS, X_HEAD_DIM), lambda i: (i, 0, 0)), kv_spec, kv_spec],
        out_specs=pl.BlockSpec((ATT_SEQS, X_HEADS, X_HEAD_DIM), lambda i: (i, 0, 0)),
        out_shape=jax.ShapeDtypeStruct((T_SAMPLE, X_HEADS, X_HEAD_DIM), BF16),
        compiler_params=_params("parallel"),
        name="attn_sample",
    )(q, ck, cv)


def _odd_even_merge_sort_pairs(n):
    pairs = []
    p = 1
    while p < n:
        k = p
        while k >= 1:
            for j in range(k % p, n - k, 2 * k):
                for i in range(min(k, n - j - k)):
                    if (i + j) // (2 * p) == (i + j + k) // (2 * p):
                        pairs.append((i + j, i + j + k))
            k //= 2
        p *= 2
    return pairs


_SORT16 = tuple(_odd_even_merge_sort_pairs(PEER_TOPK))


def _exchange(vs, i, j):
    vs[i], vs[j] = jnp.maximum(vs[i], vs[j]), jnp.minimum(vs[i], vs[j])


def _top16_sorted(s):
    vs = [s[8 * k:8 * (k + 1), :] for k in range(PEER_TOPK)]
    for i, j in _SORT16:
        _exchange(vs, i, j)
    for shift in (4, 2, 1):
        vs = _merge_top16(vs, [pltpu.roll(v, shift, 0) for v in vs])
    return vs


def _merge_top16(a, b):
    c = list(a)
    for k, bk in enumerate(b):
        c[PEER_TOPK - 1 - k] = jnp.maximum(a[PEER_TOPK - 1 - k], bk)
    for stride in (8, 4, 2, 1):
        for k in range(PEER_TOPK):
            if k & stride == 0:
                _exchange(c, k, k + stride)
    return c


def _stack_rows(vs, rows8):
    out = vs[0]
    for k in range(1, 8):
        out = jnp.where(rows8 == k, vs[k], out)
    return out


_CAND_ROWS = tuple((a, PEER_TOPK // (a + 1)) for a in range(1, 8))


def _peer_route_body(pq_ref, keys_ref, s0_ref, s1_ref, a_ref, b_ref, th_ref):
    rows8 = lax.broadcasted_iota(jnp.int32, (8, LANES), 0)

    def head(h, carry):
        q0 = pq_ref[:, pl.ds(pl.multiple_of(h * 2 * LANES, LANES), LANES)]
        q1 = pq_ref[:, pl.ds(pl.multiple_of(h * 2 * LANES + LANES, LANES), LANES)]
        nt = (((1,), (1,)), ((), ()))
        s0 = lax.dot_general(keys_ref[2 * h], q0, nt, preferred_element_type=F32)
        s1 = lax.dot_general(keys_ref[2 * h + 1], q1, nt, preferred_element_type=F32)
        r0 = _top16_sorted(s0)
        r1 = _top16_sorted(s1)
        best = [r0[0] + r1[b] for b in range(PEER_TOPK)]
        for a, nb in _CAND_ROWS:
            best = _merge_top16(best, [r0[a] + r1[b] for b in range(nb)])
        best = _merge_top16(best, [r0[a] + r1[0] for a in range(8, PEER_TOPK)])
        theta = best[PEER_TOPK - 1][0:1]
        top = best[0][0:1]
        v1_lo = _stack_rows(r1[0:8], rows8)
        cs = [r0[0] + v1_lo, r0[0] + _stack_rows(r1[8:16], rows8)]
        for a, nb in _CAND_ROWS:
            cs.append(jnp.where(rows8 < nb, r0[a] + v1_lo, -jnp.inf))
        cs.append(_stack_rows(r0[8:16], rows8) + r1[0])
        cand = jnp.concatenate(cs, axis=0)
        zsum = jnp.sum(jnp.where(cand >= theta, jnp.exp(cand - top), 0.0), axis=0, keepdims=True)
        s0_ref[h] = s0
        s1_ref[h] = s1
        a_ref[h] = jnp.exp(s0 - r0[0][0:1]) / zsum
        b_ref[h] = jnp.exp(s1 - r1[0][0:1])
        th_ref[h] = theta
        return carry

    lax.fori_loop(0, PEER_HEADS, head, 0)


def _peer_route(pq, keys):
    big = jax.ShapeDtypeStruct((PEER_HEADS, N_KEYS, T_PAD), F32)
    big_spec = pl.BlockSpec((PEER_HEADS, N_KEYS, LANES), lambda i: (0, 0, i))
    return pl.pallas_call(
        _peer_route_body,
        grid=(T_PAD // LANES,),
        in_specs=[pl.BlockSpec((LANES, D_MODEL), lambda i: (i, 0)),
                  pl.BlockSpec((2 * PEER_HEADS, N_KEYS, LANES), lambda i: (0, 0, 0))],
        out_specs=[big_spec, big_spec, big_spec, big_spec,
                   pl.BlockSpec((PEER_HEADS, 1, LANES), lambda i: (0, 0, i))],
        out_shape=[big, big, big, big, jax.ShapeDtypeStruct((PEER_HEADS, 1, T_PAD), F32)],
        compiler_params=_params("parallel"),
        name="peer_route",
    )(pq, keys)


N_EXPERT_BLOCKS = N_EXPERTS // EXPERT_TILE
N_TOKEN_BLOCKS = T_PAD // TOKEN_TILE
N_PAIRS = N_TOKEN_BLOCKS * N_EXPERT_BLOCKS
PIPE_LAG = 2
GATE_ROWS = 32
N_GROUPS = EXPERT_TILE // N_KEYS


def _pair(n, lag):
    c = jnp.clip(n - lag, 0, N_PAIRS - 1)
    return c // N_EXPERT_BLOCKS, c % N_EXPERT_BLOCKS


def _peer_dense_body(h_ref, u_ref, vt_ref, s0_ref, s1_ref, a_ref, b_ref, th_ref, o_ref, s_even, s_odd, a_even, a_odd):
    n = pl.program_id(0)

    @pl.when(n == 0)
    def _():
        for ref in (s_even, s_odd, a_even, a_odd):
            ref[...] = jnp.zeros_like(ref)

    @pl.when(jnp.logical_or(n == 0, jnp.maximum(n - PIPE_LAG, 0) % N_EXPERT_BLOCKS == 0))
    def _():
        o_ref[...] = jnp.zeros_like(o_ref)

    def run(s_w, s_r, a_w, a_r):
        def cols(nn, carry):
            t0 = pl.multiple_of(nn * MXU_DIM, MXU_DIM)
            tok = pl.ds(t0, MXU_DIM)
            o_ref[:, tok] += jnp.dot(vt_ref[...], a_r[:, tok], preferred_element_type=F32)
            s_w[:, tok] = jnp.dot(u_ref[...], h_ref[:, tok].astype(F32), preferred_element_type=F32)
            for g in range(N_GROUPS):
                for lt in range(MXU_DIM // LANES):
                    lanes = pl.ds(pl.multiple_of(t0 + lt * LANES, LANES), LANES)
                    for j0 in range(0, N_KEYS, GATE_ROWS):
                        keys = slice(j0, j0 + GATE_ROWS)
                        rows = slice(g * N_KEYS + j0, g * N_KEYS + j0 + GATE_ROWS)
                        gate = jnp.zeros((GATE_ROWS, LANES), F32)
                        for h in range(PEER_HEADS):
                            c = s0_ref[h, g:g + 1, lanes] + s1_ref[h, keys, lanes]
                            w = a_ref[h, g:g + 1, lanes] * b_ref[h, keys, lanes]
                            gate = gate + jnp.where(c >= th_ref[h, :, lanes], w, 0.0)
                        a_w[rows, lanes] = (_gelu(s_r[rows, lanes]) * gate).astype(BF16)
            return carry

        lax.fori_loop(0, TOKEN_TILE // MXU_DIM, cols, 0)

    @pl.when(n % 2 == 0)
    def _():
        run(s_even, s_odd, a_odd, a_even)

    @pl.when(n % 2 == 1)
    def _():
        run(s_odd, s_even, a_even, a_odd)


def _peer_dense(hn_t, u, vt_bf, s0, s1, a, b, th):
    once = pl.Buffered(1)
    route_spec = pl.BlockSpec((PEER_HEADS, N_KEYS, TOKEN_TILE), lambda n: (0, 0, _pair(n, 1)[0]), pipeline_mode=once)
    row_spec = pl.BlockSpec((PEER_HEADS, EXPERT_TILE // N_KEYS, TOKEN_TILE),
                            lambda n: (0, _pair(n, 1)[1], _pair(n, 1)[0]))
    return pl.pallas_call(
        _peer_dense_body,
        grid=(N_PAIRS + PIPE_LAG,),
        in_specs=[pl.BlockSpec((D_MODEL, TOKEN_TILE), lambda n: (0, _pair(n, 0)[0]), pipeline_mode=once),
                  pl.BlockSpec((EXPERT_TILE, D_MODEL), lambda n: (_pair(n, 0)[1], 0)),
                  pl.BlockSpec((D_MODEL, EXPERT_TILE), lambda n: (0, _pair(n, PIPE_LAG)[1])),
                  row_spec, route_spec, row_spec, route_spec,
                  pl.BlockSpec((PEER_HEADS, 1, TOKEN_TILE), lambda n: (0, 0, _pair(n, 1)[0]))],
        out_specs=pl.BlockSpec((D_MODEL, TOKEN_TILE), lambda n: (0, _pair(n, PIPE_LAG)[0])),
        out_shape=jax.ShapeDtypeStruct((D_MODEL, T_PAD), F32),
        scratch_shapes=[pltpu.VMEM((EXPERT_TILE, TOKEN_TILE), F32), pltpu.VMEM((EXPERT_TILE, TOKEN_TILE), F32),
                        pltpu.VMEM((EXPERT_TILE, TOKEN_TILE), BF16), pltpu.VMEM((EXPERT_TILE, TOKEN_TILE), BF16)],
        compiler_params=_params("arbitrary", vmem=PEER_VMEM_LIMIT),
        name="peer_dense",
    )(hn_t, u, vt_bf, s0, s1, a, b, th)


def _final_body(x_ref, pt_ref, g_ref, o_ref):
    x = x_ref[...] + pt_ref[...].T
    ms = jnp.mean(x * x, axis=-1, keepdims=True)
    o_ref[...] = (x * lax.rsqrt(ms + EPS)) * g_ref[...]


def _final(x2, peer_t, g, first_row, n_rows, tile):
    first_tile = first_row // tile
    return pl.pallas_call(
        _final_body,
        grid=(n_rows // tile,),
        in_specs=[pl.BlockSpec((tile, D_MODEL), lambda i: (first_tile + i, 0)),
                  pl.BlockSpec((D_MODEL, tile), lambda i: (0, first_tile + i)),
                  pl.BlockSpec((1, D_MODEL), lambda i: (0, 0))],
        out_specs=pl.BlockSpec((tile, D_MODEL), lambda i: (i, 0)),
        out_shape=jax.ShapeDtypeStruct((n_rows, D_MODEL), F32),
        compiler_params=_params("parallel"),
        name="final_norm",
    )(x2, peer_t, g.reshape(1, D_MODEL))


def kernel(x_prompt, x_sample, mem_prompt, cache_mem_k, cache_mem_v, state_gla, g_mix, w_in, w_alpha, b_alpha, w_s, b_s, g_sgu, g_gla, w_out, g_mem, w_xk, w_xv, g_xq, w_xq, w_xo, g_ffn, w_pq, sub_keys, u_emb, v_emb, g_final):
    w_in_main = w_in[0, :, :Z_COLS].astype(BF16)
    w_in_gate = jnp.pad(w_in[0, :, Z_COLS:], ((0, 0), (0, LANES - GATE_RANK))).astype(BF16)
    wa_pad = jnp.pad(w_alpha[0], ((0, LANES - GATE_RANK), (0, 0))).astype(BF16)
    ba = b_alpha[0].reshape(1, B_KEY_WIDTH)
    causal = jnp.tril(jnp.ones((SGU_CHUNK, SGU_CHUNK), dtype=bool))
    ws_tril = jnp.where(causal, w_s[0], 0).astype(BF16)
    bs_t = b_s[0].T
    w00 = jnp.repeat(w_s[0, :, 0, 0], LANES).reshape(1, A_WIDTH)
    b00 = jnp.repeat(b_s[0, :, 0], LANES).reshape(1, A_WIDTH)
    gs = g_sgu[0].reshape(1, A_WIDTH)
    gg = g_gla[0].reshape(1, B_VAL_WIDTH)
    keys = sub_keys[0].reshape(2 * PEER_HEADS, N_KEYS, LANES).astype(BF16)
    vt_bf = v_emb[0].T.astype(BF16)

    x_all = jnp.concatenate([x_prompt.reshape(T_PROMPT, D_MODEL), x_sample.reshape(T_SAMPLE, D_MODEL),
                             jnp.zeros((T_PAD - T_ALL, D_MODEL), F32)], axis=0)

    mem = mem_prompt.reshape(BATCH * MEM_LEN, D_MODEL)
    mk, = _nmm(mem, g_mem[0], w_xk[0].astype(BF16), tm=512, tn=1024)
    mv, = _nmm(mem, g_mem[0], w_xv[0].astype(BF16), tm=512, tn=1024)

    z, zg = _nmm(x_all, g_mix[0], w_in_main, tm=TOKEN_TILE, tn=1280, w_aux=w_in_gate)
    mix, cv_p = _sgu_prompt(z, ws_tril, bs_t, gs, jnp.zeros((T_PAD, D_MODEL), F32))
    mix, st_p = _gla_prompt(z, zg, wa_pad, ba, gg, mix)
    mix, cv_s, st_s = _sample_mixers(z, zg, state_gla[0], wa_pad, ba, gs, gg, w00, b00, mix)
    x1 = _mmres(mix, w_out[0].astype(BF16), x_all, tm=TOKEN_TILE)

    qx, = _nmm(x1, g_xq[0], w_xq[0].astype(BF16), tm=TOKEN_TILE, tn=D_MODEL, out_dtype=BF16)
    att = _attn_prompt(qx, mk.reshape(BATCH, MEM_LEN, D_MODEL), mv.reshape(BATCH, MEM_LEN, D_MODEL),
                       jnp.zeros((T_PAD, D_MODEL), BF16))
    att_s = _attn_sample(qx[T_PROMPT:T_ALL].reshape(T_SAMPLE, X_HEADS, X_HEAD_DIM), cache_mem_k, cache_mem_v)
    att = lax.dynamic_update_slice(att, att_s.reshape(T_SAMPLE, D_MODEL), (T_PROMPT, 0))
    x2 = _mmres(att, w_xo[0].astype(BF16), x1, tm=TOKEN_TILE)

    pq, hn = _nmm(x2, g_ffn[0], w_pq[0].astype(BF16), tm=TOKEN_TILE, tn=D_MODEL, emit_xn=True, out_dtype=BF16)
    s0, s1, a, b, th = _peer_route(pq, keys)
    peer_t = _peer_dense(hn.T, u_emb[0], vt_bf, s0, s1, a, b, th)

    y_p = _final(x2, peer_t, g_final, 0, T_PROMPT, 2 * LANES)
    y_s = _final(x2, peer_t, g_final, T_PROMPT, T_SAMPLE, LANES)

    return (y_p.reshape(BATCH, SEQ, D_MODEL),
            y_s.reshape(DEC_BATCH, 1, D_MODEL),
            mk.reshape(1, BATCH, MEM_LEN, X_HEADS, X_HEAD_DIM),
            mv.reshape(1, BATCH, MEM_LEN, X_HEADS, X_HEAD_DIM),
            st_p.reshape(1, BATCH, B_HEADS, B_KEY_DIM, B_VAL_DIM),
            st_s.reshape(1, DEC_BATCH, B_HEADS, B_KEY_DIM, B_VAL_DIM),
            cv_p.reshape(1, BATCH, SGU_CHUNK, A_HEADS, LANES),
            cv_s.reshape(1, DEC_BATCH, 1, A_HEADS, LANES))
```

```python
import functools

import jax
import jax.numpy as jnp
from jax import lax
from jax.experimental import pallas as pl
from jax.experimental.pallas import tpu as pltpu

F32 = jnp.float32
BF16 = jnp.bfloat16

D_MODEL = 2048
BATCH = 4
SEQ = 2048
DEC_BATCH = 128
T_PROMPT = BATCH * SEQ
T_SAMPLE = DEC_BATCH
T_ALL = T_PROMPT + T_SAMPLE

A_HEADS = 8
A_WIDTH = 1024
SGU_CHUNK = 128
B_HEADS = 4
B_KEY_DIM = 128
B_VAL_DIM = 256
B_KEY_WIDTH = 512
B_VAL_WIDTH = 1024
GATE_RANK = 16
GATE_TAU = 16.0
GLA_CHUNK = 64
GLA_SUB = 16
Z_COLS = 2 * A_WIDTH + 2 * B_KEY_WIDTH + 2 * B_VAL_WIDTH
MEM_LEN = 256
X_HEADS = 4
X_HEAD_DIM = 512
N_KEYS = 128
N_EXPERTS = N_KEYS * N_KEYS
PEER_HEADS = 8
PEER_TOPK = 16
EPS = 1e-6

LANES = 128
MXU_DIM = 256
TOKEN_TILE = 3 * MXU_DIM
T_PAD = 11 * TOKEN_TILE
EXPERT_TILE = 1024
VMEM_LIMIT = 56 * 1024 * 1024
PEER_VMEM_LIMIT = 60 * 1024 * 1024


def _params(*sem, vmem=VMEM_LIMIT):
    return pltpu.CompilerParams(dimension_semantics=sem, vmem_limit_bytes=vmem)


_GELU_2C = 2.0 * 0.7978845608028654
_LOG2E = 1.4426950408889634


def _gelu(x):
    p = (x * x) * (-_LOG2E * _GELU_2C * 0.044715) + (-_LOG2E * _GELU_2C)
    return x / (1.0 + jnp.exp2(x * p))


def _log_sigmoid(x):
    return jnp.minimum(x, 0.0) - jnp.log1p(jnp.exp(-jnp.abs(x)))


def _row_to_col(row, eye):
    return jnp.sum(jnp.where(eye, row, 0.0), axis=-1, keepdims=True)


def _eye(n):
    return lax.broadcasted_iota(jnp.int32, (n, n), 0) == lax.broadcasted_iota(jnp.int32, (n, n), 1)


def _nmm_body(*refs, has_aux, emit_xn):
    x_ref, g_ref, w_ref = refs[:3]
    rest = list(refs[3:])
    w2_ref = rest.pop(0) if has_aux else None
    o_ref = rest.pop(0)
    o2_ref = rest.pop(0) if has_aux else None
    xo_ref = rest.pop(0) if emit_xn else None
    xn_ref = rest.pop(0)

    @pl.when(pl.program_id(1) == 0)
    def _():
        x = x_ref[...]
        ms = jnp.mean(x * x, axis=-1, keepdims=True)
        xn = ((x * lax.rsqrt(ms + EPS)) * g_ref[...]).astype(BF16)
        xn_ref[...] = xn
        if has_aux:
            o2_ref[...] = jnp.dot(xn, w2_ref[...], preferred_element_type=F32)
        if emit_xn:
            xo_ref[...] = xn

    o_ref[...] = jnp.dot(xn_ref[...], w_ref[...], preferred_element_type=F32).astype(o_ref.dtype)


def _nmm(x, g, w, *, tm, tn, w_aux=None, emit_xn=False, out_dtype=F32):
    m, k = x.shape
    n = w.shape[1]
    w_mode = pl.Buffered(1) if tn == n else None
    in_specs = [pl.BlockSpec((tm, k), lambda i, j: (i, 0)),
                pl.BlockSpec((1, k), lambda i, j: (0, 0)),
                pl.BlockSpec((k, tn), lambda i, j: (0, j), pipeline_mode=w_mode)]
    args = [x, g.reshape(1, k), w]
    out_shape = [jax.ShapeDtypeStruct((m, n), out_dtype)]
    out_specs = [pl.BlockSpec((tm, tn), lambda i, j: (i, j))]
    if w_aux is not None:
        in_specs.append(pl.BlockSpec((k, LANES), lambda i, j: (0, 0)))
        args.append(w_aux)
        out_shape.append(jax.ShapeDtypeStruct((m, LANES), F32))
        out_specs.append(pl.BlockSpec((tm, LANES), lambda i, j: (i, 0)))
    if emit_xn:
        out_shape.append(jax.ShapeDtypeStruct((m, k), BF16))
        out_specs.append(pl.BlockSpec((tm, k), lambda i, j: (i, 0)))
    return pl.pallas_call(
        functools.partial(_nmm_body, has_aux=w_aux is not None, emit_xn=emit_xn),
        grid=(m // tm, n // tn),
        in_specs=in_specs, out_specs=out_specs, out_shape=out_shape,
        scratch_shapes=[pltpu.VMEM((tm, k), BF16)],
        compiler_params=_params("parallel", "arbitrary"),
        name="nmm",
    )(*args)


def _mmres_body(x_ref, w_ref, r_ref, o_ref):
    o_ref[...] = r_ref[...] + jnp.dot(x_ref[...].astype(BF16), w_ref[...], preferred_element_type=F32)


def _mmres(x, w, res, *, tm):
    m, k = x.shape
    n = w.shape[1]
    return pl.pallas_call(
        _mmres_body,
        grid=(m // tm,),
        in_specs=[pl.BlockSpec((tm, k), lambda i: (i, 0)),
                  pl.BlockSpec((k, n), lambda i: (0, 0), pipeline_mode=pl.Buffered(1)),
                  pl.BlockSpec((tm, n), lambda i: (i, 0))],
        out_specs=pl.BlockSpec((tm, n), lambda i: (i, 0)),
        out_shape=jax.ShapeDtypeStruct((m, n), F32),
        compiler_params=_params("parallel"),
        name="mmres",
    )(x, w, res)


SGU_STEP_CHUNKS = 2


def _sgu_body(u_ref, v_ref, ws_ref, bs_ref, gs_ref, mix_ref, o_ref, cv_ref):
    del mix_ref
    for c in range(SGU_STEP_CHUNKS):
        rows = slice(c * SGU_CHUNK, (c + 1) * SGU_CHUNK)
        for h in range(A_HEADS):
            sl = slice(h * LANES, (h + 1) * LANES)
            u = _gelu(u_ref[rows, sl])
            v = _gelu(v_ref[rows, sl])
            ms = jnp.mean(v * v, axis=-1, keepdims=True)
            va = (v * lax.rsqrt(ms + EPS)) * gs_ref[:, sl]
            mixed = jnp.dot(ws_ref[h], va.astype(BF16), preferred_element_type=F32) + bs_ref[:, h:h + 1]
            o_ref[rows, sl] = u * mixed
            if c == SGU_STEP_CHUNKS - 1:
                cv_ref[0, :, sl] = va


def _sgu_prompt(z, ws_tril, bs_t, gs, mix):
    step_rows = SGU_STEP_CHUNKS * SGU_CHUNK
    n_steps = SEQ // step_rows
    return pl.pallas_call(
        _sgu_body,
        grid=(T_PROMPT // step_rows,),
        in_specs=[pl.BlockSpec((step_rows, A_WIDTH), lambda i: (i, 0)),
                  pl.BlockSpec((step_rows, A_WIDTH), lambda i: (i, 1)),
                  pl.BlockSpec((A_HEADS, SGU_CHUNK, SGU_CHUNK), lambda i: (0, 0, 0)),
                  pl.BlockSpec((SGU_CHUNK, A_HEADS), lambda i: (0, 0)),
                  pl.BlockSpec((1, A_WIDTH), lambda i: (0, 0)),
                  pl.BlockSpec(memory_space=pl.ANY)],
        out_specs=[pl.BlockSpec((step_rows, A_WIDTH), lambda i: (i, 0)),
                   pl.BlockSpec((1, SGU_CHUNK, A_WIDTH), lambda i: (i // n_steps, 0, 0))],
        out_shape=[jax.ShapeDtypeStruct((T_PAD, D_MODEL), F32),
                   jax.ShapeDtypeStruct((BATCH, SGU_CHUNK, A_WIDTH), F32)],
        input_output_aliases={5: 0},
        compiler_params=_params("arbitrary"),
        name="sgu_prompt",
    )(z, z, ws_tril, bs_t, gs, mix)


GLA_STEP_CHUNKS = 2


def _gla_body(q_ref, k_ref, v_ref, r_ref, zg_ref, wa_ref, ba_ref, gg_ref, mix_ref, o_ref, st_ref, s_ref, g_ref):
    del mix_ref
    c = pl.program_id(1)

    @pl.when(c == 0)
    def _():
        s_ref[...] = jnp.zeros_like(s_ref)

    for cc in range(GLA_STEP_CHUNKS):
        row0 = cc * GLA_CHUNK
        rows = slice(row0, row0 + GLA_CHUNK)
        x = jnp.dot(zg_ref[rows, :].astype(BF16), wa_ref[...], preferred_element_type=F32) + ba_ref[...]
        lg = _log_sigmoid(x) * (1.0 / GATE_TAU)
        tri = (lax.broadcasted_iota(jnp.int32, (GLA_CHUNK, GLA_CHUNK), 1)
               <= lax.broadcasted_iota(jnp.int32, (GLA_CHUNK, GLA_CHUNK), 0)).astype(BF16)
        hi = lg.astype(BF16)
        r1 = lg - hi.astype(F32)
        mid = r1.astype(BF16)
        lo = (r1 - mid.astype(F32)).astype(BF16)
        g_ref[cc] = (jnp.dot(tri, hi, preferred_element_type=F32) + jnp.dot(tri, mid, preferred_element_type=F32)
                     + jnp.dot(tri, lo, preferred_element_type=F32))
        eye = _eye(B_KEY_DIM)
        half = GLA_SUB // 2
        t_iota = lax.broadcasted_iota(jnp.int32, (half, B_KEY_DIM), 0)
        scale = B_KEY_DIM ** -0.5

        for h in range(B_HEADS):
            dk = slice(h * B_KEY_DIM, (h + 1) * B_KEY_DIM)
            dv = slice(h * B_VAL_DIM, (h + 1) * B_VAL_DIM)
            k = k_ref[rows, dk]
            g = g_ref[cc, :, dk]
            vb = v_ref[rows, dv].astype(BF16)
            state = s_ref[h]
            o_inter = jnp.dot((q_ref[rows, dk] * scale * jnp.exp(g)).astype(BF16), state.astype(BF16),
                              preferred_element_type=F32)
            parts = []
            for i in range(GLA_CHUNK // GLA_SUB):
                lo_r, mid_r, hi_r = GLA_SUB * i, GLA_SUB * i + half, GLA_SUB * (i + 1)
                q_lo = q_ref[row0 + lo_r:row0 + mid_r, dk] * scale
                q_hi = q_ref[row0 + mid_r:row0 + hi_r, dk] * scale
                g_lo, g_hi = g_ref[cc, lo_r:mid_r, dk], g_ref[cc, mid_r:hi_r, dk]
                acc_lo = jnp.zeros((half, B_VAL_DIM), F32)
                acc_hi = jnp.zeros((half, B_VAL_DIM), F32)
                for s in range(GLA_SUB):
                    r = lo_r + s
                    k_row, v_row = k_ref[row0 + r:row0 + r + 1, dk], v_ref[row0 + r:row0 + r + 1, dv]
                    g_row = g_ref[cc, r:r + 1, dk]
                    if s < half:
                        dec = jnp.exp(jnp.where(t_iota >= s, g_lo - g_row, -jnp.inf))
                        col = jnp.sum(q_lo * k_row * dec, axis=-1, keepdims=True)
                        acc_lo = acc_lo + col * v_row
                        dec = jnp.exp(g_hi - g_row)
                    else:
                        dec = jnp.exp(jnp.where(t_iota >= s - half, g_hi - g_row, -jnp.inf))
                    col = jnp.sum(q_hi * k_row * dec, axis=-1, keepdims=True)
                    acc_hi = acc_hi + col * v_row
                acc = jnp.concatenate([acc_lo, acc_hi], axis=0)
                if i > 0:
                    gref = g_ref[cc, lo_r - 1:lo_r, dk]
                    qd = (jnp.concatenate([q_lo, q_hi], axis=0) * jnp.exp(g[lo_r:hi_r] - gref)).astype(BF16)
                    kd = (k[:lo_r] * jnp.exp(gref - g[:lo_r])).astype(BF16)
                    sc = lax.dot_general(qd, kd, (((1,), (1,)), ((), ())), preferred_element_type=F32)
                    acc = acc + jnp.dot(sc.astype(BF16), vb[:lo_r], preferred_element_type=F32)
                parts.append(acc)
            o = o_inter + jnp.concatenate(parts, axis=0)

            g_last = g_ref[cc, GLA_CHUNK - 1:GLA_CHUNK, dk]
            kdec = k * jnp.exp(g_last - g)
            upd = jnp.dot(kdec.T.astype(BF16), vb, preferred_element_type=F32)
            s_ref[h] = _row_to_col(jnp.exp(g_last), eye) * state + upd

            ms = jnp.mean(o * o, axis=-1, keepdims=True)
            on = (o * lax.rsqrt(ms + EPS)) * gg_ref[:, dv]
            r = r_ref[rows, dv]
            o_ref[rows, dv] = (r * jax.nn.sigmoid(r)) * on

    @pl.when(c == pl.num_programs(1) - 1)
    def _():
        st_ref[0] = s_ref[...]


def _gla_prompt(z, zg, wa_pad, ba, gg, mix):
    step_rows = GLA_STEP_CHUNKS * GLA_CHUNK
    n_steps = SEQ // step_rows
    row = lambda b, c: b * n_steps + c
    return pl.pallas_call(
        _gla_body,
        grid=(BATCH, n_steps),
        in_specs=[pl.BlockSpec((step_rows, B_KEY_WIDTH), lambda b, c: (row(b, c), 4)),
                  pl.BlockSpec((step_rows, B_KEY_WIDTH), lambda b, c: (row(b, c), 5)),
                  pl.BlockSpec((step_rows, B_VAL_WIDTH), lambda b, c: (row(b, c), 3)),
                  pl.BlockSpec((step_rows, B_VAL_WIDTH), lambda b, c: (row(b, c), 4)),
                  pl.BlockSpec((step_rows, LANES), lambda b, c: (row(b, c), 0)),
                  pl.BlockSpec((LANES, B_KEY_WIDTH), lambda b, c: (0, 0)),
                  pl.BlockSpec((1, B_KEY_WIDTH), lambda b, c: (0, 0)),
                  pl.BlockSpec((1, B_VAL_WIDTH), lambda b, c: (0, 0)),
                  pl.BlockSpec(memory_space=pl.ANY)],
        out_specs=[pl.BlockSpec((step_rows, B_VAL_WIDTH), lambda b, c: (row(b, c), 1)),
                   pl.BlockSpec((1, B_HEADS, B_KEY_DIM, B_VAL_DIM), lambda b, c: (b, 0, 0, 0))],
        out_shape=[jax.ShapeDtypeStruct((T_PAD, D_MODEL), F32),
                   jax.ShapeDtypeStruct((BATCH, B_HEADS, B_KEY_DIM, B_VAL_DIM), F32)],
        scratch_shapes=[pltpu.VMEM((B_HEADS, B_KEY_DIM, B_VAL_DIM), F32),
                        pltpu.VMEM((GLA_STEP_CHUNKS, GLA_CHUNK, B_KEY_WIDTH), F32)],
        input_output_aliases={8: 0},
        compiler_params=_params("arbitrary", "arbitrary"),
        name="gla_prompt",
    )(z, z, z, z, zg, wa_pad, ba, gg, mix)


SAMPLE_ROWS = 8


def _smix_body(z_ref, zg_ref, st_ref, wa_ref, ba_ref, gs_ref, gg_ref, w00_ref, b00_ref, mix_ref,
               o_ref, cv_ref, so_ref):
    del mix_ref
    u = _gelu(z_ref[:, 0:A_WIDTH])
    v = _gelu(z_ref[:, A_WIDTH:2 * A_WIDTH])
    for h in range(A_HEADS):
        sl = slice(h * LANES, (h + 1) * LANES)
        vh = v[:, sl]
        ms = jnp.mean(vh * vh, axis=-1, keepdims=True)
        va = (vh * lax.rsqrt(ms + EPS)) * gs_ref[:, sl]
        cv_ref[:, sl] = va
        o_ref[:, sl] = u[:, sl] * (w00_ref[:, sl] * va + b00_ref[:, sl])

    x = jnp.dot(zg_ref[...].astype(BF16), wa_ref[...], preferred_element_type=F32) + ba_ref[...]
    a = jnp.exp(_log_sigmoid(x) * (1.0 / GATE_TAU))
    q0, k0, v0, r0 = 2 * A_WIDTH, 2 * A_WIDTH + B_KEY_WIDTH, 2 * A_WIDTH + 2 * B_KEY_WIDTH, Z_COLS - B_VAL_WIDTH
    eye = _eye(B_KEY_DIM)
    for b in range(SAMPLE_ROWS):
        for h in range(B_HEADS):
            dk = slice(h * B_KEY_DIM, (h + 1) * B_KEY_DIM)
            a_col = _row_to_col(a[b:b + 1, dk], eye)
            k_col = _row_to_col(z_ref[b:b + 1, k0 + h * B_KEY_DIM:k0 + (h + 1) * B_KEY_DIM], eye)
            q_col = _row_to_col(z_ref[b:b + 1, q0 + h * B_KEY_DIM:q0 + (h + 1) * B_KEY_DIM], eye) * (B_KEY_DIM ** -0.5)
            vrow = z_ref[b:b + 1, v0 + h * B_VAL_DIM:v0 + (h + 1) * B_VAL_DIM]
            s_new = a_col * st_ref[b, h] + k_col * vrow
            so_ref[b, h] = s_new
            o = jnp.sum(q_col * s_new, axis=0, keepdims=True)
            ms = jnp.mean(o * o, axis=-1, keepdims=True)
            on = (o * lax.rsqrt(ms + EPS)) * gg_ref[:, h * B_VAL_DIM:(h + 1) * B_VAL_DIM]
            r = z_ref[b:b + 1, r0 + h * B_VAL_DIM:r0 + (h + 1) * B_VAL_DIM]
            o_ref[b:b + 1, A_WIDTH + h * B_VAL_DIM:A_WIDTH + (h + 1) * B_VAL_DIM] = (r * jax.nn.sigmoid(r)) * on


def _sample_mixers(z, zg, state, wa_pad, ba, gs, gg, w00, b00, mix):
    first = T_PROMPT // SAMPLE_ROWS
    return pl.pallas_call(
        _smix_body,
        grid=(T_SAMPLE // SAMPLE_ROWS,),
        in_specs=[pl.BlockSpec((SAMPLE_ROWS, Z_COLS), lambda i: (first + i, 0)),
                  pl.BlockSpec((SAMPLE_ROWS, LANES), lambda i: (first + i, 0)),
                  pl.BlockSpec((SAMPLE_ROWS, B_HEADS, B_KEY_DIM, B_VAL_DIM), lambda i: (i, 0, 0, 0)),
                  pl.BlockSpec((LANES, B_KEY_WIDTH), lambda i: (0, 0)),
                  pl.BlockSpec((1, B_KEY_WIDTH), lambda i: (0, 0)),
                  pl.BlockSpec((1, A_WIDTH), lambda i: (0, 0)),
                  pl.BlockSpec((1, B_VAL_WIDTH), lambda i: (0, 0)),
                  pl.BlockSpec((1, A_WIDTH), lambda i: (0, 0)),
                  pl.BlockSpec((1, A_WIDTH), lambda i: (0, 0)),
                  pl.BlockSpec(memory_space=pl.ANY)],
        out_specs=[pl.BlockSpec((SAMPLE_ROWS, D_MODEL), lambda i: (first + i, 0)),
                   pl.BlockSpec((SAMPLE_ROWS, A_WIDTH), lambda i: (i, 0)),
                   pl.BlockSpec((SAMPLE_ROWS, B_HEADS, B_KEY_DIM, B_VAL_DIM), lambda i: (i, 0, 0, 0))],
        out_shape=[jax.ShapeDtypeStruct((T_PAD, D_MODEL), F32),
                   jax.ShapeDtypeStruct((T_SAMPLE, A_WIDTH), F32),
                   jax.ShapeDtypeStruct((T_SAMPLE, B_HEADS, B_KEY_DIM, B_VAL_DIM), F32)],
        input_output_aliases={9: 0},
        compiler_params=_params("parallel"),
        name="sample_mixers",
    )(z, zg, state, wa_pad, ba, gs, gg, w00, b00, mix)


ATT_TILE = 1024


def _attn_prompt_body(q_ref, k_ref, v_ref, att_ref, o_ref):
    del att_ref
    for h in range(X_HEADS):
        sl = slice(h * X_HEAD_DIM, (h + 1) * X_HEAD_DIM)
        q = q_ref[:, sl]
        k = k_ref[0, :, sl].astype(BF16)
        s = lax.dot_general(q, k, (((1,), (1,)), ((), ())), preferred_element_type=F32) * (X_HEAD_DIM ** -0.5)
        s = s - jnp.max(s, axis=-1, keepdims=True)
        e = jnp.exp(s)
        p = e / jnp.sum(e, axis=-1, keepdims=True)
        o_ref[:, sl] = jnp.dot(p.astype(BF16), v_ref[0, :, sl].astype(BF16), preferred_element_type=F32).astype(BF16)


def _attn_prompt(qx, mk, mv, att):
    tiles = SEQ // ATT_TILE
    return pl.pallas_call(
        _attn_prompt_body,
        grid=(BATCH, tiles),
        in_specs=[pl.BlockSpec((ATT_TILE, D_MODEL), lambda b, t: (b * tiles + t, 0)),
                  pl.BlockSpec((1, MEM_LEN, D_MODEL), lambda b, t: (b, 0, 0)),
                  pl.BlockSpec((1, MEM_LEN, D_MODEL), lambda b, t: (b, 0, 0)),
                  pl.BlockSpec(memory_space=pl.ANY)],
        out_specs=pl.BlockSpec((ATT_TILE, D_MODEL), lambda b, t: (b * tiles + t, 0)),
        out_shape=jax.ShapeDtypeStruct((T_PAD, D_MODEL), BF16),
        input_output_aliases={3: 0},
        compiler_params=_params("parallel", "parallel"),
        name="attn_prompt",
    )(qx, mk, mv, att)


ATT_SEQS = 2


def _attn_sample_body(q_ref, k_ref, v_ref, o_ref):
    for b in range(ATT_SEQS):
        q = q_ref[b:b + 1].astype(F32)
        s = jnp.sum(k_ref[0, b] * q, axis=-1, keepdims=True) * (X_HEAD_DIM ** -0.5)
        s = s - jnp.max(s, axis=0, keepdims=True)
        e = jnp.exp(s)
        p = e / jnp.sum(e, axis=0, keepdims=True)
        o_ref[b] = jnp.sum(p * v_ref[0, b], axis=0).astype(BF16)


def _attn_sample(q, ck, cv):
    kv_spec = pl.BlockSpec((1, ATT_SEQS, MEM_LEN, X_HEADS, X_HEAD_DIM), lambda i: (0, i, 0, 0, 0))
    return pl.pallas_call(
        _attn_sample_body,
        grid=(T_SAMPLE // ATT_SEQS,),
        in_specs=[pl.BlockSpec((ATT_SEQS, X_HEADS, X_HEAD_DIM), lambda i: (i, 0, 0)), kv_spec, kv_spec],
        out_specs=pl.BlockSpec((ATT_SEQS, X_HEADS, X_HEAD_DIM), lambda i: (i, 0, 0)),
        out_shape=jax.ShapeDtypeStruct((T_SAMPLE, X_HEADS, X_HEAD_DIM), BF16),
        compiler_params=_params("parallel"),
        name="attn_sample",
    )(q, ck, cv)


def _odd_even_merge_sort_pairs(n):
    pairs = []
    p = 1
    while p < n:
        k = p
        while k >= 1:
            for j in range(k % p, n - k, 2 * k):
                for i in range(min(k, n - j - k)):
                    if (i + j) // (2 * p) == (i + j + k) // (2 * p):
                        pairs.append((i + j, i + j + k))
            k //= 2
        p *= 2
    return pairs


_SORT16 = tuple(_odd_even_merge_sort_pairs(PEER_TOPK))


def _exchange(vs, i, j):
    vs[i], vs[j] = jnp.maximum(vs[i], vs[j]), jnp.minimum(vs[i], vs[j])


def _top16_sorted(s):
    vs = [s[8 * k:8 * (k + 1), :] for k in range(PEER_TOPK)]
    for i, j in _SORT16:
        _exchange(vs, i, j)
    for shift in (4, 2, 1):
        vs = _merge_top16(vs, [pltpu.roll(v, shift, 0) for v in vs])
    return vs


def _merge_top16(a, b):
    c = list(a)
    for k, bk in enumerate(b):
        c[PEER_TOPK - 1 - k] = jnp.maximum(a[PEER_TOPK - 1 - k], bk)
    for stride in (8, 4, 2, 1):
        for k in range(PEER_TOPK):
            if k & stride == 0:
                _exchange(c, k, k + stride)
    return c


def _stack_rows(vs, rows8):
    out = vs[0]
    for k in range(1, 8):
        out = jnp.where(rows8 == k, vs[k], out)
    return out


_CAND_ROWS = tuple((a, PEER_TOPK // (a + 1)) for a in range(1, 8))


def _peer_route_body(pq_ref, keys_ref, s0_ref, s1_ref, a_ref, b_ref, th_ref):
    rows8 = lax.broadcasted_iota(jnp.int32, (8, LANES), 0)

    def head(h, carry):
        q0 = pq_ref[:, pl.ds(pl.multiple_of(h * 2 * LANES, LANES), LANES)]
        q1 = pq_ref[:, pl.ds(pl.multiple_of(h * 2 * LANES + LANES, LANES), LANES)]
        nt = (((1,), (1,)), ((), ()))
        s0 = lax.dot_general(keys_ref[2 * h], q0, nt, preferred_element_type=F32)
        s1 = lax.dot_general(keys_ref[2 * h + 1], q1, nt, preferred_element_type=F32)
        r0 = _top16_sorted(s0)
        r1 = _top16_sorted(s1)
        best = [r0[0] + r1[b] for b in range(PEER_TOPK)]
        for a, nb in _CAND_ROWS:
            best = _merge_top16(best, [r0[a] + r1[b] for b in range(nb)])
        best = _merge_top16(best, [r0[a] + r1[0] for a in range(8, PEER_TOPK)])
        theta = best[PEER_TOPK - 1][0:1]
        top = best[0][0:1]
        v1_lo = _stack_rows(r1[0:8], rows8)
        cs = [r0[0] + v1_lo, r0[0] + _stack_rows(r1[8:16], rows8)]
        for a, nb in _CAND_ROWS:
            cs.append(jnp.where(rows8 < nb, r0[a] + v1_lo, -jnp.inf))
        cs.append(_stack_rows(r0[8:16], rows8) + r1[0])
        cand = jnp.concatenate(cs, axis=0)
        zsum = jnp.sum(jnp.where(cand >= theta, jnp.exp(cand - top), 0.0), axis=0, keepdims=True)
        s0_ref[h] = s0
        s1_ref[h] = s1
        a_ref[h] = jnp.exp(s0 - r0[0][0:1]) / zsum
        b_ref[h] = jnp.exp(s1 - r1[0][0:1])
        th_ref[h] = theta
        return carry

    lax.fori_loop(0, PEER_HEADS, head, 0)


def _peer_route(pq, keys):
    big = jax.ShapeDtypeStruct((PEER_HEADS, N_KEYS, T_PAD), F32)
    big_spec = pl.BlockSpec((PEER_HEADS, N_KEYS, LANES), lambda i: (0, 0, i))
    return pl.pallas_call(
        _peer_route_body,
        grid=(T_PAD // LANES,),
        in_specs=[pl.BlockSpec((LANES, D_MODEL), lambda i: (i, 0)),
                  pl.BlockSpec((2 * PEER_HEADS, N_KEYS, LANES), lambda i: (0, 0, 0))],
        out_specs=[big_spec, big_spec, big_spec, big_spec,
                   pl.BlockSpec((PEER_HEADS, 1, LANES), lambda i: (0, 0, i))],
        out_shape=[big, big, big, big, jax.ShapeDtypeStruct((PEER_HEADS, 1, T_PAD), F32)],
        compiler_params=_params("parallel"),
        name="peer_route",
    )(pq, keys)


N_EXPERT_BLOCKS = N_EXPERTS // EXPERT_TILE
N_TOKEN_BLOCKS = T_PAD // TOKEN_TILE
N_PAIRS = N_TOKEN_BLOCKS * N_EXPERT_BLOCKS
PIPE_LAG = 2
GATE_ROWS = 32
N_GROUPS = EXPERT_TILE // N_KEYS


def _pair(n, lag):
    c = jnp.clip(n - lag, 0, N_PAIRS - 1)
    return c // N_EXPERT_BLOCKS, c % N_EXPERT_BLOCKS


def _peer_dense_body(h_ref, u_ref, vt_ref, s0_ref, s1_ref, a_ref, b_ref, th_ref, o_ref, s_even, s_odd, a_even, a_odd):
    n = pl.program_id(0)

    @pl.when(n == 0)
    def _():
        for ref in (s_even, s_odd, a_even, a_odd):
            ref[...] = jnp.zeros_like(ref)

    @pl.when(jnp.logical_or(n == 0, jnp.maximum(n - PIPE_LAG, 0) % N_EXPERT_BLOCKS == 0))
    def _():
        o_ref[...] = jnp.zeros_like(o_ref)

    def run(s_w, s_r, a_w, a_r):
        def cols(nn, carry):
            t0 = pl.multiple_of(nn * MXU_DIM, MXU_DIM)
            tok = pl.ds(t0, MXU_DIM)
            o_ref[:, tok] += jnp.dot(vt_ref[...], a_r[:, tok], preferred_element_type=F32)
            s_w[:, tok] = jnp.dot(u_ref[...], h_ref[:, tok].astype(F32), preferred_element_type=F32)
            for g in range(N_GROUPS):
                for lt in range(MXU_DIM // LANES):
                    lanes = pl.ds(pl.multiple_of(t0 + lt * LANES, LANES), LANES)
                    for j0 in range(0, N_KEYS, GATE_ROWS):
                        keys = slice(j0, j0 + GATE_ROWS)
                        rows = slice(g * N_KEYS + j0, g * N_KEYS + j0 + GATE_ROWS)
                        gate = jnp.zeros((GATE_ROWS, LANES), F32)
                        for h in range(PEER_HEADS):
                            c = s0_ref[h, g:g + 1, lanes] + s1_ref[h, keys, lanes]
                            w = a_ref[h, g:g + 1, lanes] * b_ref[h, keys, lanes]
                            gate = gate + jnp.where(c >= th_ref[h, :, lanes], w, 0.0)
                        a_w[rows, lanes] = (_gelu(s_r[rows, lanes]) * gate).astype(BF16)
            return carry

        lax.fori_loop(0, TOKEN_TILE // MXU_DIM, cols, 0)

    @pl.when(n % 2 == 0)
    def _():
        run(s_even, s_odd, a_odd, a_even)

    @pl.when(n % 2 == 1)
    def _():
        run(s_odd, s_even, a_even, a_odd)


def _peer_dense(hn_t, u, vt_bf, s0, s1, a, b, th):
    once = pl.Buffered(1)
    route_spec = pl.BlockSpec((PEER_HEADS, N_KEYS, TOKEN_TILE), lambda n: (0, 0, _pair(n, 1)[0]), pipeline_mode=once)
    row_spec = pl.BlockSpec((PEER_HEADS, EXPERT_TILE // N_KEYS, TOKEN_TILE),
                            lambda n: (0, _pair(n, 1)[1], _pair(n, 1)[0]))
    return pl.pallas_call(
        _peer_dense_body,
        grid=(N_PAIRS + PIPE_LAG,),
        in_specs=[pl.BlockSpec((D_MODEL, TOKEN_TILE), lambda n: (0, _pair(n, 0)[0]), pipeline_mode=once),
                  pl.BlockSpec((EXPERT_TILE, D_MODEL), lambda n: (_pair(n, 0)[1], 0)),
                  pl.BlockSpec((D_MODEL, EXPERT_TILE), lambda n: (0, _pair(n, PIPE_LAG)[1])),
                  row_spec, route_spec, row_spec, route_spec,
                  pl.BlockSpec((PEER_HEADS, 1, TOKEN_TILE), lambda n: (0, 0, _pair(n, 1)[0]))],
        out_specs=pl.BlockSpec((D_MODEL, TOKEN_TILE), lambda n: (0, _pair(n, PIPE_LAG)[0])),
        out_shape=jax.ShapeDtypeStruct((D_MODEL, T_PAD), F32),
        scratch_shapes=[pltpu.VMEM((EXPERT_TILE, TOKEN_TILE), F32), pltpu.VMEM((EXPERT_TILE, TOKEN_TILE), F32),
                        pltpu.VMEM((EXPERT_TILE, TOKEN_TILE), BF16), pltpu.VMEM((EXPERT_TILE, TOKEN_TILE), BF16)],
        compiler_params=_params("arbitrary", vmem=PEER_VMEM_LIMIT),
        name="peer_dense",
    )(hn_t, u, vt_bf, s0, s1, a, b, th)


def _final_body(x_ref, pt_ref, g_ref, o_ref):
    x = x_ref[...] + pt_ref[...].T
    ms = jnp.mean(x * x, axis=-1, keepdims=True)
    o_ref[...] = (x * lax.rsqrt(ms + EPS)) * g_ref[...]


def _final(x2, peer_t, g, first_row, n_rows, tile):
    first_tile = first_row // tile
    return pl.pallas_call(
        _final_body,
        grid=(n_rows // tile,),
        in_specs=[pl.BlockSpec((tile, D_MODEL), lambda i: (first_tile + i, 0)),
                  pl.BlockSpec((D_MODEL, tile), lambda i: (0, first_tile + i)),
                  pl.BlockSpec((1, D_MODEL), lambda i: (0, 0))],
        out_specs=pl.BlockSpec((tile, D_MODEL), lambda i: (i, 0)),
        out_shape=jax.ShapeDtypeStruct((n_rows, D_MODEL), F32),
        compiler_params=_params("parallel"),
        name="final_norm",
    )(x2, peer_t, g.reshape(1, D_MODEL))


def kernel(x_prompt, x_sample, mem_prompt, cache_mem_k, cache_mem_v, state_gla, g_mix, w_in, w_alpha, b_alpha, w_s, b_s, g_sgu, g_gla, w_out, g_mem, w_xk, w_xv, g_xq, w_xq, w_xo, g_ffn, w_pq, sub_keys, u_emb, v_emb, g_final):
    w_in_main = w_in[0, :, :Z_COLS].astype(BF16)
    w_in_gate = jnp.pad(w_in[0, :, Z_COLS:], ((0, 0), (0, LANES - GATE_RANK))).astype(BF16)
    wa_pad = jnp.pad(w_alpha[0], ((0, LANES - GATE_RANK), (0, 0))).astype(BF16)
    ba = b_alpha[0].reshape(1, B_KEY_WIDTH)
    causal = jnp.tril(jnp.ones((SGU_CHUNK, SGU_CHUNK), dtype=bool))
    ws_tril = jnp.where(causal, w_s[0], 0).astype(BF16)
    bs_t = b_s[0].T
    w00 = jnp.repeat(w_s[0, :, 0, 0], LANES).reshape(1, A_WIDTH)
    b00 = jnp.repeat(b_s[0, :, 0], LANES).reshape(1, A_WIDTH)
    gs = g_sgu[0].reshape(1, A_WIDTH)
    gg = g_gla[0].reshape(1, B_VAL_WIDTH)
    keys = sub_keys[0].reshape(2 * PEER_HEADS, N_KEYS, LANES).astype(BF16)
    vt_bf = v_emb[0].T.astype(BF16)

    x_all = jnp.concatenate([x_prompt.reshape(T_PROMPT, D_MODEL), x_sample.reshape(T_SAMPLE, D_MODEL),
                             jnp.zeros((T_PAD - T_ALL, D_MODEL), F32)], axis=0)

    mem = mem_prompt.reshape(BATCH * MEM_LEN, D_MODEL)
    mk, = _nmm(mem, g_mem[0], w_xk[0].astype(BF16), tm=512, tn=D_MODEL)
    mv, = _nmm(mem, g_mem[0], w_xv[0].astype(BF16), tm=512, tn=D_MODEL)

    z, zg = _nmm(x_all, g_mix[0], w_in_main, tm=TOKEN_TILE // 2, tn=Z_COLS, w_aux=w_in_gate)
    mix, cv_p = _sgu_prompt(z, ws_tril, bs_t, gs, jnp.zeros((T_PAD, D_MODEL), F32))
    mix, st_p = _gla_prompt(z, zg, wa_pad, ba, gg, mix)
    mix, cv_s, st_s = _sample_mixers(z, zg, state_gla[0], wa_pad, ba, gs, gg, w00, b00, mix)
    x1 = _mmres(mix, w_out[0].astype(BF16), x_all, tm=TOKEN_TILE)

    qx, = _nmm(x1, g_xq[0], w_xq[0].astype(BF16), tm=TOKEN_TILE, tn=D_MODEL, out_dtype=BF16)
    att = _attn_prompt(qx, mk.reshape(BATCH, MEM_LEN, D_MODEL), mv.reshape(BATCH, MEM_LEN, D_MODEL),
                       jnp.zeros((T_PAD, D_MODEL), BF16))
    att_s = _attn_sample(qx[T_PROMPT:T_ALL].reshape(T_SAMPLE, X_HEADS, X_HEAD_DIM), cache_mem_k, cache_mem_v)
    att = lax.dynamic_update_slice(att, att_s.reshape(T_SAMPLE, D_MODEL), (T_PROMPT, 0))
    x2 = _mmres(att, w_xo[0].astype(BF16), x1, tm=TOKEN_TILE)

    pq, hn = _nmm(x2, g_ffn[0], w_pq[0].astype(BF16), tm=TOKEN_TILE, tn=D_MODEL, emit_xn=True, out_dtype=BF16)
    s0, s1, a, b, th = _peer_route(pq, keys)
    peer_t = _peer_dense(hn.T, u_emb[0], vt_bf, s0, s1, a, b, th)

    y_p = _final(x2, peer_t, g_final, 0, T_PROMPT, 2 * LANES)
    y_s = _final(x2, peer_t, g_final, T_PROMPT, T_SAMPLE, LANES)

    return (y_p.reshape(BATCH, SEQ, D_MODEL),
            y_s.reshape(DEC_BATCH, 1, D_MODEL),
            mk.reshape(1, BATCH, MEM_LEN, X_HEADS, X_HEAD_DIM),
            mv.reshape(1, BATCH, MEM_LEN, X_HEADS, X_HEAD_DIM),
            st_p.reshape(1, BATCH, B_HEADS, B_KEY_DIM, B_VAL_DIM),
            st_s.reshape(1, DEC_BATCH, B_HEADS, B_KEY_DIM, B_VAL_DIM),
            cv_p.reshape(1, BATCH, SGU_CHUNK, A_HEADS, LANES),
            cv_s.reshape(1, DEC_BATCH, 1, A_HEADS, LANES))
```

```python
import functools

import jax
import jax.numpy as jnp
from jax import lax
from jax.experimental import pallas as pl
from jax.experimental.pallas import tpu as pltpu

F32 = jnp.float32
BF16 = jnp.bfloat16

D_MODEL = 2048
BATCH = 4
SEQ = 2048
DEC_BATCH = 128
T_PROMPT = BATCH * SEQ
T_SAMPLE = DEC_BATCH
T_ALL = T_PROMPT + T_SAMPLE

A_HEADS = 8
A_WIDTH = 1024
SGU_CHUNK = 128
B_HEADS = 4
B_KEY_DIM = 128
B_VAL_DIM = 256
B_KEY_WIDTH = 512
B_VAL_WIDTH = 1024
GATE_RANK = 16
GATE_TAU = 16.0
GLA_CHUNK = 64
GLA_SUB = 16
Z_COLS = 2 * A_WIDTH + 2 * B_KEY_WIDTH + 2 * B_VAL_WIDTH
MEM_LEN = 256
X_HEADS = 4
X_HEAD_DIM = 512
N_KEYS = 128
N_EXPERTS = N_KEYS * N_KEYS
PEER_HEADS = 8
PEER_TOPK = 16
EPS = 1e-6

LANES = 128
MXU_DIM = 256
TOKEN_TILE = 3 * MXU_DIM
T_PAD = 11 * TOKEN_TILE
EXPERT_TILE = 1024
VMEM_LIMIT = 56 * 1024 * 1024
PEER_VMEM_LIMIT = 60 * 1024 * 1024


def _params(*sem, vmem=VMEM_LIMIT):
    return pltpu.CompilerParams(dimension_semantics=sem, vmem_limit_bytes=vmem)


_GELU_2C = 2.0 * 0.7978845608028654
_LOG2E = 1.4426950408889634


def _gelu(x):
    p = (x * x) * (-_LOG2E * _GELU_2C * 0.044715) + (-_LOG2E * _GELU_2C)
    return x / (1.0 + jnp.exp2(x * p))


def _log_sigmoid(x):
    return jnp.minimum(x, 0.0) - jnp.log1p(jnp.exp(-jnp.abs(x)))


def _row_to_col(row, eye):
    return jnp.sum(jnp.where(eye, row, 0.0), axis=-1, keepdims=True)


def _eye(n):
    return lax.broadcasted_iota(jnp.int32, (n, n), 0) == lax.broadcasted_iota(jnp.int32, (n, n), 1)


def _nmm_body(*refs, has_aux, emit_xn, heads):
    x_ref, g_ref, w_ref = refs[:3]
    rest = list(refs[3:])
    w2_ref = rest.pop(0) if has_aux else None
    o_ref = rest.pop(0)
    o2_ref = rest.pop(0) if has_aux else None
    xo_ref = rest.pop(0) if emit_xn else None
    oh_ref = rest.pop(0) if heads else None
    xn_ref = rest.pop(0)

    @pl.when(pl.program_id(1) == 0)
    def _():
        x = x_ref[...]
        ms = jnp.mean(x * x, axis=-1, keepdims=True)
        xn = ((x * lax.rsqrt(ms + EPS)) * g_ref[...]).astype(BF16)
        xn_ref[...] = xn
        if has_aux:
            o2_ref[...] = jnp.dot(xn, w2_ref[...], preferred_element_type=F32)
        if emit_xn:
            xo_ref[...] = xn

    res = jnp.dot(xn_ref[...], w_ref[...], preferred_element_type=F32)
    o_ref[...] = res.astype(o_ref.dtype)
    if heads:
        hd = res.shape[1] // heads
        for h in range(heads):
            oh_ref[:, h, :] = res[:, h * hd:(h + 1) * hd]


def _nmm(x, g, w, *, tm, tn, w_aux=None, emit_xn=False, out_dtype=F32, heads=0):
    m, k = x.shape
    n = w.shape[1]
    w_mode = pl.Buffered(1) if tn == n else None
    in_specs = [pl.BlockSpec((tm, k), lambda i, j: (i, 0)),
                pl.BlockSpec((1, k), lambda i, j: (0, 0)),
                pl.BlockSpec((k, tn), lambda i, j: (0, j), pipeline_mode=w_mode)]
    args = [x, g.reshape(1, k), w]
    out_shape = [jax.ShapeDtypeStruct((m, n), out_dtype)]
    out_specs = [pl.BlockSpec((tm, tn), lambda i, j: (i, j))]
    if w_aux is not None:
        in_specs.append(pl.BlockSpec((k, LANES), lambda i, j: (0, 0)))
        args.append(w_aux)
        out_shape.append(jax.ShapeDtypeStruct((m, LANES), F32))
        out_specs.append(pl.BlockSpec((tm, LANES), lambda i, j: (i, 0)))
    if emit_xn:
        out_shape.append(jax.ShapeDtypeStruct((m, k), BF16))
        out_specs.append(pl.BlockSpec((tm, k), lambda i, j: (i, 0)))
    if heads:
        assert tn == n
        out_shape.append(jax.ShapeDtypeStruct((m, heads, n // heads), F32))
        out_specs.append(pl.BlockSpec((tm, heads, n // heads), lambda i, j: (i, 0, 0)))
    return pl.pallas_call(
        functools.partial(_nmm_body, has_aux=w_aux is not None, emit_xn=emit_xn, heads=heads),
        grid=(m // tm, n // tn),
        in_specs=in_specs, out_specs=out_specs, out_shape=out_shape,
        scratch_shapes=[pltpu.VMEM((tm, k), BF16)],
        compiler_params=_params("parallel", "arbitrary"),
        name="nmm",
    )(*args)


def _mmres_body(x_ref, w_ref, r_ref, o_ref):
    o_ref[...] = r_ref[...] + jnp.dot(x_ref[...].astype(BF16), w_ref[...], preferred_element_type=F32)


def _mmres(x, w, res, *, tm):
    m, k = x.shape
    n = w.shape[1]
    return pl.pallas_call(
        _mmres_body,
        grid=(m // tm,),
        in_specs=[pl.BlockSpec((tm, k), lambda i: (i, 0)),
                  pl.BlockSpec((k, n), lambda i: (0, 0), pipeline_mode=pl.Buffered(1)),
                  pl.BlockSpec((tm, n), lambda i: (i, 0))],
        out_specs=pl.BlockSpec((tm, n), lambda i: (i, 0)),
        out_shape=jax.ShapeDtypeStruct((m, n), F32),
        compiler_params=_params("parallel"),
        name="mmres",
    )(x, w, res)


SGU_STEP_CHUNKS = 2


def _sgu_body(u_ref, v_ref, ws_ref, bs_ref, gs_ref, mix_ref, o_ref, cv_ref):
    del mix_ref
    for c in range(SGU_STEP_CHUNKS):
        rows = slice(c * SGU_CHUNK, (c + 1) * SGU_CHUNK)
        for h in range(A_HEADS):
            sl = slice(h * LANES, (h + 1) * LANES)
            u = _gelu(u_ref[rows, sl])
            v = _gelu(v_ref[rows, sl])
            ms = jnp.mean(v * v, axis=-1, keepdims=True)
            va = (v * lax.rsqrt(ms + EPS)) * gs_ref[:, sl]
            mixed = jnp.dot(ws_ref[h], va.astype(BF16), preferred_element_type=F32) + bs_ref[:, h:h + 1]
            o_ref[rows, sl] = u * mixed
            if c == SGU_STEP_CHUNKS - 1:
                cv_ref[0, :, sl] = va


def _sgu_prompt(z, ws_tril, bs_t, gs, mix):
    step_rows = SGU_STEP_CHUNKS * SGU_CHUNK
    n_steps = SEQ // step_rows
    return pl.pallas_call(
        _sgu_body,
        grid=(T_PROMPT // step_rows,),
        in_specs=[pl.BlockSpec((step_rows, A_WIDTH), lambda i: (i, 0)),
                  pl.BlockSpec((step_rows, A_WIDTH), lambda i: (i, 1)),
                  pl.BlockSpec((A_HEADS, SGU_CHUNK, SGU_CHUNK), lambda i: (0, 0, 0)),
                  pl.BlockSpec((SGU_CHUNK, A_HEADS), lambda i: (0, 0)),
                  pl.BlockSpec((1, A_WIDTH), lambda i: (0, 0)),
                  pl.BlockSpec(memory_space=pl.ANY)],
        out_specs=[pl.BlockSpec((step_rows, A_WIDTH), lambda i: (i, 0)),
                   pl.BlockSpec((1, SGU_CHUNK, A_WIDTH), lambda i: (i // n_steps, 0, 0))],
        out_shape=[jax.ShapeDtypeStruct((T_PAD, D_MODEL), F32),
                   jax.ShapeDtypeStruct((BATCH, SGU_CHUNK, A_WIDTH), F32)],
        input_output_aliases={5: 0},
        compiler_params=_params("arbitrary"),
        name="sgu_prompt",
    )(z, z, ws_tril, bs_t, gs, mix)


GLA_STEP_CHUNKS = 2


def _gla_body(q_ref, k_ref, v_ref, r_ref, zg_ref, wa_ref, ba_ref, gg_ref, mix_ref, o_ref, st_ref, s_ref, g_ref):
    del mix_ref
    c = pl.program_id(1)

    @pl.when(c == 0)
    def _():
        s_ref[...] = jnp.zeros_like(s_ref)

    for cc in range(GLA_STEP_CHUNKS):
        row0 = cc * GLA_CHUNK
        rows = slice(row0, row0 + GLA_CHUNK)
        x = jnp.dot(zg_ref[rows, :].astype(BF16), wa_ref[...], preferred_element_type=F32) + ba_ref[...]
        lg = _log_sigmoid(x) * (1.0 / GATE_TAU)
        tri = (lax.broadcasted_iota(jnp.int32, (GLA_CHUNK, GLA_CHUNK), 1)
               <= lax.broadcasted_iota(jnp.int32, (GLA_CHUNK, GLA_CHUNK), 0)).astype(BF16)
        hi = lg.astype(BF16)
        r1 = lg - hi.astype(F32)
        mid = r1.astype(BF16)
        lo = (r1 - mid.astype(F32)).astype(BF16)
        g_ref[cc] = (jnp.dot(tri, hi, preferred_element_type=F32) + jnp.dot(tri, mid, preferred_element_type=F32)
                     + jnp.dot(tri, lo, preferred_element_type=F32))
        eye = _eye(B_KEY_DIM)
        half = GLA_SUB // 2
        t_iota = lax.broadcasted_iota(jnp.int32, (half, B_KEY_DIM), 0)
        scale = B_KEY_DIM ** -0.5

        for h in range(B_HEADS):
            dk = slice(h * B_KEY_DIM, (h + 1) * B_KEY_DIM)
            dv = slice(h * B_VAL_DIM, (h + 1) * B_VAL_DIM)
            k = k_ref[rows, dk]
            g = g_ref[cc, :, dk]
            vb = v_ref[rows, dv].astype(BF16)
            state = s_ref[h]
            o_inter = jnp.dot((q_ref[rows, dk] * scale * jnp.exp(g)).astype(BF16), state.astype(BF16),
                              preferred_element_type=F32)
            parts = []
            for i in range(GLA_CHUNK // GLA_SUB):
                lo_r, mid_r, hi_r = GLA_SUB * i, GLA_SUB * i + half, GLA_SUB * (i + 1)
                q_lo = q_ref[row0 + lo_r:row0 + mid_r, dk] * scale
                q_hi = q_ref[row0 + mid_r:row0 + hi_r, dk] * scale
                g_lo, g_hi = g_ref[cc, lo_r:mid_r, dk], g_ref[cc, mid_r:hi_r, dk]
                acc_lo = jnp.zeros((half, B_VAL_DIM), F32)
                acc_hi = jnp.zeros((half, B_VAL_DIM), F32)
                for s in range(GLA_SUB):
                    r = lo_r + s
                    k_row, v_row = k_ref[row0 + r:row0 + r + 1, dk], v_ref[row0 + r:row0 + r + 1, dv]
                    g_row = g_ref[cc, r:r + 1, dk]
                    if s < half:
                        dec = jnp.exp(jnp.where(t_iota >= s, g_lo - g_row, -jnp.inf))
                        col = jnp.sum(q_lo * k_row * dec, axis=-1, keepdims=True)
                        acc_lo = acc_lo + col * v_row
                        dec = jnp.exp(g_hi - g_row)
                    else:
                        dec = jnp.exp(jnp.where(t_iota >= s - half, g_hi - g_row, -jnp.inf))
                    col = jnp.sum(q_hi * k_row * dec, axis=-1, keepdims=True)
                    acc_hi = acc_hi + col * v_row
                acc = jnp.concatenate([acc_lo, acc_hi], axis=0)
                if i > 0:
                    gref = g_ref[cc, lo_r - 1:lo_r, dk]
                    qd = (jnp.concatenate([q_lo, q_hi], axis=0) * jnp.exp(g[lo_r:hi_r] - gref)).astype(BF16)
                    kd = (k[:lo_r] * jnp.exp(gref - g[:lo_r])).astype(BF16)
                    sc = lax.dot_general(qd, kd, (((1,), (1,)), ((), ())), preferred_element_type=F32)
                    acc = acc + jnp.dot(sc.astype(BF16), vb[:lo_r], preferred_element_type=F32)
                parts.append(acc)
            o = o_inter + jnp.concatenate(parts, axis=0)

            g_last = g_ref[cc, GLA_CHUNK - 1:GLA_CHUNK, dk]
            kdec = k * jnp.exp(g_last - g)
            upd = jnp.dot(kdec.T.astype(BF16), vb, preferred_element_type=F32)
            s_ref[h] = _row_to_col(jnp.exp(g_last), eye) * state + upd

            ms = jnp.mean(o * o, axis=-1, keepdims=True)
            on = (o * lax.rsqrt(ms + EPS)) * gg_ref[:, dv]
            r = r_ref[rows, dv]
            o_ref[rows, dv] = (r * jax.nn.sigmoid(r)) * on

    @pl.when(c == pl.num_programs(1) - 1)
    def _():
        st_ref[0] = s_ref[...]


def _gla_prompt(z, zg, wa_pad, ba, gg, mix):
    step_rows = GLA_STEP_CHUNKS * GLA_CHUNK
    n_steps = SEQ // step_rows
    row = lambda b, c: b * n_steps + c
    return pl.pallas_call(
        _gla_body,
        grid=(BATCH, n_steps),
        in_specs=[pl.BlockSpec((step_rows, B_KEY_WIDTH), lambda b, c: (row(b, c), 4)),
                  pl.BlockSpec((step_rows, B_KEY_WIDTH), lambda b, c: (row(b, c), 5)),
                  pl.BlockSpec((step_rows, B_VAL_WIDTH), lambda b, c: (row(b, c), 3)),
                  pl.BlockSpec((step_rows, B_VAL_WIDTH), lambda b, c: (row(b, c), 4)),
                  pl.BlockSpec((step_rows, LANES), lambda b, c: (row(b, c), 0)),
                  pl.BlockSpec((LANES, B_KEY_WIDTH), lambda b, c: (0, 0)),
                  pl.BlockSpec((1, B_KEY_WIDTH), lambda b, c: (0, 0)),
                  pl.BlockSpec((1, B_VAL_WIDTH), lambda b, c: (0, 0)),
                  pl.BlockSpec(memory_space=pl.ANY)],
        out_specs=[pl.BlockSpec((step_rows, B_VAL_WIDTH), lambda b, c: (row(b, c), 1)),
                   pl.BlockSpec((1, B_HEADS, B_KEY_DIM, B_VAL_DIM), lambda b, c: (b, 0, 0, 0))],
        out_shape=[jax.ShapeDtypeStruct((T_PAD, D_MODEL), F32),
                   jax.ShapeDtypeStruct((BATCH, B_HEADS, B_KEY_DIM, B_VAL_DIM), F32)],
        scratch_shapes=[pltpu.VMEM((B_HEADS, B_KEY_DIM, B_VAL_DIM), F32),
                        pltpu.VMEM((GLA_STEP_CHUNKS, GLA_CHUNK, B_KEY_WIDTH), F32)],
        input_output_aliases={8: 0},
        compiler_params=_params("arbitrary", "arbitrary"),
        name="gla_prompt",
    )(z, z, z, z, zg, wa_pad, ba, gg, mix)


SAMPLE_ROWS = 8


def _smix_body(z_ref, zg_ref, st_ref, wa_ref, ba_ref, gs_ref, gg_ref, w00_ref, b00_ref, mix_ref,
               o_ref, cv_ref, so_ref):
    del mix_ref
    u = _gelu(z_ref[:, 0:A_WIDTH])
    v = _gelu(z_ref[:, A_WIDTH:2 * A_WIDTH])
    for h in range(A_HEADS):
        sl = slice(h * LANES, (h + 1) * LANES)
        vh = v[:, sl]
        ms = jnp.mean(vh * vh, axis=-1, keepdims=True)
        va = (vh * lax.rsqrt(ms + EPS)) * gs_ref[:, sl]
        cv_ref[:, sl] = va
        o_ref[:, sl] = u[:, sl] * (w00_ref[:, sl] * va + b00_ref[:, sl])

    x = jnp.dot(zg_ref[...].astype(BF16), wa_ref[...], preferred_element_type=F32) + ba_ref[...]
    a = jnp.exp(_log_sigmoid(x) * (1.0 / GATE_TAU))
    q0, k0, v0, r0 = 2 * A_WIDTH, 2 * A_WIDTH + B_KEY_WIDTH, 2 * A_WIDTH + 2 * B_KEY_WIDTH, Z_COLS - B_VAL_WIDTH
    eye = _eye(B_KEY_DIM)
    for b in range(SAMPLE_ROWS):
        for h in range(B_HEADS):
            dk = slice(h * B_KEY_DIM, (h + 1) * B_KEY_DIM)
            a_col = _row_to_col(a[b:b + 1, dk], eye)
            k_col = _row_to_col(z_ref[b:b + 1, k0 + h * B_KEY_DIM:k0 + (h + 1) * B_KEY_DIM], eye)
            q_col = _row_to_col(z_ref[b:b + 1, q0 + h * B_KEY_DIM:q0 + (h + 1) * B_KEY_DIM], eye) * (B_KEY_DIM ** -0.5)
            vrow = z_ref[b:b + 1, v0 + h * B_VAL_DIM:v0 + (h + 1) * B_VAL_DIM]
            s_new = a_col * st_ref[b, h] + k_col * vrow
            so_ref[b, h] = s_new
            o = jnp.sum(q_col * s_new, axis=0, keepdims=True)
            ms = jnp.mean(o * o, axis=-1, keepdims=True)
            on = (o * lax.rsqrt(ms + EPS)) * gg_ref[:, h * B_VAL_DIM:(h + 1) * B_VAL_DIM]
            r = z_ref[b:b + 1, r0 + h * B_VAL_DIM:r0 + (h + 1) * B_VAL_DIM]
            o_ref[b:b + 1, A_WIDTH + h * B_VAL_DIM:A_WIDTH + (h + 1) * B_VAL_DIM] = (r * jax.nn.sigmoid(r)) * on


def _sample_mixers(z, zg, state, wa_pad, ba, gs, gg, w00, b00, mix):
    first = T_PROMPT // SAMPLE_ROWS
    return pl.pallas_call(
        _smix_body,
        grid=(T_SAMPLE // SAMPLE_ROWS,),
        in_specs=[pl.BlockSpec((SAMPLE_ROWS, Z_COLS), lambda i: (first + i, 0)),
                  pl.BlockSpec((SAMPLE_ROWS, LANES), lambda i: (first + i, 0)),
                  pl.BlockSpec((SAMPLE_ROWS, B_HEADS, B_KEY_DIM, B_VAL_DIM), lambda i: (i, 0, 0, 0)),
                  pl.BlockSpec((LANES, B_KEY_WIDTH), lambda i: (0, 0)),
                  pl.BlockSpec((1, B_KEY_WIDTH), lambda i: (0, 0)),
                  pl.BlockSpec((1, A_WIDTH), lambda i: (0, 0)),
                  pl.BlockSpec((1, B_VAL_WIDTH), lambda i: (0, 0)),
                  pl.BlockSpec((1, A_WIDTH), lambda i: (0, 0)),
                  pl.BlockSpec((1, A_WIDTH), lambda i: (0, 0)),
                  pl.BlockSpec(memory_space=pl.ANY)],
        out_specs=[pl.BlockSpec((SAMPLE_ROWS, D_MODEL), lambda i: (first + i, 0)),
                   pl.BlockSpec((SAMPLE_ROWS, A_WIDTH), lambda i: (i, 0)),
                   pl.BlockSpec((SAMPLE_ROWS, B_HEADS, B_KEY_DIM, B_VAL_DIM), lambda i: (i, 0, 0, 0))],
        out_shape=[jax.ShapeDtypeStruct((T_PAD, D_MODEL), F32),
                   jax.ShapeDtypeStruct((T_SAMPLE, A_WIDTH), F32),
                   jax.ShapeDtypeStruct((T_SAMPLE, B_HEADS, B_KEY_DIM, B_VAL_DIM), F32)],
        input_output_aliases={9: 0},
        compiler_params=_params("parallel"),
        name="sample_mixers",
    )(z, zg, state, wa_pad, ba, gs, gg, w00, b00, mix)


ATT_TILE = 1024


def _attn_prompt_body(q_ref, k_ref, v_ref, att_ref, o_ref):
    del att_ref
    for h in range(X_HEADS):
        sl = slice(h * X_HEAD_DIM, (h + 1) * X_HEAD_DIM)
        q = q_ref[:, sl]
        k = k_ref[0, :, sl].astype(BF16)
        s = lax.dot_general(q, k, (((1,), (1,)), ((), ())), preferred_element_type=F32) * (X_HEAD_DIM ** -0.5)
        s = s - jnp.max(s, axis=-1, keepdims=True)
        e = jnp.exp(s)
        p = e / jnp.sum(e, axis=-1, keepdims=True)
        o_ref[:, sl] = jnp.dot(p.astype(BF16), v_ref[0, :, sl].astype(BF16), preferred_element_type=F32).astype(BF16)


def _attn_prompt(qx, mk, mv, att):
    tiles = SEQ // ATT_TILE
    return pl.pallas_call(
        _attn_prompt_body,
        grid=(BATCH, tiles),
        in_specs=[pl.BlockSpec((ATT_TILE, D_MODEL), lambda b, t: (b * tiles + t, 0)),
                  pl.BlockSpec((1, MEM_LEN, D_MODEL), lambda b, t: (b, 0, 0)),
                  pl.BlockSpec((1, MEM_LEN, D_MODEL), lambda b, t: (b, 0, 0)),
                  pl.BlockSpec(memory_space=pl.ANY)],
        out_specs=pl.BlockSpec((ATT_TILE, D_MODEL), lambda b, t: (b * tiles + t, 0)),
        out_shape=jax.ShapeDtypeStruct((T_PAD, D_MODEL), BF16),
        input_output_aliases={3: 0},
        compiler_params=_params("parallel", "parallel"),
        name="attn_prompt",
    )(qx, mk, mv, att)


ATT_SEQS = 2


def _attn_sample_body(q_ref, k_ref, v_ref, o_ref):
    for b in range(ATT_SEQS):
        q = q_ref[b:b + 1].astype(F32)
        s = jnp.sum(k_ref[0, b] * q, axis=-1, keepdims=True) * (X_HEAD_DIM ** -0.5)
        s = s - jnp.max(s, axis=0, keepdims=True)
        e = jnp.exp(s)
        p = e / jnp.sum(e, axis=0, keepdims=True)
        o_ref[b] = jnp.sum(p * v_ref[0, b], axis=0).astype(BF16)


def _attn_sample(q, ck, cv):
    kv_spec = pl.BlockSpec((1, ATT_SEQS, MEM_LEN, X_HEADS, X_HEAD_DIM), lambda i: (0, i, 0, 0, 0))
    return pl.pallas_call(
        _attn_sample_body,
        grid=(T_SAMPLE // ATT_SEQS,),
        in_specs=[pl.BlockSpec((ATT_SEQS, X_HEADS, X_HEAD_DIM), lambda i: (i, 0, 0)), kv_spec, kv_spec],
        out_specs=pl.BlockSpec((ATT_SEQS, X_HEADS, X_HEAD_DIM), lambda i: (i, 0, 0)),
        out_shape=jax.ShapeDtypeStruct((T_SAMPLE, X_HEADS, X_HEAD_DIM), BF16),
        compiler_params=_params("parallel"),
        name="attn_sample",
    )(q, ck, cv)


def _odd_even_merge_sort_pairs(n):
    pairs = []
    p = 1
    while p < n:
        k = p
        while k >= 1:
            for j in range(k % p, n - k, 2 * k):
                for i in range(min(k, n - j - k)):
                    if (i + j) // (2 * p) == (i + j + k) // (2 * p):
                        pairs.append((i + j, i + j + k))
            k //= 2
        p *= 2
    return pairs


_SORT16 = tuple(_odd_even_merge_sort_pairs(PEER_TOPK))


def _exchange(vs, i, j):
    vs[i], vs[j] = jnp.maximum(vs[i], vs[j]), jnp.minimum(vs[i], vs[j])


def _top16_sorted(s):
    vs = [s[8 * k:8 * (k + 1), :] for k in range(PEER_TOPK)]
    for i, j in _SORT16:
        _exchange(vs, i, j)
    for shift in (4, 2, 1):
        vs = _merge_top16(vs, [pltpu.roll(v, shift, 0) for v in vs])
    return vs


def _merge_top16(a, b):
    c = list(a)
    for k, bk in enumerate(b):
        c[PEER_TOPK - 1 - k] = jnp.maximum(a[PEER_TOPK - 1 - k], bk)
    for stride in (8, 4, 2, 1):
        for k in range(PEER_TOPK):
            if k & stride == 0:
                _exchange(c, k, k + stride)
    return c


def _stack_rows(vs, rows8):
    out = vs[0]
    for k in range(1, 8):
        out = jnp.where(rows8 == k, vs[k], out)
    return out


_CAND_ROWS = tuple((a, PEER_TOPK // (a + 1)) for a in range(1, 8))


def _peer_route_body(pq_ref, keys_ref, s0_ref, s1_ref, a_ref, b_ref, th_ref):
    rows8 = lax.broadcasted_iota(jnp.int32, (8, LANES), 0)

    def head(h, carry):
        q0 = pq_ref[:, pl.ds(pl.multiple_of(h * 2 * LANES, LANES), LANES)]
        q1 = pq_ref[:, pl.ds(pl.multiple_of(h * 2 * LANES + LANES, LANES), LANES)]
        nt = (((1,), (1,)), ((), ()))
        s0 = lax.dot_general(keys_ref[2 * h], q0, nt, preferred_element_type=F32)
        s1 = lax.dot_general(keys_ref[2 * h + 1], q1, nt, preferred_element_type=F32)
        r0 = _top16_sorted(s0)
        r1 = _top16_sorted(s1)
        best = [r0[0] + r1[b] for b in range(PEER_TOPK)]
        for a, nb in _CAND_ROWS:
            best = _merge_top16(best, [r0[a] + r1[b] for b in range(nb)])
        best = _merge_top16(best, [r0[a] + r1[0] for a in range(8, PEER_TOPK)])
        theta = best[PEER_TOPK - 1][0:1]
        top = best[0][0:1]
        v1_lo = _stack_rows(r1[0:8], rows8)
        cs = [r0[0] + v1_lo, r0[0] + _stack_rows(r1[8:16], rows8)]
        for a, nb in _CAND_ROWS:
            cs.append(jnp.where(rows8 < nb, r0[a] + v1_lo, -jnp.inf))
        cs.append(_stack_rows(r0[8:16], rows8) + r1[0])
        cand = jnp.concatenate(cs, axis=0)
        zsum = jnp.sum(jnp.where(cand >= theta, jnp.exp(cand - top), 0.0), axis=0, keepdims=True)
        s0_ref[h] = s0
        s1_ref[h] = s1
        a_ref[h] = jnp.exp(s0 - r0[0][0:1]) / zsum
        b_ref[h] = jnp.exp(s1 - r1[0][0:1])
        th_ref[h] = theta
        return carry

    lax.fori_loop(0, PEER_HEADS, head, 0)


def _peer_route(pq, keys):
    big = jax.ShapeDtypeStruct((PEER_HEADS, N_KEYS, T_PAD), F32)
    big_spec = pl.BlockSpec((PEER_HEADS, N_KEYS, LANES), lambda i: (0, 0, i))
    return pl.pallas_call(
        _peer_route_body,
        grid=(T_PAD // LANES,),
        in_specs=[pl.BlockSpec((LANES, D_MODEL), lambda i: (i, 0)),
                  pl.BlockSpec((2 * PEER_HEADS, N_KEYS, LANES), lambda i: (0, 0, 0))],
        out_specs=[big_spec, big_spec, big_spec, big_spec,
                   pl.BlockSpec((PEER_HEADS, 1, LANES), lambda i: (0, 0, i))],
        out_shape=[big, big, big, big, jax.ShapeDtypeStruct((PEER_HEADS, 1, T_PAD), F32)],
        compiler_params=_params("parallel"),
        name="peer_route",
    )(pq, keys)


N_EXPERT_BLOCKS = N_EXPERTS // EXPERT_TILE
N_TOKEN_BLOCKS = T_PAD // TOKEN_TILE
N_PAIRS = N_TOKEN_BLOCKS * N_EXPERT_BLOCKS
PIPE_LAG = 2
GATE_ROWS = 32
N_GROUPS = EXPERT_TILE // N_KEYS


def _pair(n, lag):
    c = jnp.clip(n - lag, 0, N_PAIRS - 1)
    return c // N_EXPERT_BLOCKS, c % N_EXPERT_BLOCKS


def _peer_dense_body(h_ref, u_ref, vt_ref, s0_ref, s1_ref, a_ref, b_ref, th_ref, o_ref, s_even, s_odd, a_even, a_odd):
    n = pl.program_id(0)

    @pl.when(n == 0)
    def _():
        for ref in (s_even, s_odd, a_even, a_odd):
            ref[...] = jnp.zeros_like(ref)

    @pl.when(jnp.logical_or(n == 0, jnp.maximum(n - PIPE_LAG, 0) % N_EXPERT_BLOCKS == 0))
    def _():
        o_ref[...] = jnp.zeros_like(o_ref)

    def run(s_w, s_r, a_w, a_r):
        def cols(nn, carry):
            t0 = pl.multiple_of(nn * MXU_DIM, MXU_DIM)
            tok = pl.ds(t0, MXU_DIM)
            o_ref[:, tok] += jnp.dot(vt_ref[...], a_r[:, tok], preferred_element_type=F32)
            s_w[:, tok] = jnp.dot(u_ref[...], h_ref[:, tok].astype(F32), preferred_element_type=F32)
            for g in range(N_GROUPS):
                for lt in range(MXU_DIM // LANES):
                    lanes = pl.ds(pl.multiple_of(t0 + lt * LANES, LANES), LANES)
                    for j0 in range(0, N_KEYS, GATE_ROWS):
                        keys = slice(j0, j0 + GATE_ROWS)
                        rows = slice(g * N_KEYS + j0, g * N_KEYS + j0 + GATE_ROWS)
                        gate = jnp.zeros((GATE_ROWS, LANES), F32)
                        for h in range(PEER_HEADS):
                            c = s0_ref[h, g:g + 1, lanes] + s1_ref[h, keys, lanes]
                            w = a_ref[h, g:g + 1, lanes] * b_ref[h, keys, lanes]
                            gate = gate + jnp.where(c >= th_ref[h, :, lanes], w, 0.0)
                        a_w[rows, lanes] = (_gelu(s_r[rows, lanes]) * gate).astype(BF16)
            return carry

        lax.fori_loop(0, TOKEN_TILE // MXU_DIM, cols, 0)

    @pl.when(n % 2 == 0)
    def _():
        run(s_even, s_odd, a_odd, a_even)

    @pl.when(n % 2 == 1)
    def _():
        run(s_odd, s_even, a_even, a_odd)


def _peer_dense(hn_t, u, vt_bf, s0, s1, a, b, th):
    once = pl.Buffered(1)
    route_spec = pl.BlockSpec((PEER_HEADS, N_KEYS, TOKEN_TILE), lambda n: (0, 0, _pair(n, 1)[0]), pipeline_mode=once)
    row_spec = pl.BlockSpec((PEER_HEADS, EXPERT_TILE // N_KEYS, TOKEN_TILE),
                            lambda n: (0, _pair(n, 1)[1], _pair(n, 1)[0]))
    return pl.pallas_call(
        _peer_dense_body,
        grid=(N_PAIRS + PIPE_LAG,),
        in_specs=[pl.BlockSpec((D_MODEL, TOKEN_TILE), lambda n: (0, _pair(n, 0)[0]), pipeline_mode=once),
                  pl.BlockSpec((EXPERT_TILE, D_MODEL), lambda n: (_pair(n, 0)[1], 0)),
                  pl.BlockSpec((D_MODEL, EXPERT_TILE), lambda n: (0, _pair(n, PIPE_LAG)[1])),
                  row_spec, route_spec, row_spec, route_spec,
                  pl.BlockSpec((PEER_HEADS, 1, TOKEN_TILE), lambda n: (0, 0, _pair(n, 1)[0]))],
        out_specs=pl.BlockSpec((D_MODEL, TOKEN_TILE), lambda n: (0, _pair(n, PIPE_LAG)[0])),
        out_shape=jax.ShapeDtypeStruct((D_MODEL, T_PAD), F32),
        scratch_shapes=[pltpu.VMEM((EXPERT_TILE, TOKEN_TILE), F32), pltpu.VMEM((EXPERT_TILE, TOKEN_TILE), F32),
                        pltpu.VMEM((EXPERT_TILE, TOKEN_TILE), BF16), pltpu.VMEM((EXPERT_TILE, TOKEN_TILE), BF16)],
        compiler_params=_params("arbitrary", vmem=PEER_VMEM_LIMIT),
        name="peer_dense",
    )(hn_t, u, vt_bf, s0, s1, a, b, th)


def _final_body(x_ref, pt_ref, g_ref, o_ref):
    x = x_ref[...] + pt_ref[...].T
    ms = jnp.mean(x * x, axis=-1, keepdims=True)
    o_ref[...] = (x * lax.rsqrt(ms + EPS)) * g_ref[...]


def _final(x2, peer_t, g, first_row, n_rows, tile):
    first_tile = first_row // tile
    return pl.pallas_call(
        _final_body,
        grid=(n_rows // tile,),
        in_specs=[pl.BlockSpec((tile, D_MODEL), lambda i: (first_tile + i, 0)),
                  pl.BlockSpec((D_MODEL, tile), lambda i: (0, first_tile + i)),
                  pl.BlockSpec((1, D_MODEL), lambda i: (0, 0))],
        out_specs=pl.BlockSpec((tile, D_MODEL), lambda i: (i, 0)),
        out_shape=jax.ShapeDtypeStruct((n_rows, D_MODEL), F32),
        compiler_params=_params("parallel"),
        name="final_norm",
    )(x2, peer_t, g.reshape(1, D_MODEL))


def kernel(x_prompt, x_sample, mem_prompt, cache_mem_k, cache_mem_v, state_gla, g_mix, w_in, w_alpha, b_alpha, w_s, b_s, g_sgu, g_gla, w_out, g_mem, w_xk, w_xv, g_xq, w_xq, w_xo, g_ffn, w_pq, sub_keys, u_emb, v_emb, g_final):
    w_in_main = w_in[0, :, :Z_COLS].astype(BF16)
    w_in_gate = jnp.pad(w_in[0, :, Z_COLS:], ((0, 0), (0, LANES - GATE_RANK))).astype(BF16)
    wa_pad = jnp.pad(w_alpha[0], ((0, LANES - GATE_RANK), (0, 0))).astype(BF16)
    ba = b_alpha[0].reshape(1, B_KEY_WIDTH)
    causal = jnp.tril(jnp.ones((SGU_CHUNK, SGU_CHUNK), dtype=bool))
    ws_tril = jnp.where(causal, w_s[0], 0).astype(BF16)
    bs_t = b_s[0].T
    w00 = jnp.repeat(w_s[0, :, 0, 0], LANES).reshape(1, A_WIDTH)
    b00 = jnp.repeat(b_s[0, :, 0], LANES).reshape(1, A_WIDTH)
    gs = g_sgu[0].reshape(1, A_WIDTH)
    gg = g_gla[0].reshape(1, B_VAL_WIDTH)
    keys = sub_keys[0].reshape(2 * PEER_HEADS, N_KEYS, LANES).astype(BF16)
    vt_bf = v_emb[0].T.astype(BF16)

    x_all = jnp.concatenate([x_prompt.reshape(T_PROMPT, D_MODEL), x_sample.reshape(T_SAMPLE, D_MODEL),
                             jnp.zeros((T_PAD - T_ALL, D_MODEL), F32)], axis=0)

    mem = mem_prompt.reshape(BATCH * MEM_LEN, D_MODEL)
    mk, mk_heads = _nmm(mem, g_mem[0], w_xk[0].astype(BF16), tm=512, tn=D_MODEL, heads=X_HEADS)
    mv, mv_heads = _nmm(mem, g_mem[0], w_xv[0].astype(BF16), tm=512, tn=D_MODEL, heads=X_HEADS)

    z, zg = _nmm(x_all, g_mix[0], w_in_main, tm=TOKEN_TILE // 2, tn=Z_COLS, w_aux=w_in_gate)
    mix, cv_p = _sgu_prompt(z, ws_tril, bs_t, gs, jnp.zeros((T_PAD, D_MODEL), F32))
    mix, st_p = _gla_prompt(z, zg, wa_pad, ba, gg, mix)
    mix, cv_s, st_s = _sample_mixers(z, zg, state_gla[0], wa_pad, ba, gs, gg, w00, b00, mix)
    x1 = _mmres(mix, w_out[0].astype(BF16), x_all, tm=TOKEN_TILE)

    qx, = _nmm(x1, g_xq[0], w_xq[0].astype(BF16), tm=TOKEN_TILE, tn=D_MODEL, out_dtype=BF16)
    att = _attn_prompt(qx, mk.reshape(BATCH, MEM_LEN, D_MODEL), mv.reshape(BATCH, MEM_LEN, D_MODEL),
                       jnp.zeros((T_PAD, D_MODEL), BF16))
    att_s = _attn_sample(qx[T_PROMPT:T_ALL].reshape(T_SAMPLE, X_HEADS, X_HEAD_DIM), cache_mem_k, cache_mem_v)
    att = lax.dynamic_update_slice(att, att_s.reshape(T_SAMPLE, D_MODEL), (T_PROMPT, 0))
    x2 = _mmres(att, w_xo[0].astype(BF16), x1, tm=TOKEN_TILE)

    pq, hn = _nmm(x2, g_ffn[0], w_pq[0].astype(BF16), tm=TOKEN_TILE, tn=D_MODEL, emit_xn=True, out_dtype=BF16)
    s0, s1, a, b, th = _peer_route(pq, keys)
    peer_t = _peer_dense(hn.T, u_emb[0], vt_bf, s0, s1, a, b, th)

    y_p = _final(x2, peer_t, g_final, 0, T_PROMPT, 2 * LANES)
    y_s = _final(x2, peer_t, g_final, T_PROMPT, T_SAMPLE, LANES)

    return (y_p.reshape(BATCH, SEQ, D_MODEL),
            y_s.reshape(DEC_BATCH, 1, D_MODEL),
            mk_heads.reshape(1, BATCH, MEM_LEN, X_HEADS, X_HEAD_DIM),
            mv_heads.reshape(1, BATCH, MEM_LEN, X_HEADS, X_HEAD_DIM),
            st_p.reshape(1, BATCH, B_HEADS, B_KEY_DIM, B_VAL_DIM),
            st_s.reshape(1, DEC_BATCH, B_HEADS, B_KEY_DIM, B_VAL_DIM),
            cv_p.reshape(1, BATCH, SGU_CHUNK, A_HEADS, LANES),
            cv_s.reshape(1, DEC_BATCH, 1, A_HEADS, LANES))
```

```python
import functools

import jax
import jax.numpy as jnp
from jax import lax
from jax.experimental import pallas as pl
from jax.experimental.pallas import tpu as pltpu

F32 = jnp.float32
BF16 = jnp.bfloat16

D_MODEL = 2048
BATCH = 4
SEQ = 2048
DEC_BATCH = 128
T_PROMPT = BATCH * SEQ
T_SAMPLE = DEC_BATCH
T_ALL = T_PROMPT + T_SAMPLE

A_HEADS = 8
A_WIDTH = 1024
SGU_CHUNK = 128
B_HEADS = 4
B_KEY_DIM = 128
B_VAL_DIM = 256
B_KEY_WIDTH = 512
B_VAL_WIDTH = 1024
GATE_RANK = 16
GATE_TAU = 16.0
GLA_CHUNK = 64
GLA_SUB = 16
Z_COLS = 2 * A_WIDTH + 2 * B_KEY_WIDTH + 2 * B_VAL_WIDTH
MEM_LEN = 256
X_HEADS = 4
X_HEAD_DIM = 512
N_KEYS = 128
N_EXPERTS = N_KEYS * N_KEYS
PEER_HEADS = 8
PEER_TOPK = 16
EPS = 1e-6

LANES = 128
MXU_DIM = 256
TOKEN_TILE = 3 * MXU_DIM
T_PAD = 11 * TOKEN_TILE
EXPERT_TILE = 1024
VMEM_LIMIT = 56 * 1024 * 1024
PEER_VMEM_LIMIT = 60 * 1024 * 1024


def _params(*sem, vmem=VMEM_LIMIT):
    return pltpu.CompilerParams(dimension_semantics=sem, vmem_limit_bytes=vmem)


_GELU_2C = 2.0 * 0.7978845608028654
_LOG2E = 1.4426950408889634


def _gelu(x):
    p = (x * x) * (-_LOG2E * _GELU_2C * 0.044715) + (-_LOG2E * _GELU_2C)
    return x / (1.0 + jnp.exp2(x * p))


def _log_sigmoid(x):
    return jnp.minimum(x, 0.0) - jnp.log1p(jnp.exp(-jnp.abs(x)))


def _row_to_col(row, eye):
    return jnp.sum(jnp.where(eye, row, 0.0), axis=-1, keepdims=True)


def _eye(n):
    return lax.broadcasted_iota(jnp.int32, (n, n), 0) == lax.broadcasted_iota(jnp.int32, (n, n), 1)


def _nmm_body(*refs, has_aux, emit_xn, heads):
    x_ref, g_ref, w_ref = refs[:3]
    rest = list(refs[3:])
    w2_ref = rest.pop(0) if has_aux else None
    o_ref = rest.pop(0)
    o2_ref = rest.pop(0) if has_aux else None
    xo_ref = rest.pop(0) if emit_xn else None
    oh_ref = rest.pop(0) if heads else None
    xn_ref = rest.pop(0)

    @pl.when(pl.program_id(1) == 0)
    def _():
        x = x_ref[...]
        ms = jnp.mean(x * x, axis=-1, keepdims=True)
        xn = ((x * lax.rsqrt(ms + EPS)) * g_ref[...]).astype(BF16)
        xn_ref[...] = xn
        if has_aux:
            o2_ref[...] = jnp.dot(xn, w2_ref[...], preferred_element_type=F32)
        if emit_xn:
            xo_ref[...] = xn

    res = jnp.dot(xn_ref[...], w_ref[...], preferred_element_type=F32)
    o_ref[...] = res.astype(o_ref.dtype)
    if heads:
        hd = res.shape[1] // heads
        for h in range(heads):
            oh_ref[:, h, :] = res[:, h * hd:(h + 1) * hd]


def _nmm(x, g, w, *, tm, tn, w_aux=None, emit_xn=False, out_dtype=F32, heads=0):
    m, k = x.shape
    n = w.shape[1]
    w_mode = pl.Buffered(1) if tn == n else None
    in_specs = [pl.BlockSpec((tm, k), lambda i, j: (i, 0)),
                pl.BlockSpec((1, k), lambda i, j: (0, 0)),
                pl.BlockSpec((k, tn), lambda i, j: (0, j), pipeline_mode=w_mode)]
    args = [x, g.reshape(1, k), w]
    out_shape = [jax.ShapeDtypeStruct((m, n), out_dtype)]
    out_specs = [pl.BlockSpec((tm, tn), lambda i, j: (i, j))]
    if w_aux is not None:
        in_specs.append(pl.BlockSpec((k, LANES), lambda i, j: (0, 0)))
        args.append(w_aux)
        out_shape.append(jax.ShapeDtypeStruct((m, LANES), F32))
        out_specs.append(pl.BlockSpec((tm, LANES), lambda i, j: (i, 0)))
    if emit_xn:
        out_shape.append(jax.ShapeDtypeStruct((m, k), BF16))
        out_specs.append(pl.BlockSpec((tm, k), lambda i, j: (i, 0)))
    if heads:
        assert tn == n
        out_shape.append(jax.ShapeDtypeStruct((m, heads, n // heads), F32))
        out_specs.append(pl.BlockSpec((tm, heads, n // heads), lambda i, j: (i, 0, 0)))
    return pl.pallas_call(
        functools.partial(_nmm_body, has_aux=w_aux is not None, emit_xn=emit_xn, heads=heads),
        grid=(m // tm, n // tn),
        in_specs=in_specs, out_specs=out_specs, out_shape=out_shape,
        scratch_shapes=[pltpu.VMEM((tm, k), BF16)],
        compiler_params=_params("parallel", "arbitrary"),
        name="nmm",
    )(*args)


def _mmres_body(x_ref, w_ref, r_ref, o_ref):
    o_ref[...] = r_ref[...] + jnp.dot(x_ref[...].astype(BF16), w_ref[...], preferred_element_type=F32)


def _mmres(x, w, res, *, tm):
    m, k = x.shape
    n = w.shape[1]
    return pl.pallas_call(
        _mmres_body,
        grid=(m // tm,),
        in_specs=[pl.BlockSpec((tm, k), lambda i: (i, 0)),
                  pl.BlockSpec((k, n), lambda i: (0, 0), pipeline_mode=pl.Buffered(1)),
                  pl.BlockSpec((tm, n), lambda i: (i, 0))],
        out_specs=pl.BlockSpec((tm, n), lambda i: (i, 0)),
        out_shape=jax.ShapeDtypeStruct((m, n), F32),
        compiler_params=_params("parallel"),
        name="mmres",
    )(x, w, res)


SGU_STEP_CHUNKS = 2


def _sgu_body(u_ref, v_ref, ws_ref, bs_ref, gs_ref, mix_ref, o_ref, cv_ref):
    del mix_ref
    for c in range(SGU_STEP_CHUNKS):
        rows = slice(c * SGU_CHUNK, (c + 1) * SGU_CHUNK)
        for h in range(A_HEADS):
            sl = slice(h * LANES, (h + 1) * LANES)
            u = _gelu(u_ref[rows, sl])
            v = _gelu(v_ref[rows, sl])
            ms = jnp.mean(v * v, axis=-1, keepdims=True)
            va = (v * lax.rsqrt(ms + EPS)) * gs_ref[:, sl]
            mixed = jnp.dot(ws_ref[h], va.astype(BF16), preferred_element_type=F32) + bs_ref[:, h:h + 1]
            o_ref[rows, sl] = u * mixed
            if c == SGU_STEP_CHUNKS - 1:
                cv_ref[0, :, sl] = va


def _sgu_prompt(z, ws_tril, bs_t, gs, mix):
    step_rows = SGU_STEP_CHUNKS * SGU_CHUNK
    n_steps = SEQ // step_rows
    return pl.pallas_call(
        _sgu_body,
        grid=(T_PROMPT // step_rows,),
        in_specs=[pl.BlockSpec((step_rows, A_WIDTH), lambda i: (i, 0)),
                  pl.BlockSpec((step_rows, A_WIDTH), lambda i: (i, 1)),
                  pl.BlockSpec((A_HEADS, SGU_CHUNK, SGU_CHUNK), lambda i: (0, 0, 0)),
                  pl.BlockSpec((SGU_CHUNK, A_HEADS), lambda i: (0, 0)),
                  pl.BlockSpec((1, A_WIDTH), lambda i: (0, 0)),
                  pl.BlockSpec(memory_space=pl.ANY)],
        out_specs=[pl.BlockSpec((step_rows, A_WIDTH), lambda i: (i, 0)),
                   pl.BlockSpec((1, SGU_CHUNK, A_WIDTH), lambda i: (i // n_steps, 0, 0))],
        out_shape=[jax.ShapeDtypeStruct((T_PAD, D_MODEL), F32),
                   jax.ShapeDtypeStruct((BATCH, SGU_CHUNK, A_WIDTH), F32)],
        input_output_aliases={5: 0},
        compiler_params=_params("arbitrary"),
        name="sgu_prompt",
    )(z, z, ws_tril, bs_t, gs, mix)


GLA_STEP_CHUNKS = 2


def _gla_body(q_ref, k_ref, v_ref, r_ref, zg_ref, wa_ref, ba_ref, gg_ref, mix_ref, o_ref, st_ref, s_ref, g_ref):
    del mix_ref
    c = pl.program_id(1)

    @pl.when(c == 0)
    def _():
        s_ref[...] = jnp.zeros_like(s_ref)

    for cc in range(GLA_STEP_CHUNKS):
        row0 = cc * GLA_CHUNK
        rows = slice(row0, row0 + GLA_CHUNK)
        x = jnp.dot(zg_ref[rows, :].astype(BF16), wa_ref[...], preferred_element_type=F32) + ba_ref[...]
        lg = _log_sigmoid(x) * (1.0 / GATE_TAU)
        tri = (lax.broadcasted_iota(jnp.int32, (GLA_CHUNK, GLA_CHUNK), 1)
               <= lax.broadcasted_iota(jnp.int32, (GLA_CHUNK, GLA_CHUNK), 0)).astype(BF16)
        hi = lg.astype(BF16)
        r1 = lg - hi.astype(F32)
        mid = r1.astype(BF16)
        lo = (r1 - mid.astype(F32)).astype(BF16)
        g_ref[cc] = (jnp.dot(tri, hi, preferred_element_type=F32) + jnp.dot(tri, mid, preferred_element_type=F32)
                     + jnp.dot(tri, lo, preferred_element_type=F32))
        eye = _eye(B_KEY_DIM)
        half = GLA_SUB // 2
        t_iota = lax.broadcasted_iota(jnp.int32, (half, B_KEY_DIM), 0)
        scale = B_KEY_DIM ** -0.5

        for h in range(B_HEADS):
            dk = slice(h * B_KEY_DIM, (h + 1) * B_KEY_DIM)
            dv = slice(h * B_VAL_DIM, (h + 1) * B_VAL_DIM)
            k = k_ref[rows, dk]
            g = g_ref[cc, :, dk]
            vb = v_ref[rows, dv].astype(BF16)
            state = s_ref[h]
            o_inter = jnp.dot((q_ref[rows, dk] * scale * jnp.exp(g)).astype(BF16), state.astype(BF16),
                              preferred_element_type=F32)
            parts = []
            for i in range(GLA_CHUNK // GLA_SUB):
                lo_r, mid_r, hi_r = GLA_SUB * i, GLA_SUB * i + half, GLA_SUB * (i + 1)
                q_lo = q_ref[row0 + lo_r:row0 + mid_r, dk] * scale
                q_hi = q_ref[row0 + mid_r:row0 + hi_r, dk] * scale
                g_lo, g_hi = g_ref[cc, lo_r:mid_r, dk], g_ref[cc, mid_r:hi_r, dk]
                acc_lo = jnp.zeros((half, B_VAL_DIM), F32)
                acc_hi = jnp.zeros((half, B_VAL_DIM), F32)
                for s in range(GLA_SUB):
                    r = lo_r + s
                    k_row, v_row = k_ref[row0 + r:row0 + r + 1, dk], v_ref[row0 + r:row0 + r + 1, dv]
                    g_row = g_ref[cc, r:r + 1, dk]
                    if s < half:
                        dec = jnp.exp(jnp.where(t_iota >= s, g_lo - g_row, -jnp.inf))
                        col = jnp.sum(q_lo * k_row * dec, axis=-1, keepdims=True)
                        acc_lo = acc_lo + col * v_row
                        dec = jnp.exp(g_hi - g_row)
                    else:
                        dec = jnp.exp(jnp.where(t_iota >= s - half, g_hi - g_row, -jnp.inf))
                    col = jnp.sum(q_hi * k_row * dec, axis=-1, keepdims=True)
                    acc_hi = acc_hi + col * v_row
                acc = jnp.concatenate([acc_lo, acc_hi], axis=0)
                if i > 0:
                    gref = g_ref[cc, lo_r - 1:lo_r, dk]
                    qd = (jnp.concatenate([q_lo, q_hi], axis=0) * jnp.exp(g[lo_r:hi_r] - gref)).astype(BF16)
                    kd = (k[:lo_r] * jnp.exp(gref - g[:lo_r])).astype(BF16)
                    sc = lax.dot_general(qd, kd, (((1,), (1,)), ((), ())), preferred_element_type=F32)
                    acc = acc + jnp.dot(sc.astype(BF16), vb[:lo_r], preferred_element_type=F32)
                parts.append(acc)
            o = o_inter + jnp.concatenate(parts, axis=0)

            g_last = g_ref[cc, GLA_CHUNK - 1:GLA_CHUNK, dk]
            kdec = k * jnp.exp(g_last - g)
            upd = jnp.dot(kdec.T.astype(BF16), vb, preferred_element_type=F32)
            s_ref[h] = _row_to_col(jnp.exp(g_last), eye) * state + upd

            ms = jnp.mean(o * o, axis=-1, keepdims=True)
            on = (o * lax.rsqrt(ms + EPS)) * gg_ref[:, dv]
            r = r_ref[rows, dv]
            o_ref[rows, dv] = (r * jax.nn.sigmoid(r)) * on

    @pl.when(c == pl.num_programs(1) - 1)
    def _():
        st_ref[0] = s_ref[...]


def _gla_prompt(z, zg, wa_pad, ba, gg, mix):
    step_rows = GLA_STEP_CHUNKS * GLA_CHUNK
    n_steps = SEQ // step_rows
    row = lambda b, c: b * n_steps + c
    return pl.pallas_call(
        _gla_body,
        grid=(BATCH, n_steps),
        in_specs=[pl.BlockSpec((step_rows, B_KEY_WIDTH), lambda b, c: (row(b, c), 4)),
                  pl.BlockSpec((step_rows, B_KEY_WIDTH), lambda b, c: (row(b, c), 5)),
                  pl.BlockSpec((step_rows, B_VAL_WIDTH), lambda b, c: (row(b, c), 3)),
                  pl.BlockSpec((step_rows, B_VAL_WIDTH), lambda b, c: (row(b, c), 4)),
                  pl.BlockSpec((step_rows, LANES), lambda b, c: (row(b, c), 0)),
                  pl.BlockSpec((LANES, B_KEY_WIDTH), lambda b, c: (0, 0)),
                  pl.BlockSpec((1, B_KEY_WIDTH), lambda b, c: (0, 0)),
                  pl.BlockSpec((1, B_VAL_WIDTH), lambda b, c: (0, 0)),
                  pl.BlockSpec(memory_space=pl.ANY)],
        out_specs=[pl.BlockSpec((step_rows, B_VAL_WIDTH), lambda b, c: (row(b, c), 1)),
                   pl.BlockSpec((1, B_HEADS, B_KEY_DIM, B_VAL_DIM), lambda b, c: (b, 0, 0, 0))],
        out_shape=[jax.ShapeDtypeStruct((T_PAD, D_MODEL), F32),
                   jax.ShapeDtypeStruct((BATCH, B_HEADS, B_KEY_DIM, B_VAL_DIM), F32)],
        scratch_shapes=[pltpu.VMEM((B_HEADS, B_KEY_DIM, B_VAL_DIM), F32),
                        pltpu.VMEM((GLA_STEP_CHUNKS, GLA_CHUNK, B_KEY_WIDTH), F32)],
        input_output_aliases={8: 0},
        compiler_params=_params("arbitrary", "arbitrary"),
        name="gla_prompt",
    )(z, z, z, z, zg, wa_pad, ba, gg, mix)


SAMPLE_ROWS = 8


def _smix_body(z_ref, zg_ref, st_ref, wa_ref, ba_ref, gs_ref, gg_ref, w00_ref, b00_ref, mix_ref,
               o_ref, cv_ref, so_ref):
    del mix_ref
    u = _gelu(z_ref[:, 0:A_WIDTH])
    v = _gelu(z_ref[:, A_WIDTH:2 * A_WIDTH])
    for h in range(A_HEADS):
        sl = slice(h * LANES, (h + 1) * LANES)
        vh = v[:, sl]
        ms = jnp.mean(vh * vh, axis=-1, keepdims=True)
        va = (vh * lax.rsqrt(ms + EPS)) * gs_ref[:, sl]
        cv_ref[:, sl] = va
        o_ref[:, sl] = u[:, sl] * (w00_ref[:, sl] * va + b00_ref[:, sl])

    x = jnp.dot(zg_ref[...].astype(BF16), wa_ref[...], preferred_element_type=F32) + ba_ref[...]
    a = jnp.exp(_log_sigmoid(x) * (1.0 / GATE_TAU))
    q0, k0, v0, r0 = 2 * A_WIDTH, 2 * A_WIDTH + B_KEY_WIDTH, 2 * A_WIDTH + 2 * B_KEY_WIDTH, Z_COLS - B_VAL_WIDTH
    eye = _eye(B_KEY_DIM)
    for b in range(SAMPLE_ROWS):
        for h in range(B_HEADS):
            dk = slice(h * B_KEY_DIM, (h + 1) * B_KEY_DIM)
            a_col = _row_to_col(a[b:b + 1, dk], eye)
            k_col = _row_to_col(z_ref[b:b + 1, k0 + h * B_KEY_DIM:k0 + (h + 1) * B_KEY_DIM], eye)
            q_col = _row_to_col(z_ref[b:b + 1, q0 + h * B_KEY_DIM:q0 + (h + 1) * B_KEY_DIM], eye) * (B_KEY_DIM ** -0.5)
            vrow = z_ref[b:b + 1, v0 + h * B_VAL_DIM:v0 + (h + 1) * B_VAL_DIM]
            s_new = a_col * st_ref[b, h] + k_col * vrow
            so_ref[b, h] = s_new
            o = jnp.sum(q_col * s_new, axis=0, keepdims=True)
            ms = jnp.mean(o * o, axis=-1, keepdims=True)
            on = (o * lax.rsqrt(ms + EPS)) * gg_ref[:, h * B_VAL_DIM:(h + 1) * B_VAL_DIM]
            r = z_ref[b:b + 1, r0 + h * B_VAL_DIM:r0 + (h + 1) * B_VAL_DIM]
            o_ref[b:b + 1, A_WIDTH + h * B_VAL_DIM:A_WIDTH + (h + 1) * B_VAL_DIM] = (r * jax.nn.sigmoid(r)) * on


def _sample_mixers(z, zg, state, wa_pad, ba, gs, gg, w00, b00, mix):
    first = T_PROMPT // SAMPLE_ROWS
    return pl.pallas_call(
        _smix_body,
        grid=(T_SAMPLE // SAMPLE_ROWS,),
        in_specs=[pl.BlockSpec((SAMPLE_ROWS, Z_COLS), lambda i: (first + i, 0)),
                  pl.BlockSpec((SAMPLE_ROWS, LANES), lambda i: (first + i, 0)),
                  pl.BlockSpec((SAMPLE_ROWS, B_HEADS, B_KEY_DIM, B_VAL_DIM), lambda i: (i, 0, 0, 0)),
                  pl.BlockSpec((LANES, B_KEY_WIDTH), lambda i: (0, 0)),
                  pl.BlockSpec((1, B_KEY_WIDTH), lambda i: (0, 0)),
                  pl.BlockSpec((1, A_WIDTH), lambda i: (0, 0)),
                  pl.BlockSpec((1, B_VAL_WIDTH), lambda i: (0, 0)),
                  pl.BlockSpec((1, A_WIDTH), lambda i: (0, 0)),
                  pl.BlockSpec((1, A_WIDTH), lambda i: (0, 0)),
                  pl.BlockSpec(memory_space=pl.ANY)],
        out_specs=[pl.BlockSpec((SAMPLE_ROWS, D_MODEL), lambda i: (first + i, 0)),
                   pl.BlockSpec((SAMPLE_ROWS, A_WIDTH), lambda i: (i, 0)),
                   pl.BlockSpec((SAMPLE_ROWS, B_HEADS, B_KEY_DIM, B_VAL_DIM), lambda i: (i, 0, 0, 0))],
        out_shape=[jax.ShapeDtypeStruct((T_PAD, D_MODEL), F32),
                   jax.ShapeDtypeStruct((T_SAMPLE, A_WIDTH), F32),
                   jax.ShapeDtypeStruct((T_SAMPLE, B_HEADS, B_KEY_DIM, B_VAL_DIM), F32)],
        input_output_aliases={9: 0},
        compiler_params=_params("parallel"),
        name="sample_mixers",
    )(z, zg, state, wa_pad, ba, gs, gg, w00, b00, mix)


ATT_TILE = 1024


def _attn_prompt_body(q_ref, k_ref, v_ref, att_ref, o_ref):
    del att_ref
    for h in range(X_HEADS):
        sl = slice(h * X_HEAD_DIM, (h + 1) * X_HEAD_DIM)
        q = q_ref[:, sl]
        k = k_ref[0, :, sl].astype(BF16)
        s = lax.dot_general(q, k, (((1,), (1,)), ((), ())), preferred_element_type=F32) * (X_HEAD_DIM ** -0.5)
        s = s - jnp.max(s, axis=-1, keepdims=True)
        e = jnp.exp(s)
        p = e / jnp.sum(e, axis=-1, keepdims=True)
        o_ref[:, sl] = jnp.dot(p.astype(BF16), v_ref[0, :, sl].astype(BF16), preferred_element_type=F32).astype(BF16)


def _attn_prompt(qx, mk, mv, att):
    tiles = SEQ // ATT_TILE
    return pl.pallas_call(
        _attn_prompt_body,
        grid=(BATCH, tiles),
        in_specs=[pl.BlockSpec((ATT_TILE, D_MODEL), lambda b, t: (b * tiles + t, 0)),
                  pl.BlockSpec((1, MEM_LEN, D_MODEL), lambda b, t: (b, 0, 0)),
                  pl.BlockSpec((1, MEM_LEN, D_MODEL), lambda b, t: (b, 0, 0)),
                  pl.BlockSpec(memory_space=pl.ANY)],
        out_specs=pl.BlockSpec((ATT_TILE, D_MODEL), lambda b, t: (b * tiles + t, 0)),
        out_shape=jax.ShapeDtypeStruct((T_PAD, D_MODEL), BF16),
        input_output_aliases={3: 0},
        compiler_params=_params("parallel", "parallel"),
        name="attn_prompt",
    )(qx, mk, mv, att)


ATT_SEQS = 2


def _attn_sample_body(q_ref, k_ref, v_ref, o_ref):
    for b in range(ATT_SEQS):
        q = q_ref[b:b + 1].astype(F32)
        s = jnp.sum(k_ref[0, b] * q, axis=-1, keepdims=True) * (X_HEAD_DIM ** -0.5)
        s = s - jnp.max(s, axis=0, keepdims=True)
        e = jnp.exp(s)
        p = e / jnp.sum(e, axis=0, keepdims=True)
        o_ref[b] = jnp.sum(p * v_ref[0, b], axis=0).astype(BF16)


def _attn_sample(q, ck, cv):
    kv_spec = pl.BlockSpec((1, ATT_SEQS, MEM_LEN, X_HEADS, X_HEAD_DIM), lambda i: (0, i, 0, 0, 0))
    return pl.pallas_call(
        _attn_sample_body,
        grid=(T_SAMPLE // ATT_SEQS,),
        in_specs=[pl.BlockSpec((ATT_SEQS, X_HEADS, X_HEAD_DIM), lambda i: (i, 0, 0)), kv_spec, kv_spec],
        out_specs=pl.BlockSpec((ATT_SEQS, X_HEADS, X_HEAD_DIM), lambda i: (i, 0, 0)),
        out_shape=jax.ShapeDtypeStruct((T_SAMPLE, X_HEADS, X_HEAD_DIM), BF16),
        compiler_params=_params("parallel"),
        name="attn_sample",
    )(q, ck, cv)


def _odd_even_merge_sort_pairs(n):
    pairs = []
    p = 1
    while p < n:
        k = p
        while k >= 1:
            for j in range(k % p, n - k, 2 * k):
                for i in range(min(k, n - j - k)):
                    if (i + j) // (2 * p) == (i + j + k) // (2 * p):
                        pairs.append((i + j, i + j + k))
            k //= 2
        p *= 2
    return pairs


_SORT16 = tuple(_odd_even_merge_sort_pairs(PEER_TOPK))


def _exchange(vs, i, j):
    vs[i], vs[j] = jnp.maximum(vs[i], vs[j]), jnp.minimum(vs[i], vs[j])


def _top16_sorted(s):
    vs = [s[8 * k:8 * (k + 1), :] for k in range(PEER_TOPK)]
    for i, j in _SORT16:
        _exchange(vs, i, j)
    for shift in (4, 2, 1):
        vs = _merge_top16(vs, [pltpu.roll(v, shift, 0) for v in vs])
    return vs


def _merge_top16(a, b):
    c = list(a)
    for k, bk in enumerate(b):
        c[PEER_TOPK - 1 - k] = jnp.maximum(a[PEER_TOPK - 1 - k], bk)
    for stride in (8, 4, 2, 1):
        for k in range(PEER_TOPK):
            if k & stride == 0:
                _exchange(c, k, k + stride)
    return c


def _stack_rows(vs, rows8):
    out = vs[0]
    for k in range(1, 8):
        out = jnp.where(rows8 == k, vs[k], out)
    return out


_CAND_ROWS = tuple((a, PEER_TOPK // (a + 1)) for a in range(1, 8))


def _peer_route_body(pq_ref, keys_ref, s0_ref, s1_ref, a_ref, b_ref, th_ref):
    rows8 = lax.broadcasted_iota(jnp.int32, (8, LANES), 0)

    def head(h, carry):
        q0 = pq_ref[:, pl.ds(pl.multiple_of(h * 2 * LANES, LANES), LANES)]
        q1 = pq_ref[:, pl.ds(pl.multiple_of(h * 2 * LANES + LANES, LANES), LANES)]
        nt = (((1,), (1,)), ((), ()))
        s0 = lax.dot_general(keys_ref[2 * h], q0, nt, preferred_element_type=F32)
        s1 = lax.dot_general(keys_ref[2 * h + 1], q1, nt, preferred_element_type=F32)
        r0 = _top16_sorted(s0)
        r1 = _top16_sorted(s1)
        best = [r0[0] + r1[b] for b in range(PEER_TOPK)]
        for a, nb in _CAND_ROWS:
            best = _merge_top16(best, [r0[a] + r1[b] for b in range(nb)])
        best = _merge_top16(best, [r0[a] + r1[0] for a in range(8, PEER_TOPK)])
        theta = best[PEER_TOPK - 1][0:1]
        top = best[0][0:1]
        v1_lo = _stack_rows(r1[0:8], rows8)
        cs = [r0[0] + v1_lo, r0[0] + _stack_rows(r1[8:16], rows8)]
        for a, nb in _CAND_ROWS:
            cs.append(jnp.where(rows8 < nb, r0[a] + v1_lo, -jnp.inf))
        cs.append(_stack_rows(r0[8:16], rows8) + r1[0])
        cand = jnp.concatenate(cs, axis=0)
        zsum = jnp.sum(jnp.where(cand >= theta, jnp.exp(cand - top), 0.0), axis=0, keepdims=True)
        s0_ref[h] = s0
        s1_ref[h] = s1
        a_ref[h] = jnp.exp(s0 - r0[0][0:1]) / zsum
        b_ref[h] = jnp.exp(s1 - r1[0][0:1])
        th_ref[h] = theta
        return carry

    lax.fori_loop(0, PEER_HEADS, head, 0)


def _peer_route(pq, keys):
    big = jax.ShapeDtypeStruct((PEER_HEADS, N_KEYS, T_PAD), F32)
    big_spec = pl.BlockSpec((PEER_HEADS, N_KEYS, LANES), lambda i: (0, 0, i))
    return pl.pallas_call(
        _peer_route_body,
        grid=(T_PAD // LANES,),
        in_specs=[pl.BlockSpec((LANES, D_MODEL), lambda i: (i, 0)),
                  pl.BlockSpec((2 * PEER_HEADS, N_KEYS, LANES), lambda i: (0, 0, 0))],
        out_specs=[big_spec, big_spec, big_spec, big_spec,
                   pl.BlockSpec((PEER_HEADS, 1, LANES), lambda i: (0, 0, i))],
        out_shape=[big, big, big, big, jax.ShapeDtypeStruct((PEER_HEADS, 1, T_PAD), F32)],
        compiler_params=_params("parallel"),
        name="peer_route",
    )(pq, keys)


N_EXPERT_BLOCKS = N_EXPERTS // EXPERT_TILE
N_TOKEN_BLOCKS = T_PAD // TOKEN_TILE
N_PAIRS = N_TOKEN_BLOCKS * N_EXPERT_BLOCKS
PIPE_LAG = 2
GATE_ROWS = 32
N_GROUPS = EXPERT_TILE // N_KEYS


def _pair(n, lag):
    c = jnp.clip(n - lag, 0, N_PAIRS - 1)
    return c // N_EXPERT_BLOCKS, c % N_EXPERT_BLOCKS


def _peer_dense_body(h_ref, u_ref, vt_ref, s0_ref, s1_ref, a_ref, b_ref, th_ref, o_ref, s_even, s_odd, a_even, a_odd):
    n = pl.program_id(0)

    @pl.when(n == 0)
    def _():
        for ref in (s_even, s_odd, a_even, a_odd):
            ref[...] = jnp.zeros_like(ref)

    @pl.when(jnp.logical_or(n == 0, jnp.maximum(n - PIPE_LAG, 0) % N_EXPERT_BLOCKS == 0))
    def _():
        o_ref[...] = jnp.zeros_like(o_ref)

    def run(s_w, s_r, a_w, a_r):
        def cols(nn, carry):
            t0 = pl.multiple_of(nn * MXU_DIM, MXU_DIM)
            tok = pl.ds(t0, MXU_DIM)
            o_ref[:, tok] += jnp.dot(vt_ref[...], a_r[:, tok], preferred_element_type=F32)
            s_w[:, tok] = jnp.dot(u_ref[...], h_ref[:, tok].astype(F32), preferred_element_type=F32)
            for g in range(N_GROUPS):
                for lt in range(MXU_DIM // LANES):
                    lanes = pl.ds(pl.multiple_of(t0 + lt * LANES, LANES), LANES)
                    for j0 in range(0, N_KEYS, GATE_ROWS):
                        keys = slice(j0, j0 + GATE_ROWS)
                        rows = slice(g * N_KEYS + j0, g * N_KEYS + j0 + GATE_ROWS)
                        gate = jnp.zeros((GATE_ROWS, LANES), F32)
                        for h in range(PEER_HEADS):
                            c = s0_ref[h, g:g + 1, lanes] + s1_ref[h, keys, lanes]
                            w = a_ref[h, g:g + 1, lanes] * b_ref[h, keys, lanes]
                            gate = gate + jnp.where(c >= th_ref[h, :, lanes], w, 0.0)
                        a_w[rows, lanes] = (_gelu(s_r[rows, lanes]) * gate).astype(BF16)
            return carry

        lax.fori_loop(0, TOKEN_TILE // MXU_DIM, cols, 0)

    @pl.when(n % 2 == 0)
    def _():
        run(s_even, s_odd, a_odd, a_even)

    @pl.when(n % 2 == 1)
    def _():
        run(s_odd, s_even, a_even, a_odd)


def _peer_dense(hn_t, u, vt_bf, s0, s1, a, b, th):
    once = pl.Buffered(1)
    route_spec = pl.BlockSpec((PEER_HEADS, N_KEYS, TOKEN_TILE), lambda n: (0, 0, _pair(n, 1)[0]), pipeline_mode=once)
    row_spec = pl.BlockSpec((PEER_HEADS, EXPERT_TILE // N_KEYS, TOKEN_TILE),
                            lambda n: (0, _pair(n, 1)[1], _pair(n, 1)[0]))
    return pl.pallas_call(
        _peer_dense_body,
        grid=(N_PAIRS + PIPE_LAG,),
        in_specs=[pl.BlockSpec((D_MODEL, TOKEN_TILE), lambda n: (0, _pair(n, 0)[0]), pipeline_mode=once),
                  pl.BlockSpec((EXPERT_TILE, D_MODEL), lambda n: (_pair(n, 0)[1], 0)),
                  pl.BlockSpec((D_MODEL, EXPERT_TILE), lambda n: (0, _pair(n, PIPE_LAG)[1])),
                  row_spec, route_spec, row_spec, route_spec,
                  pl.BlockSpec((PEER_HEADS, 1, TOKEN_TILE), lambda n: (0, 0, _pair(n, 1)[0]))],
        out_specs=pl.BlockSpec((D_MODEL, TOKEN_TILE), lambda n: (0, _pair(n, PIPE_LAG)[0])),
        out_shape=jax.ShapeDtypeStruct((D_MODEL, T_PAD), F32),
        scratch_shapes=[pltpu.VMEM((EXPERT_TILE, TOKEN_TILE), F32), pltpu.VMEM((EXPERT_TILE, TOKEN_TILE), F32),
                        pltpu.VMEM((EXPERT_TILE, TOKEN_TILE), BF16), pltpu.VMEM((EXPERT_TILE, TOKEN_TILE), BF16)],
        compiler_params=_params("arbitrary", vmem=PEER_VMEM_LIMIT),
        name="peer_dense",
    )(hn_t, u, vt_bf, s0, s1, a, b, th)


VT_ROWS = 512


def _transpose_cast_body(v_ref, o_ref):
    o_ref[...] = v_ref[...].T.astype(BF16)


def _transpose_cast(v):
    n, d = v.shape
    return pl.pallas_call(
        _transpose_cast_body,
        grid=(n // VT_ROWS,),
        in_specs=[pl.BlockSpec((VT_ROWS, d), lambda i: (i, 0))],
        out_specs=pl.BlockSpec((d, VT_ROWS), lambda i: (0, i)),
        out_shape=jax.ShapeDtypeStruct((d, n), BF16),
        compiler_params=_params("parallel"),
        name="transpose_cast",
    )(v)


def _final_body(x_ref, pt_ref, g_ref, o_ref):
    x = x_ref[...] + pt_ref[...].T
    ms = jnp.mean(x * x, axis=-1, keepdims=True)
    o_ref[...] = (x * lax.rsqrt(ms + EPS)) * g_ref[...]


def _final(x2, peer_t, g, first_row, n_rows, tile):
    first_tile = first_row // tile
    return pl.pallas_call(
        _final_body,
        grid=(n_rows // tile,),
        in_specs=[pl.BlockSpec((tile, D_MODEL), lambda i: (first_tile + i, 0)),
                  pl.BlockSpec((D_MODEL, tile), lambda i: (0, first_tile + i)),
                  pl.BlockSpec((1, D_MODEL), lambda i: (0, 0))],
        out_specs=pl.BlockSpec((tile, D_MODEL), lambda i: (i, 0)),
        out_shape=jax.ShapeDtypeStruct((n_rows, D_MODEL), F32),
        compiler_params=_params("parallel"),
        name="final_norm",
    )(x2, peer_t, g.reshape(1, D_MODEL))


def kernel(x_prompt, x_sample, mem_prompt, cache_mem_k, cache_mem_v, state_gla, g_mix, w_in, w_alpha, b_alpha, w_s, b_s, g_sgu, g_gla, w_out, g_mem, w_xk, w_xv, g_xq, w_xq, w_xo, g_ffn, w_pq, sub_keys, u_emb, v_emb, g_final):
    w_in_main = w_in[0, :, :Z_COLS].astype(BF16)
    w_in_gate = jnp.pad(w_in[0, :, Z_COLS:], ((0, 0), (0, LANES - GATE_RANK))).astype(BF16)
    wa_pad = jnp.pad(w_alpha[0], ((0, LANES - GATE_RANK), (0, 0))).astype(BF16)
    ba = b_alpha[0].reshape(1, B_KEY_WIDTH)
    causal = jnp.tril(jnp.ones((SGU_CHUNK, SGU_CHUNK), dtype=bool))
    ws_tril = jnp.where(causal, w_s[0], 0).astype(BF16)
    bs_t = b_s[0].T
    w00 = jnp.repeat(w_s[0, :, 0, 0], LANES).reshape(1, A_WIDTH)
    b00 = jnp.repeat(b_s[0, :, 0], LANES).reshape(1, A_WIDTH)
    gs = g_sgu[0].reshape(1, A_WIDTH)
    gg = g_gla[0].reshape(1, B_VAL_WIDTH)
    keys = sub_keys[0].reshape(2 * PEER_HEADS, N_KEYS, LANES).astype(BF16)
    vt_bf = _transpose_cast(v_emb[0])

    x_all = jnp.concatenate([x_prompt.reshape(T_PROMPT, D_MODEL), x_sample.reshape(T_SAMPLE, D_MODEL),
                             jnp.zeros((T_PAD - T_ALL, D_MODEL), F32)], axis=0)

    mem = mem_prompt.reshape(BATCH * MEM_LEN, D_MODEL)
    mk, mk_heads = _nmm(mem, g_mem[0], w_xk[0].astype(BF16), tm=512, tn=D_MODEL, heads=X_HEADS)
    mv, mv_heads = _nmm(mem, g_mem[0], w_xv[0].astype(BF16), tm=512, tn=D_MODEL, heads=X_HEADS)

    z, zg = _nmm(x_all, g_mix[0], w_in_main, tm=TOKEN_TILE // 2, tn=Z_COLS, w_aux=w_in_gate)
    mix, cv_p = _sgu_prompt(z, ws_tril, bs_t, gs, jnp.zeros((T_PAD, D_MODEL), F32))
    mix, st_p = _gla_prompt(z, zg, wa_pad, ba, gg, mix)
    mix, cv_s, st_s = _sample_mixers(z, zg, state_gla[0], wa_pad, ba, gs, gg, w00, b00, mix)
    x1 = _mmres(mix, w_out[0].astype(BF16), x_all, tm=TOKEN_TILE)

    qx, = _nmm(x1, g_xq[0], w_xq[0].astype(BF16), tm=TOKEN_TILE, tn=D_MODEL, out_dtype=BF16)
    att = _attn_prompt(qx, mk.reshape(BATCH, MEM_LEN, D_MODEL), mv.reshape(BATCH, MEM_LEN, D_MODEL),
                       jnp.zeros((T_PAD, D_MODEL), BF16))
    att_s = _attn_sample(qx[T_PROMPT:T_ALL].reshape(T_SAMPLE, X_HEADS, X_HEAD_DIM), cache_mem_k, cache_mem_v)
    att = lax.dynamic_update_slice(att, att_s.reshape(T_SAMPLE, D_MODEL), (T_PROMPT, 0))
    x2 = _mmres(att, w_xo[0].astype(BF16), x1, tm=TOKEN_TILE)

    pq, hn = _nmm(x2, g_ffn[0], w_pq[0].astype(BF16), tm=TOKEN_TILE, tn=D_MODEL, emit_xn=True, out_dtype=BF16)
    s0, s1, a, b, th = _peer_route(pq, keys)
    peer_t = _peer_dense(hn.T, u_emb[0], vt_bf, s0, s1, a, b, th)

    y_p = _final(x2, peer_t, g_final, 0, T_PROMPT, 2 * LANES)
    y_s = _final(x2, peer_t, g_final, T_PROMPT, T_SAMPLE, LANES)

    return (y_p.reshape(BATCH, SEQ, D_MODEL),
            y_s.reshape(DEC_BATCH, 1, D_MODEL),
            mk_heads.reshape(1, BATCH, MEM_LEN, X_HEADS, X_HEAD_DIM),
            mv_heads.reshape(1, BATCH, MEM_LEN, X_HEADS, X_HEAD_DIM),
            st_p.reshape(1, BATCH, B_HEADS, B_KEY_DIM, B_VAL_DIM),
            st_s.reshape(1, DEC_BATCH, B_HEADS, B_KEY_DIM, B_VAL_DIM),
            cv_p.reshape(1, BATCH, SGU_CHUNK, A_HEADS, LANES),
            cv_s.reshape(1, DEC_BATCH, 1, A_HEADS, LANES))
```

```python
import functools

import jax
import jax.numpy as jnp
from jax import lax
from jax.experimental import pallas as pl
from jax.experimental.pallas import tpu as pltpu

F32 = jnp.float32
BF16 = jnp.bfloat16

D_MODEL = 2048
BATCH = 4
SEQ = 2048
DEC_BATCH = 128
T_PROMPT = BATCH * SEQ
T_SAMPLE = DEC_BATCH
T_ALL = T_PROMPT + T_SAMPLE

A_HEADS = 8
A_WIDTH = 1024
SGU_CHUNK = 128
B_HEADS = 4
B_KEY_DIM = 128
B_VAL_DIM = 256
B_KEY_WIDTH = 512
B_VAL_WIDTH = 1024
GATE_RANK = 16
GATE_TAU = 16.0
GLA_CHUNK = 64
GLA_SUB = 16
Z_COLS = 2 * A_WIDTH + 2 * B_KEY_WIDTH + 2 * B_VAL_WIDTH
MEM_LEN = 256
X_HEADS = 4
X_HEAD_DIM = 512
N_KEYS = 128
N_EXPERTS = N_KEYS * N_KEYS
PEER_HEADS = 8
PEER_TOPK = 16
EPS = 1e-6

LANES = 128
MXU_DIM = 256
TOKEN_TILE = 3 * MXU_DIM
T_PAD = 11 * TOKEN_TILE
EXPERT_TILE = 1024
VMEM_LIMIT = 56 * 1024 * 1024
PEER_VMEM_LIMIT = 60 * 1024 * 1024


def _params(*sem, vmem=VMEM_LIMIT):
    return pltpu.CompilerParams(dimension_semantics=sem, vmem_limit_bytes=vmem)


_GELU_2C = 2.0 * 0.7978845608028654
_LOG2E = 1.4426950408889634


def _gelu(x):
    p = (x * x) * (-_LOG2E * _GELU_2C * 0.044715) + (-_LOG2E * _GELU_2C)
    return x / (1.0 + jnp.exp2(x * p))


def _log_sigmoid(x):
    return jnp.minimum(x, 0.0) - jnp.log1p(jnp.exp(-jnp.abs(x)))


def _row_to_col(row, eye):
    return jnp.sum(jnp.where(eye, row, 0.0), axis=-1, keepdims=True)


def _eye(n):
    return lax.broadcasted_iota(jnp.int32, (n, n), 0) == lax.broadcasted_iota(jnp.int32, (n, n), 1)


def _nmm_body(*refs, has_aux, emit_xn, heads):
    x_ref, g_ref, w_ref = refs[:3]
    rest = list(refs[3:])
    w2_ref = rest.pop(0) if has_aux else None
    o_ref = rest.pop(0)
    o2_ref = rest.pop(0) if has_aux else None
    xo_ref = rest.pop(0) if emit_xn else None
    oh_ref = rest.pop(0) if heads else None
    xn_ref = rest.pop(0)

    @pl.when(pl.program_id(1) == 0)
    def _():
        x = x_ref[...]
        ms = jnp.mean(x * x, axis=-1, keepdims=True)
        xn = ((x * lax.rsqrt(ms + EPS)) * g_ref[...]).astype(BF16)
        xn_ref[...] = xn
        if has_aux:
            o2_ref[...] = jnp.dot(xn, w2_ref[...], preferred_element_type=F32)
        if emit_xn:
            xo_ref[...] = xn

    res = jnp.dot(xn_ref[...], w_ref[...], preferred_element_type=F32)
    o_ref[...] = res.astype(o_ref.dtype)
    if heads:
        hd = res.shape[1] // heads
        for h in range(heads):
            oh_ref[:, h, :] = res[:, h * hd:(h + 1) * hd]


def _nmm(x, g, w, *, tm, tn, w_aux=None, emit_xn=False, out_dtype=F32, heads=0):
    m, k = x.shape
    n = w.shape[1]
    w_mode = pl.Buffered(1) if tn == n else None
    in_specs = [pl.BlockSpec((tm, k), lambda i, j: (i, 0)),
                pl.BlockSpec((1, k), lambda i, j: (0, 0)),
                pl.BlockSpec((k, tn), lambda i, j: (0, j), pipeline_mode=w_mode)]
    args = [x, g.reshape(1, k), w]
    out_shape = [jax.ShapeDtypeStruct((m, n), out_dtype)]
    out_specs = [pl.BlockSpec((tm, tn), lambda i, j: (i, j))]
    if w_aux is not None:
        in_specs.append(pl.BlockSpec((k, LANES), lambda i, j: (0, 0)))
        args.append(w_aux)
        out_shape.append(jax.ShapeDtypeStruct((m, LANES), F32))
        out_specs.append(pl.BlockSpec((tm, LANES), lambda i, j: (i, 0)))
    if emit_xn:
        out_shape.append(jax.ShapeDtypeStruct((m, k), BF16))
        out_specs.append(pl.BlockSpec((tm, k), lambda i, j: (i, 0)))
    if heads:
        assert tn == n
        out_shape.append(jax.ShapeDtypeStruct((m, heads, n // heads), F32))
        out_specs.append(pl.BlockSpec((tm, heads, n // heads), lambda i, j: (i, 0, 0)))
    return pl.pallas_call(
        functools.partial(_nmm_body, has_aux=w_aux is not None, emit_xn=emit_xn, heads=heads),
        grid=(m // tm, n // tn),
        in_specs=in_specs, out_specs=out_specs, out_shape=out_shape,
        scratch_shapes=[pltpu.VMEM((tm, k), BF16)],
        compiler_params=_params("parallel", "arbitrary"),
        name="nmm",
    )(*args)


def _mmres_body(x_ref, w_ref, r_ref, o_ref):
    o_ref[...] = r_ref[...] + jnp.dot(x_ref[...].astype(BF16), w_ref[...], preferred_element_type=F32)


def _mmres(x, w, res, *, tm):
    m, k = x.shape
    n = w.shape[1]
    return pl.pallas_call(
        _mmres_body,
        grid=(m // tm,),
        in_specs=[pl.BlockSpec((tm, k), lambda i: (i, 0)),
                  pl.BlockSpec((k, n), lambda i: (0, 0), pipeline_mode=pl.Buffered(1)),
                  pl.BlockSpec((tm, n), lambda i: (i, 0))],
        out_specs=pl.BlockSpec((tm, n), lambda i: (i, 0)),
        out_shape=jax.ShapeDtypeStruct((m, n), F32),
        compiler_params=_params("parallel"),
        name="mmres",
    )(x, w, res)


SGU_STEP_CHUNKS = 2


def _sgu_body(u_ref, v_ref, ws_ref, bs_ref, gs_ref, mix_ref, o_ref, cv_ref):
    del mix_ref
    for c in range(SGU_STEP_CHUNKS):
        rows = slice(c * SGU_CHUNK, (c + 1) * SGU_CHUNK)
        for h in range(A_HEADS):
            sl = slice(h * LANES, (h + 1) * LANES)
            u = _gelu(u_ref[rows, sl])
            v = _gelu(v_ref[rows, sl])
            ms = jnp.mean(v * v, axis=-1, keepdims=True)
            va = (v * lax.rsqrt(ms + EPS)) * gs_ref[:, sl]
            mixed = jnp.dot(ws_ref[h], va.astype(BF16), preferred_element_type=F32) + bs_ref[:, h:h + 1]
            o_ref[rows, sl] = u * mixed
            if c == SGU_STEP_CHUNKS - 1:
                cv_ref[0, :, sl] = va


def _sgu_prompt(z, ws_tril, bs_t, gs, mix):
    step_rows = SGU_STEP_CHUNKS * SGU_CHUNK
    n_steps = SEQ // step_rows
    return pl.pallas_call(
        _sgu_body,
        grid=(T_PROMPT // step_rows,),
        in_specs=[pl.BlockSpec((step_rows, A_WIDTH), lambda i: (i, 0)),
                  pl.BlockSpec((step_rows, A_WIDTH), lambda i: (i, 1)),
                  pl.BlockSpec((A_HEADS, SGU_CHUNK, SGU_CHUNK), lambda i: (0, 0, 0)),
                  pl.BlockSpec((SGU_CHUNK, A_HEADS), lambda i: (0, 0)),
                  pl.BlockSpec((1, A_WIDTH), lambda i: (0, 0)),
                  pl.BlockSpec(memory_space=pl.ANY)],
        out_specs=[pl.BlockSpec((step_rows, A_WIDTH), lambda i: (i, 0)),
                   pl.BlockSpec((1, SGU_CHUNK, A_WIDTH), lambda i: (i // n_steps, 0, 0))],
        out_shape=[jax.ShapeDtypeStruct((T_PAD, D_MODEL), F32),
                   jax.ShapeDtypeStruct((BATCH, SGU_CHUNK, A_WIDTH), F32)],
        input_output_aliases={5: 0},
        compiler_params=_params("arbitrary"),
        name="sgu_prompt",
    )(z, z, ws_tril, bs_t, gs, mix)


GLA_STEP_CHUNKS = 2


def _gla_body(q_ref, k_ref, v_ref, r_ref, zg_ref, wa_ref, ba_ref, gg_ref, mix_ref, o_ref, st_ref, s_ref, g_ref):
    del mix_ref
    c = pl.program_id(1)

    @pl.when(c == 0)
    def _():
        s_ref[...] = jnp.zeros_like(s_ref)

    for cc in range(GLA_STEP_CHUNKS):
        row0 = cc * GLA_CHUNK
        rows = slice(row0, row0 + GLA_CHUNK)
        x = jnp.dot(zg_ref[rows, :].astype(BF16), wa_ref[...], preferred_element_type=F32) + ba_ref[...]
        lg = _log_sigmoid(x) * (1.0 / GATE_TAU)
        tri = (lax.broadcasted_iota(jnp.int32, (GLA_CHUNK, GLA_CHUNK), 1)
               <= lax.broadcasted_iota(jnp.int32, (GLA_CHUNK, GLA_CHUNK), 0)).astype(BF16)
        hi = lg.astype(BF16)
        r1 = lg - hi.astype(F32)
        mid = r1.astype(BF16)
        lo = (r1 - mid.astype(F32)).astype(BF16)
        g_ref[cc] = (jnp.dot(tri, hi, preferred_element_type=F32) + jnp.dot(tri, mid, preferred_element_type=F32)
                     + jnp.dot(tri, lo, preferred_element_type=F32))
        eye = _eye(B_KEY_DIM)
        half = GLA_SUB // 2
        t_iota = lax.broadcasted_iota(jnp.int32, (half, B_KEY_DIM), 0)
        scale = B_KEY_DIM ** -0.5

        for h in range(B_HEADS):
            dk = slice(h * B_KEY_DIM, (h + 1) * B_KEY_DIM)
            dv = slice(h * B_VAL_DIM, (h + 1) * B_VAL_DIM)
            k = k_ref[rows, dk]
            g = g_ref[cc, :, dk]
            vb = v_ref[rows, dv].astype(BF16)
            state = s_ref[h]
            o_inter = jnp.dot((q_ref[rows, dk] * scale * jnp.exp(g)).astype(BF16), state.astype(BF16),
                              preferred_element_type=F32)
            parts = []
            for i in range(GLA_CHUNK // GLA_SUB):
                lo_r, mid_r, hi_r = GLA_SUB * i, GLA_SUB * i + half, GLA_SUB * (i + 1)
                q_lo = q_ref[row0 + lo_r:row0 + mid_r, dk] * scale
                q_hi = q_ref[row0 + mid_r:row0 + hi_r, dk] * scale
                g_lo, g_hi = g_ref[cc, lo_r:mid_r, dk], g_ref[cc, mid_r:hi_r, dk]
                acc_lo = jnp.zeros((half, B_VAL_DIM), F32)
                acc_hi = jnp.zeros((half, B_VAL_DIM), F32)
                for s in range(GLA_SUB):
                    r = lo_r + s
                    k_row, v_row = k_ref[row0 + r:row0 + r + 1, dk], v_ref[row0 + r:row0 + r + 1, dv]
                    g_row = g_ref[cc, r:r + 1, dk]
                    if s < half:
                        dec = jnp.exp(jnp.where(t_iota >= s, g_lo - g_row, -jnp.inf))
                        col = jnp.sum(q_lo * k_row * dec, axis=-1, keepdims=True)
                        acc_lo = acc_lo + col * v_row
                        dec = jnp.exp(g_hi - g_row)
                    else:
                        dec = jnp.exp(jnp.where(t_iota >= s - half, g_hi - g_row, -jnp.inf))
                    col = jnp.sum(q_hi * k_row * dec, axis=-1, keepdims=True)
                    acc_hi = acc_hi + col * v_row
                acc = jnp.concatenate([acc_lo, acc_hi], axis=0)
                if i > 0:
                    gref = g_ref[cc, lo_r - 1:lo_r, dk]
                    qd = (jnp.concatenate([q_lo, q_hi], axis=0) * jnp.exp(g[lo_r:hi_r] - gref)).astype(BF16)
                    kd = (k[:lo_r] * jnp.exp(gref - g[:lo_r])).astype(BF16)
                    sc = lax.dot_general(qd, kd, (((1,), (1,)), ((), ())), preferred_element_type=F32)
                    acc = acc + jnp.dot(sc.astype(BF16), vb[:lo_r], preferred_element_type=F32)
                parts.append(acc)
            o = o_inter + jnp.concatenate(parts, axis=0)

            g_last = g_ref[cc, GLA_CHUNK - 1:GLA_CHUNK, dk]
            kdec = k * jnp.exp(g_last - g)
            upd = jnp.dot(kdec.T.astype(BF16), vb, preferred_element_type=F32)
            s_ref[h] = _row_to_col(jnp.exp(g_last), eye) * state + upd

            ms = jnp.mean(o * o, axis=-1, keepdims=True)
            on = (o * lax.rsqrt(ms + EPS)) * gg_ref[:, dv]
            r = r_ref[rows, dv]
            o_ref[rows, dv] = (r * jax.nn.sigmoid(r)) * on

    @pl.when(c == pl.num_programs(1) - 1)
    def _():
        st_ref[0] = s_ref[...]


def _gla_prompt(z, zg, wa_pad, ba, gg, mix):
    step_rows = GLA_STEP_CHUNKS * GLA_CHUNK
    n_steps = SEQ // step_rows
    row = lambda b, c: b * n_steps + c
    return pl.pallas_call(
        _gla_body,
        grid=(BATCH, n_steps),
        in_specs=[pl.BlockSpec((step_rows, B_KEY_WIDTH), lambda b, c: (row(b, c), 4)),
                  pl.BlockSpec((step_rows, B_KEY_WIDTH), lambda b, c: (row(b, c), 5)),
                  pl.BlockSpec((step_rows, B_VAL_WIDTH), lambda b, c: (row(b, c), 3)),
                  pl.BlockSpec((step_rows, B_VAL_WIDTH), lambda b, c: (row(b, c), 4)),
                  pl.BlockSpec((step_rows, LANES), lambda b, c: (row(b, c), 0)),
                  pl.BlockSpec((LANES, B_KEY_WIDTH), lambda b, c: (0, 0)),
                  pl.BlockSpec((1, B_KEY_WIDTH), lambda b, c: (0, 0)),
                  pl.BlockSpec((1, B_VAL_WIDTH), lambda b, c: (0, 0)),
                  pl.BlockSpec(memory_space=pl.ANY)],
        out_specs=[pl.BlockSpec((step_rows, B_VAL_WIDTH), lambda b, c: (row(b, c), 1)),
                   pl.BlockSpec((1, B_HEADS, B_KEY_DIM, B_VAL_DIM), lambda b, c: (b, 0, 0, 0))],
        out_shape=[jax.ShapeDtypeStruct((T_PAD, D_MODEL), F32),
                   jax.ShapeDtypeStruct((BATCH, B_HEADS, B_KEY_DIM, B_VAL_DIM), F32)],
        scratch_shapes=[pltpu.VMEM((B_HEADS, B_KEY_DIM, B_VAL_DIM), F32),
                        pltpu.VMEM((GLA_STEP_CHUNKS, GLA_CHUNK, B_KEY_WIDTH), F32)],
        input_output_aliases={8: 0},
        compiler_params=_params("arbitrary", "arbitrary"),
        name="gla_prompt",
    )(z, z, z, z, zg, wa_pad, ba, gg, mix)


SAMPLE_ROWS = 8


def _smix_body(z_ref, zg_ref, st_ref, wa_ref, ba_ref, gs_ref, gg_ref, w00_ref, b00_ref, mix_ref,
               o_ref, cv_ref, so_ref):
    del mix_ref
    u = _gelu(z_ref[:, 0:A_WIDTH])
    v = _gelu(z_ref[:, A_WIDTH:2 * A_WIDTH])
    for h in range(A_HEADS):
        sl = slice(h * LANES, (h + 1) * LANES)
        vh = v[:, sl]
        ms = jnp.mean(vh * vh, axis=-1, keepdims=True)
        va = (vh * lax.rsqrt(ms + EPS)) * gs_ref[:, sl]
        cv_ref[:, sl] = va
        o_ref[:, sl] = u[:, sl] * (w00_ref[:, sl] * va + b00_ref[:, sl])

    x = jnp.dot(zg_ref[...].astype(BF16), wa_ref[...], preferred_element_type=F32) + ba_ref[...]
    a = jnp.exp(_log_sigmoid(x) * (1.0 / GATE_TAU))
    q0, k0, v0, r0 = 2 * A_WIDTH, 2 * A_WIDTH + B_KEY_WIDTH, 2 * A_WIDTH + 2 * B_KEY_WIDTH, Z_COLS - B_VAL_WIDTH
    eye = _eye(B_KEY_DIM)
    for b in range(SAMPLE_ROWS):
        for h in range(B_HEADS):
            dk = slice(h * B_KEY_DIM, (h + 1) * B_KEY_DIM)
            a_col = _row_to_col(a[b:b + 1, dk], eye)
            k_col = _row_to_col(z_ref[b:b + 1, k0 + h * B_KEY_DIM:k0 + (h + 1) * B_KEY_DIM], eye)
            q_col = _row_to_col(z_ref[b:b + 1, q0 + h * B_KEY_DIM:q0 + (h + 1) * B_KEY_DIM], eye) * (B_KEY_DIM ** -0.5)
            vrow = z_ref[b:b + 1, v0 + h * B_VAL_DIM:v0 + (h + 1) * B_VAL_DIM]
            s_new = a_col * st_ref[b, h] + k_col * vrow
            so_ref[b, h] = s_new
            o = jnp.sum(q_col * s_new, axis=0, keepdims=True)
            ms = jnp.mean(o * o, axis=-1, keepdims=True)
            on = (o * lax.rsqrt(ms + EPS)) * gg_ref[:, h * B_VAL_DIM:(h + 1) * B_VAL_DIM]
            r = z_ref[b:b + 1, r0 + h * B_VAL_DIM:r0 + (h + 1) * B_VAL_DIM]
            o_ref[b:b + 1, A_WIDTH + h * B_VAL_DIM:A_WIDTH + (h + 1) * B_VAL_DIM] = (r * jax.nn.sigmoid(r)) * on


def _sample_mixers(z, zg, state, wa_pad, ba, gs, gg, w00, b00, mix):
    first = T_PROMPT // SAMPLE_ROWS
    return pl.pallas_call(
        _smix_body,
        grid=(T_SAMPLE // SAMPLE_ROWS,),
        in_specs=[pl.BlockSpec((SAMPLE_ROWS, Z_COLS), lambda i: (first + i, 0)),
                  pl.BlockSpec((SAMPLE_ROWS, LANES), lambda i: (first + i, 0)),
                  pl.BlockSpec((SAMPLE_ROWS, B_HEADS, B_KEY_DIM, B_VAL_DIM), lambda i: (i, 0, 0, 0)),
                  pl.BlockSpec((LANES, B_KEY_WIDTH), lambda i: (0, 0)),
                  pl.BlockSpec((1, B_KEY_WIDTH), lambda i: (0, 0)),
                  pl.BlockSpec((1, A_WIDTH), lambda i: (0, 0)),
                  pl.BlockSpec((1, B_VAL_WIDTH), lambda i: (0, 0)),
                  pl.BlockSpec((1, A_WIDTH), lambda i: (0, 0)),
                  pl.BlockSpec((1, A_WIDTH), lambda i: (0, 0)),
                  pl.BlockSpec(memory_space=pl.ANY)],
        out_specs=[pl.BlockSpec((SAMPLE_ROWS, D_MODEL), lambda i: (first + i, 0)),
                   pl.BlockSpec((SAMPLE_ROWS, A_WIDTH), lambda i: (i, 0)),
                   pl.BlockSpec((SAMPLE_ROWS, B_HEADS, B_KEY_DIM, B_VAL_DIM), lambda i: (i, 0, 0, 0))],
        out_shape=[jax.ShapeDtypeStruct((T_PAD, D_MODEL), F32),
                   jax.ShapeDtypeStruct((T_SAMPLE, A_WIDTH), F32),
                   jax.ShapeDtypeStruct((T_SAMPLE, B_HEADS, B_KEY_DIM, B_VAL_DIM), F32)],
        input_output_aliases={9: 0},
        compiler_params=_params("parallel"),
        name="sample_mixers",
    )(z, zg, state, wa_pad, ba, gs, gg, w00, b00, mix)


ATT_TILE = 1024


def _attn_prompt_body(q_ref, k_ref, v_ref, att_ref, o_ref):
    del att_ref
    for h in range(X_HEADS):
        sl = slice(h * X_HEAD_DIM, (h + 1) * X_HEAD_DIM)
        q = q_ref[:, sl]
        k = k_ref[0, :, sl].astype(BF16)
        s = lax.dot_general(q, k, (((1,), (1,)), ((), ())), preferred_element_type=F32) * (X_HEAD_DIM ** -0.5)
        s = s - jnp.max(s, axis=-1, keepdims=True)
        e = jnp.exp(s)
        p = e / jnp.sum(e, axis=-1, keepdims=True)
        o_ref[:, sl] = jnp.dot(p.astype(BF16), v_ref[0, :, sl].astype(BF16), preferred_element_type=F32).astype(BF16)


def _attn_prompt(qx, mk, mv, att):
    tiles = SEQ // ATT_TILE
    return pl.pallas_call(
        _attn_prompt_body,
        grid=(BATCH, tiles),
        in_specs=[pl.BlockSpec((ATT_TILE, D_MODEL), lambda b, t: (b * tiles + t, 0)),
                  pl.BlockSpec((1, MEM_LEN, D_MODEL), lambda b, t: (b, 0, 0)),
                  pl.BlockSpec((1, MEM_LEN, D_MODEL), lambda b, t: (b, 0, 0)),
                  pl.BlockSpec(memory_space=pl.ANY)],
        out_specs=pl.BlockSpec((ATT_TILE, D_MODEL), lambda b, t: (b * tiles + t, 0)),
        out_shape=jax.ShapeDtypeStruct((T_PAD, D_MODEL), BF16),
        input_output_aliases={3: 0},
        compiler_params=_params("parallel", "parallel"),
        name="attn_prompt",
    )(qx, mk, mv, att)


ATT_SEQS = 2


def _attn_sample_body(q_ref, k_ref, v_ref, o_ref):
    for b in range(ATT_SEQS):
        q = q_ref[b:b + 1].astype(F32)
        s = jnp.sum(k_ref[0, b] * q, axis=-1, keepdims=True) * (X_HEAD_DIM ** -0.5)
        s = s - jnp.max(s, axis=0, keepdims=True)
        e = jnp.exp(s)
        p = e / jnp.sum(e, axis=0, keepdims=True)
        o_ref[b] = jnp.sum(p * v_ref[0, b], axis=0).astype(BF16)


def _attn_sample(q, ck, cv):
    kv_spec = pl.BlockSpec((1, ATT_SEQS, MEM_LEN, X_HEADS, X_HEAD_DIM), lambda i: (0, i, 0, 0, 0))
    return pl.pallas_call(
        _attn_sample_body,
        grid=(T_SAMPLE // ATT_SEQS,),
        in_specs=[pl.BlockSpec((ATT_SEQS, X_HEADS, X_HEAD_DIM), lambda i: (i, 0, 0)), kv_spec, kv_spec],
        out_specs=pl.BlockSpec((ATT_SEQS, X_HEADS, X_HEAD_DIM), lambda i: (i, 0, 0)),
        out_shape=jax.ShapeDtypeStruct((T_SAMPLE, X_HEADS, X_HEAD_DIM), BF16),
        compiler_params=_params("parallel"),
        name="attn_sample",
    )(q, ck, cv)


def _odd_even_merge_sort_pairs(n):
    pairs = []
    p = 1
    while p < n:
        k = p
        while k >= 1:
            for j in range(k % p, n - k, 2 * k):
                for i in range(min(k, n - j - k)):
                    if (i + j) // (2 * p) == (i + j + k) // (2 * p):
                        pairs.append((i + j, i + j + k))
            k //= 2
        p *= 2
    return pairs


_SORT16 = tuple(_odd_even_merge_sort_pairs(PEER_TOPK))


def _exchange(vs, i, j):
    vs[i], vs[j] = jnp.maximum(vs[i], vs[j]), jnp.minimum(vs[i], vs[j])


def _top16_sorted(s):
    vs = [s[8 * k:8 * (k + 1), :] for k in range(PEER_TOPK)]
    for i, j in _SORT16:
        _exchange(vs, i, j)
    for shift in (4, 2, 1):
        vs = _merge_top16(vs, [pltpu.roll(v, shift, 0) for v in vs])
    return vs


def _merge_top16(a, b):
    c = list(a)
    for k, bk in enumerate(b):
        c[PEER_TOPK - 1 - k] = jnp.maximum(a[PEER_TOPK - 1 - k], bk)
    for stride in (8, 4, 2, 1):
        for k in range(PEER_TOPK):
            if k & stride == 0:
                _exchange(c, k, k + stride)
    return c


def _stack_rows(vs, rows8):
    out = vs[0]
    for k in range(1, 8):
        out = jnp.where(rows8 == k, vs[k], out)
    return out


_CAND_ROWS = tuple((a, PEER_TOPK // (a + 1)) for a in range(1, 8))


def _peer_route_body(pq_ref, keys_ref, s0_ref, s1_ref, a_ref, b_ref, th_ref):
    rows8 = lax.broadcasted_iota(jnp.int32, (8, LANES), 0)

    def head(h, carry):
        q0 = pq_ref[:, pl.ds(pl.multiple_of(h * 2 * LANES, LANES), LANES)]
        q1 = pq_ref[:, pl.ds(pl.multiple_of(h * 2 * LANES + LANES, LANES), LANES)]
        nt = (((1,), (1,)), ((), ()))
        s0 = lax.dot_general(keys_ref[2 * h], q0, nt, preferred_element_type=F32)
        s1 = lax.dot_general(keys_ref[2 * h + 1], q1, nt, preferred_element_type=F32)
        r0 = _top16_sorted(s0)
        r1 = _top16_sorted(s1)
        best = [r0[0] + r1[b] for b in range(PEER_TOPK)]
        for a, nb in _CAND_ROWS:
            best = _merge_top16(best, [r0[a] + r1[b] for b in range(nb)])
        best = _merge_top16(best, [r0[a] + r1[0] for a in range(8, PEER_TOPK)])
        theta = best[PEER_TOPK - 1][0:1]
        top = best[0][0:1]
        v1_lo = _stack_rows(r1[0:8], rows8)
        cs = [r0[0] + v1_lo, r0[0] + _stack_rows(r1[8:16], rows8)]
        for a, nb in _CAND_ROWS:
            cs.append(jnp.where(rows8 < nb, r0[a] + v1_lo, -jnp.inf))
        cs.append(_stack_rows(r0[8:16], rows8) + r1[0])
        cand = jnp.concatenate(cs, axis=0)
        zsum = jnp.sum(jnp.where(cand >= theta, jnp.exp(cand - top), 0.0), axis=0, keepdims=True)
        s0_ref[h] = s0
        s1_ref[h] = s1
        a_ref[h] = jnp.exp(s0 - r0[0][0:1]) / zsum
        b_ref[h] = jnp.exp(s1 - r1[0][0:1])
        th_ref[h] = theta
        return carry

    lax.fori_loop(0, PEER_HEADS, head, 0)


def _peer_route(pq, keys):
    big = jax.ShapeDtypeStruct((PEER_HEADS, N_KEYS, T_PAD), F32)
    big_spec = pl.BlockSpec((PEER_HEADS, N_KEYS, LANES), lambda i: (0, 0, i))
    return pl.pallas_call(
        _peer_route_body,
        grid=(T_PAD // LANES,),
        in_specs=[pl.BlockSpec((LANES, D_MODEL), lambda i: (i, 0)),
                  pl.BlockSpec((2 * PEER_HEADS, N_KEYS, LANES), lambda i: (0, 0, 0))],
        out_specs=[big_spec, big_spec, big_spec, big_spec,
                   pl.BlockSpec((PEER_HEADS, 1, LANES), lambda i: (0, 0, i))],
        out_shape=[big, big, big, big, jax.ShapeDtypeStruct((PEER_HEADS, 1, T_PAD), F32)],
        compiler_params=_params("parallel"),
        name="peer_route",
    )(pq, keys)


N_EXPERT_BLOCKS = N_EXPERTS // EXPERT_TILE
N_TOKEN_BLOCKS = T_PAD // TOKEN_TILE
N_PAIRS = N_TOKEN_BLOCKS * N_EXPERT_BLOCKS
PIPE_LAG = 2
GATE_ROWS = 32
N_GROUPS = EXPERT_TILE // N_KEYS


def _pair(n, lag):
    c = jnp.clip(n - lag, 0, N_PAIRS - 1)
    return c // N_EXPERT_BLOCKS, c % N_EXPERT_BLOCKS


def _peer_dense_body(h_ref, u_ref, vt_ref, s0_ref, s1_ref, a_ref, b_ref, th_ref, o_ref, s_even, s_odd, a_even, a_odd):
    n = pl.program_id(0)

    @pl.when(n == 0)
    def _():
        for ref in (s_even, s_odd, a_even, a_odd):
            ref[...] = jnp.zeros_like(ref)

    @pl.when(jnp.logical_or(n == 0, jnp.maximum(n - PIPE_LAG, 0) % N_EXPERT_BLOCKS == 0))
    def _():
        o_ref[...] = jnp.zeros_like(o_ref)

    def run(s_w, s_r, a_w, a_r):
        def cols(nn, carry):
            t0 = pl.multiple_of(nn * MXU_DIM, MXU_DIM)
            tok = pl.ds(t0, MXU_DIM)
            o_ref[:, tok] += jnp.dot(vt_ref[...], a_r[:, tok], preferred_element_type=F32)
            s_w[:, tok] = jnp.dot(u_ref[...], h_ref[:, tok].astype(F32), preferred_element_type=F32)
            for g in range(N_GROUPS):
                for lt in range(MXU_DIM // LANES):
                    lanes = pl.ds(pl.multiple_of(t0 + lt * LANES, LANES), LANES)
                    for j0 in range(0, N_KEYS, GATE_ROWS):
                        keys = slice(j0, j0 + GATE_ROWS)
                        rows = slice(g * N_KEYS + j0, g * N_KEYS + j0 + GATE_ROWS)
                        gate = jnp.zeros((GATE_ROWS, LANES), F32)
                        for h in range(PEER_HEADS):
                            c = s0_ref[h, g:g + 1, lanes] + s1_ref[h, keys, lanes]
                            w = a_ref[h, g:g + 1, lanes] * b_ref[h, keys, lanes]
                            gate = gate + jnp.where(c >= th_ref[h, :, lanes], w, 0.0)
                        a_w[rows, lanes] = (_gelu(s_r[rows, lanes]) * gate).astype(BF16)
            return carry

        lax.fori_loop(0, TOKEN_TILE // MXU_DIM, cols, 0)

    @pl.when(n % 2 == 0)
    def _():
        run(s_even, s_odd, a_odd, a_even)

    @pl.when(n % 2 == 1)
    def _():
        run(s_odd, s_even, a_even, a_odd)


def _peer_dense(hn_t, u, vt_bf, s0, s1, a, b, th):
    once = pl.Buffered(1)
    route_spec = pl.BlockSpec((PEER_HEADS, N_KEYS, TOKEN_TILE), lambda n: (0, 0, _pair(n, 1)[0]), pipeline_mode=once)
    row_spec = pl.BlockSpec((PEER_HEADS, EXPERT_TILE // N_KEYS, TOKEN_TILE),
                            lambda n: (0, _pair(n, 1)[1], _pair(n, 1)[0]))
    return pl.pallas_call(
        _peer_dense_body,
        grid=(N_PAIRS + PIPE_LAG,),
        in_specs=[pl.BlockSpec((D_MODEL, TOKEN_TILE), lambda n: (0, _pair(n, 0)[0]), pipeline_mode=once),
                  pl.BlockSpec((EXPERT_TILE, D_MODEL), lambda n: (_pair(n, 0)[1], 0)),
                  pl.BlockSpec((D_MODEL, EXPERT_TILE), lambda n: (0, _pair(n, PIPE_LAG)[1])),
                  row_spec, route_spec, row_spec, route_spec,
                  pl.BlockSpec((PEER_HEADS, 1, TOKEN_TILE), lambda n: (0, 0, _pair(n, 1)[0]))],
        out_specs=pl.BlockSpec((D_MODEL, TOKEN_TILE), lambda n: (0, _pair(n, PIPE_LAG)[0])),
        out_shape=jax.ShapeDtypeStruct((D_MODEL, T_PAD), F32),
        scratch_shapes=[pltpu.VMEM((EXPERT_TILE, TOKEN_TILE), F32), pltpu.VMEM((EXPERT_TILE, TOKEN_TILE), F32),
                        pltpu.VMEM((EXPERT_TILE, TOKEN_TILE), BF16), pltpu.VMEM((EXPERT_TILE, TOKEN_TILE), BF16)],
        compiler_params=_params("arbitrary", vmem=PEER_VMEM_LIMIT),
        name="peer_dense",
    )(hn_t, u, vt_bf, s0, s1, a, b, th)


VT_ROWS = 1024


def _transpose_cast_body(v_ref, o_ref):
    o_ref[...] = v_ref[...].T.astype(BF16)


def _transpose_cast(v):
    n, d = v.shape
    return pl.pallas_call(
        _transpose_cast_body,
        grid=(n // VT_ROWS,),
        in_specs=[pl.BlockSpec((VT_ROWS, d), lambda i: (i, 0))],
        out_specs=pl.BlockSpec((d, VT_ROWS), lambda i: (0, i)),
        out_shape=jax.ShapeDtypeStruct((d, n), BF16),
        compiler_params=_params("parallel"),
        name="transpose_cast",
    )(v)


def _final_body(x_ref, pt_ref, g_ref, o_ref):
    x = x_ref[...] + pt_ref[...].T
    ms = jnp.mean(x * x, axis=-1, keepdims=True)
    o_ref[...] = (x * lax.rsqrt(ms + EPS)) * g_ref[...]


def _final(x2, peer_t, g, first_row, n_rows, tile):
    first_tile = first_row // tile
    return pl.pallas_call(
        _final_body,
        grid=(n_rows // tile,),
        in_specs=[pl.BlockSpec((tile, D_MODEL), lambda i: (first_tile + i, 0)),
                  pl.BlockSpec((D_MODEL, tile), lambda i: (0, first_tile + i)),
                  pl.BlockSpec((1, D_MODEL), lambda i: (0, 0))],
        out_specs=pl.BlockSpec((tile, D_MODEL), lambda i: (i, 0)),
        out_shape=jax.ShapeDtypeStruct((n_rows, D_MODEL), F32),
        compiler_params=_params("parallel"),
        name="final_norm",
    )(x2, peer_t, g.reshape(1, D_MODEL))


def kernel(x_prompt, x_sample, mem_prompt, cache_mem_k, cache_mem_v, state_gla, g_mix, w_in, w_alpha, b_alpha, w_s, b_s, g_sgu, g_gla, w_out, g_mem, w_xk, w_xv, g_xq, w_xq, w_xo, g_ffn, w_pq, sub_keys, u_emb, v_emb, g_final):
    w_in_main = w_in[0, :, :Z_COLS].astype(BF16)
    w_in_gate = jnp.pad(w_in[0, :, Z_COLS:], ((0, 0), (0, LANES - GATE_RANK))).astype(BF16)
    wa_pad = jnp.pad(w_alpha[0], ((0, LANES - GATE_RANK), (0, 0))).astype(BF16)
    ba = b_alpha[0].reshape(1, B_KEY_WIDTH)
    causal = jnp.tril(jnp.ones((SGU_CHUNK, SGU_CHUNK), dtype=bool))
    ws_tril = jnp.where(causal, w_s[0], 0).astype(BF16)
    bs_t = b_s[0].T
    w00 = jnp.repeat(w_s[0, :, 0, 0], LANES).reshape(1, A_WIDTH)
    b00 = jnp.repeat(b_s[0, :, 0], LANES).reshape(1, A_WIDTH)
    gs = g_sgu[0].reshape(1, A_WIDTH)
    gg = g_gla[0].reshape(1, B_VAL_WIDTH)
    keys = sub_keys[0].reshape(2 * PEER_HEADS, N_KEYS, LANES).astype(BF16)
    vt_bf = _transpose_cast(v_emb[0])

    x_all = jnp.concatenate([x_prompt.reshape(T_PROMPT, D_MODEL), x_sample.reshape(T_SAMPLE, D_MODEL),
                             jnp.zeros((T_PAD - T_ALL, D_MODEL), F32)], axis=0)

    mem = mem_prompt.reshape(BATCH * MEM_LEN, D_MODEL)
    mk, mk_heads = _nmm(mem, g_mem[0], w_xk[0].astype(BF16), tm=512, tn=D_MODEL, heads=X_HEADS)
    mv, mv_heads = _nmm(mem, g_mem[0], w_xv[0].astype(BF16), tm=512, tn=D_MODEL, heads=X_HEADS)

    z, zg = _nmm(x_all, g_mix[0], w_in_main, tm=TOKEN_TILE // 2, tn=Z_COLS, w_aux=w_in_gate)
    mix, cv_p = _sgu_prompt(z, ws_tril, bs_t, gs, jnp.zeros((T_PAD, D_MODEL), F32))
    mix, st_p = _gla_prompt(z, zg, wa_pad, ba, gg, mix)
    mix, cv_s, st_s = _sample_mixers(z, zg, state_gla[0], wa_pad, ba, gs, gg, w00, b00, mix)
    x1 = _mmres(mix, w_out[0].astype(BF16), x_all, tm=TOKEN_TILE)

    qx, = _nmm(x1, g_xq[0], w_xq[0].astype(BF16), tm=TOKEN_TILE, tn=D_MODEL, out_dtype=BF16)
    att = _attn_prompt(qx, mk.reshape(BATCH, MEM_LEN, D_MODEL), mv.reshape(BATCH, MEM_LEN, D_MODEL),
                       jnp.zeros((T_PAD, D_MODEL), BF16))
    att_s = _attn_sample(qx[T_PROMPT:T_ALL].reshape(T_SAMPLE, X_HEADS, X_HEAD_DIM), cache_mem_k, cache_mem_v)
    att = lax.dynamic_update_slice(att, att_s.reshape(T_SAMPLE, D_MODEL), (T_PROMPT, 0))
    x2 = _mmres(att, w_xo[0].astype(BF16), x1, tm=TOKEN_TILE)

    pq, hn = _nmm(x2, g_ffn[0], w_pq[0].astype(BF16), tm=TOKEN_TILE, tn=D_MODEL, emit_xn=True, out_dtype=BF16)
    s0, s1, a, b, th = _peer_route(pq, keys)
    peer_t = _peer_dense(hn.T, u_emb[0], vt_bf, s0, s1, a, b, th)

    y_p = _final(x2, peer_t, g_final, 0, T_PROMPT, 2 * LANES)
    y_s = _final(x2, peer_t, g_final, T_PROMPT, T_SAMPLE, LANES)

    return (y_p.reshape(BATCH, SEQ, D_MODEL),
            y_s.reshape(DEC_BATCH, 1, D_MODEL),
            mk_heads.reshape(1, BATCH, MEM_LEN, X_HEADS, X_HEAD_DIM),
            mv_heads.reshape(1, BATCH, MEM_LEN, X_HEADS, X_HEAD_DIM),
            st_p.reshape(1, BATCH, B_HEADS, B_KEY_DIM, B_VAL_DIM),
            st_s.reshape(1, DEC_BATCH, B_HEADS, B_KEY_DIM, B_VAL_DIM),
            cv_p.reshape(1, BATCH, SGU_CHUNK, A_HEADS, LANES),
            cv_s.reshape(1, DEC_BATCH, 1, A_HEADS, LANES))
```

```python
import functools

import jax
import jax.numpy as jnp
from jax import lax
from jax.experimental import pallas as pl
from jax.experimental.pallas import tpu as pltpu

F32 = jnp.float32
BF16 = jnp.bfloat16

D_MODEL = 2048
BATCH = 4
SEQ = 2048
DEC_BATCH = 128
T_PROMPT = BATCH * SEQ
T_SAMPLE = DEC_BATCH
T_ALL = T_PROMPT + T_SAMPLE

A_HEADS = 8
A_WIDTH = 1024
SGU_CHUNK = 128
B_HEADS = 4
B_KEY_DIM = 128
B_VAL_DIM = 256
B_KEY_WIDTH = 512
B_VAL_WIDTH = 1024
GATE_RANK = 16
GATE_TAU = 16.0
GLA_CHUNK = 64
GLA_SUB = 16
Z_COLS = 2 * A_WIDTH + 2 * B_KEY_WIDTH + 2 * B_VAL_WIDTH
MEM_LEN = 256
X_HEADS = 4
X_HEAD_DIM = 512
N_KEYS = 128
N_EXPERTS = N_KEYS * N_KEYS
PEER_HEADS = 8
PEER_TOPK = 16
EPS = 1e-6

LANES = 128
MXU_DIM = 256
TOKEN_TILE = 3 * MXU_DIM
T_PAD = 11 * TOKEN_TILE
EXPERT_TILE = 1024
VMEM_LIMIT = 56 * 1024 * 1024
PEER_VMEM_LIMIT = 60 * 1024 * 1024


def _params(*sem, vmem=VMEM_LIMIT):
    return pltpu.CompilerParams(dimension_semantics=sem, vmem_limit_bytes=vmem)


_GELU_2C = 2.0 * 0.7978845608028654
_LOG2E = 1.4426950408889634


def _gelu(x):
    p = (x * x) * (-_LOG2E * _GELU_2C * 0.044715) + (-_LOG2E * _GELU_2C)
    return x / (1.0 + jnp.exp2(x * p))


def _log_sigmoid(x):
    return jnp.minimum(x, 0.0) - jnp.log1p(jnp.exp(-jnp.abs(x)))


def _row_to_col(row, eye):
    return jnp.sum(jnp.where(eye, row, 0.0), axis=-1, keepdims=True)


def _eye(n):
    return lax.broadcasted_iota(jnp.int32, (n, n), 0) == lax.broadcasted_iota(jnp.int32, (n, n), 1)


def _nmm_body(*refs, has_aux, emit_xn, heads):
    x_ref, g_ref, w_ref = refs[:3]
    rest = list(refs[3:])
    w2_ref = rest.pop(0) if has_aux else None
    o_ref = rest.pop(0)
    o2_ref = rest.pop(0) if has_aux else None
    xo_ref = rest.pop(0) if emit_xn else None
    oh_ref = rest.pop(0) if heads else None
    xn_ref = rest.pop(0)

    @pl.when(pl.program_id(1) == 0)
    def _():
        x = x_ref[...]
        ms = jnp.mean(x * x, axis=-1, keepdims=True)
        xn = ((x * lax.rsqrt(ms + EPS)) * g_ref[...]).astype(BF16)
        xn_ref[...] = xn
        if has_aux:
            o2_ref[...] = jnp.dot(xn, w2_ref[...], preferred_element_type=F32)
        if emit_xn:
            xo_ref[...] = xn

    res = jnp.dot(xn_ref[...], w_ref[...], preferred_element_type=F32)
    o_ref[...] = res.astype(o_ref.dtype)
    if heads:
        hd = res.shape[1] // heads
        for h in range(heads):
            oh_ref[:, h, :] = res[:, h * hd:(h + 1) * hd]


def _nmm(x, g, w, *, tm, tn, w_aux=None, emit_xn=False, out_dtype=F32, heads=0):
    m, k = x.shape
    n = w.shape[1]
    w_mode = pl.Buffered(1) if tn == n else None
    in_specs = [pl.BlockSpec((tm, k), lambda i, j: (i, 0)),
                pl.BlockSpec((1, k), lambda i, j: (0, 0)),
                pl.BlockSpec((k, tn), lambda i, j: (0, j), pipeline_mode=w_mode)]
    args = [x, g.reshape(1, k), w]
    out_shape = [jax.ShapeDtypeStruct((m, n), out_dtype)]
    out_specs = [pl.BlockSpec((tm, tn), lambda i, j: (i, j))]
    if w_aux is not None:
        in_specs.append(pl.BlockSpec((k, LANES), lambda i, j: (0, 0)))
        args.append(w_aux)
        out_shape.append(jax.ShapeDtypeStruct((m, LANES), F32))
        out_specs.append(pl.BlockSpec((tm, LANES), lambda i, j: (i, 0)))
    if emit_xn:
        out_shape.append(jax.ShapeDtypeStruct((m, k), BF16))
        out_specs.append(pl.BlockSpec((tm, k), lambda i, j: (i, 0)))
    if heads:
        assert tn == n
        out_shape.append(jax.ShapeDtypeStruct((m, heads, n // heads), F32))
        out_specs.append(pl.BlockSpec((tm, heads, n // heads), lambda i, j: (i, 0, 0)))
    return pl.pallas_call(
        functools.partial(_nmm_body, has_aux=w_aux is not None, emit_xn=emit_xn, heads=heads),
        grid=(m // tm, n // tn),
        in_specs=in_specs, out_specs=out_specs, out_shape=out_shape,
        scratch_shapes=[pltpu.VMEM((tm, k), BF16)],
        compiler_params=_params("parallel", "arbitrary"),
        name="nmm",
    )(*args)


def _mmres_body(x_ref, w_ref, r_ref, o_ref):
    o_ref[...] = r_ref[...] + jnp.dot(x_ref[...].astype(BF16), w_ref[...], preferred_element_type=F32)


def _mmres(x, w, res, *, tm):
    m, k = x.shape
    n = w.shape[1]
    return pl.pallas_call(
        _mmres_body,
        grid=(m // tm,),
        in_specs=[pl.BlockSpec((tm, k), lambda i: (i, 0)),
                  pl.BlockSpec((k, n), lambda i: (0, 0), pipeline_mode=pl.Buffered(1)),
                  pl.BlockSpec((tm, n), lambda i: (i, 0))],
        out_specs=pl.BlockSpec((tm, n), lambda i: (i, 0)),
        out_shape=jax.ShapeDtypeStruct((m, n), F32),
        compiler_params=_params("parallel"),
        name="mmres",
    )(x, w, res)


SGU_STEP_CHUNKS = 2


def _sgu_body(u_ref, v_ref, ws_ref, bs_ref, gs_ref, mix_ref, o_ref, cv_ref):
    del mix_ref
    for c in range(SGU_STEP_CHUNKS):
        rows = slice(c * SGU_CHUNK, (c + 1) * SGU_CHUNK)
        for h in range(A_HEADS):
            sl = slice(h * LANES, (h + 1) * LANES)
            u = _gelu(u_ref[rows, sl])
            v = _gelu(v_ref[rows, sl])
            ms = jnp.mean(v * v, axis=-1, keepdims=True)
            va = (v * lax.rsqrt(ms + EPS)) * gs_ref[:, sl]
            mixed = jnp.dot(ws_ref[h], va.astype(BF16), preferred_element_type=F32) + bs_ref[:, h:h + 1]
            o_ref[rows, sl] = u * mixed
            if c == SGU_STEP_CHUNKS - 1:
                cv_ref[0, :, sl] = va


def _sgu_prompt(z, ws_tril, bs_t, gs, mix):
    step_rows = SGU_STEP_CHUNKS * SGU_CHUNK
    n_steps = SEQ // step_rows
    return pl.pallas_call(
        _sgu_body,
        grid=(T_PROMPT // step_rows,),
        in_specs=[pl.BlockSpec((step_rows, A_WIDTH), lambda i: (i, 0)),
                  pl.BlockSpec((step_rows, A_WIDTH), lambda i: (i, 1)),
                  pl.BlockSpec((A_HEADS, SGU_CHUNK, SGU_CHUNK), lambda i: (0, 0, 0)),
                  pl.BlockSpec((SGU_CHUNK, A_HEADS), lambda i: (0, 0)),
                  pl.BlockSpec((1, A_WIDTH), lambda i: (0, 0)),
                  pl.BlockSpec(memory_space=pl.ANY)],
        out_specs=[pl.BlockSpec((step_rows, A_WIDTH), lambda i: (i, 0)),
                   pl.BlockSpec((1, SGU_CHUNK, A_WIDTH), lambda i: (i // n_steps, 0, 0))],
        out_shape=[jax.ShapeDtypeStruct((T_PAD, D_MODEL), F32),
                   jax.ShapeDtypeStruct((BATCH, SGU_CHUNK, A_WIDTH), F32)],
        input_output_aliases={5: 0},
        compiler_params=_params("arbitrary"),
        name="sgu_prompt",
    )(z, z, ws_tril, bs_t, gs, mix)


GLA_STEP_CHUNKS = 2


def _gla_body(q_ref, k_ref, v_ref, r_ref, zg_ref, wa_ref, ba_ref, gg_ref, mix_ref, o_ref, st_ref, s_ref, g_ref):
    del mix_ref
    c = pl.program_id(1)

    @pl.when(c == 0)
    def _():
        s_ref[...] = jnp.zeros_like(s_ref)

    for cc in range(GLA_STEP_CHUNKS):
        row0 = cc * GLA_CHUNK
        rows = slice(row0, row0 + GLA_CHUNK)
        x = jnp.dot(zg_ref[rows, :].astype(BF16), wa_ref[...], preferred_element_type=F32) + ba_ref[...]
        lg = _log_sigmoid(x) * (1.0 / GATE_TAU)
        tri = (lax.broadcasted_iota(jnp.int32, (GLA_CHUNK, GLA_CHUNK), 1)
               <= lax.broadcasted_iota(jnp.int32, (GLA_CHUNK, GLA_CHUNK), 0)).astype(BF16)
        hi = lg.astype(BF16)
        r1 = lg - hi.astype(F32)
        mid = r1.astype(BF16)
        lo = (r1 - mid.astype(F32)).astype(BF16)
        g_ref[cc] = (jnp.dot(tri, hi, preferred_element_type=F32) + jnp.dot(tri, mid, preferred_element_type=F32)
                     + jnp.dot(tri, lo, preferred_element_type=F32))
        eye = _eye(B_KEY_DIM)
        half = GLA_SUB // 2
        t_iota = lax.broadcasted_iota(jnp.int32, (half, B_KEY_DIM), 0)
        scale = B_KEY_DIM ** -0.5

        def head(h, carry, cc=cc, row0=row0, rows=rows):
            dk = pl.ds(pl.multiple_of(h * B_KEY_DIM, B_KEY_DIM), B_KEY_DIM)
            dv = pl.ds(pl.multiple_of(h * B_VAL_DIM, B_VAL_DIM), B_VAL_DIM)
            k = k_ref[rows, dk]
            g = g_ref[cc, :, dk]
            vb = v_ref[rows, dv].astype(BF16)
            state = s_ref[h]
            o_inter = jnp.dot((q_ref[rows, dk] * scale * jnp.exp(g)).astype(BF16), state.astype(BF16),
                              preferred_element_type=F32)
            parts = []
            for i in range(GLA_CHUNK // GLA_SUB):
                lo_r, mid_r, hi_r = GLA_SUB * i, GLA_SUB * i + half, GLA_SUB * (i + 1)
                q_lo = q_ref[row0 + lo_r:row0 + mid_r, dk] * scale
                q_hi = q_ref[row0 + mid_r:row0 + hi_r, dk] * scale
                g_lo, g_hi = g_ref[cc, lo_r:mid_r, dk], g_ref[cc, mid_r:hi_r, dk]
                acc_lo = jnp.zeros((half, B_VAL_DIM), F32)
                acc_hi = jnp.zeros((half, B_VAL_DIM), F32)
                for s in range(GLA_SUB):
                    r = lo_r + s
                    k_row, v_row = k_ref[row0 + r:row0 + r + 1, dk], v_ref[row0 + r:row0 + r + 1, dv]
                    g_row = g_ref[cc, r:r + 1, dk]
                    if s < half:
                        dec = jnp.exp(jnp.where(t_iota >= s, g_lo - g_row, -jnp.inf))
                        col = jnp.sum(q_lo * k_row * dec, axis=-1, keepdims=True)
                        acc_lo = acc_lo + col * v_row
                        dec = jnp.exp(g_hi - g_row)
                    else:
                        dec = jnp.exp(jnp.where(t_iota >= s - half, g_hi - g_row, -jnp.inf))
                    col = jnp.sum(q_hi * k_row * dec, axis=-1, keepdims=True)
                    acc_hi = acc_hi + col * v_row
                acc = jnp.concatenate([acc_lo, acc_hi], axis=0)
                if i > 0:
                    gref = g_ref[cc, lo_r - 1:lo_r, dk]
                    qd = (jnp.concatenate([q_lo, q_hi], axis=0) * jnp.exp(g[lo_r:hi_r] - gref)).astype(BF16)
                    kd = (k[:lo_r] * jnp.exp(gref - g[:lo_r])).astype(BF16)
                    sc = lax.dot_general(qd, kd, (((1,), (1,)), ((), ())), preferred_element_type=F32)
                    acc = acc + jnp.dot(sc.astype(BF16), vb[:lo_r], preferred_element_type=F32)
                parts.append(acc)
            o = o_inter + jnp.concatenate(parts, axis=0)

            g_last = g_ref[cc, GLA_CHUNK - 1:GLA_CHUNK, dk]
            kdec = k * jnp.exp(g_last - g)
            upd = jnp.dot(kdec.T.astype(BF16), vb, preferred_element_type=F32)
            s_ref[h] = _row_to_col(jnp.exp(g_last), eye) * state + upd

            ms = jnp.mean(o * o, axis=-1, keepdims=True)
            on = (o * lax.rsqrt(ms + EPS)) * gg_ref[:, dv]
            r = r_ref[rows, dv]
            o_ref[rows, dv] = (r * jax.nn.sigmoid(r)) * on
            return carry

        lax.fori_loop(0, B_HEADS, head, 0)

    @pl.when(c == pl.num_programs(1) - 1)
    def _():
        st_ref[0] = s_ref[...]


def _gla_prompt(z, zg, wa_pad, ba, gg, mix):
    step_rows = GLA_STEP_CHUNKS * GLA_CHUNK
    n_steps = SEQ // step_rows
    row = lambda b, c: b * n_steps + c
    return pl.pallas_call(
        _gla_body,
        grid=(BATCH, n_steps),
        in_specs=[pl.BlockSpec((step_rows, B_KEY_WIDTH), lambda b, c: (row(b, c), 4)),
                  pl.BlockSpec((step_rows, B_KEY_WIDTH), lambda b, c: (row(b, c), 5)),
                  pl.BlockSpec((step_rows, B_VAL_WIDTH), lambda b, c: (row(b, c), 3)),
                  pl.BlockSpec((step_rows, B_VAL_WIDTH), lambda b, c: (row(b, c), 4)),
                  pl.BlockSpec((step_rows, LANES), lambda b, c: (row(b, c), 0)),
                  pl.BlockSpec((LANES, B_KEY_WIDTH), lambda b, c: (0, 0)),
                  pl.BlockSpec((1, B_KEY_WIDTH), lambda b, c: (0, 0)),
                  pl.BlockSpec((1, B_VAL_WIDTH), lambda b, c: (0, 0)),
                  pl.BlockSpec(memory_space=pl.ANY)],
        out_specs=[pl.BlockSpec((step_rows, B_VAL_WIDTH), lambda b, c: (row(b, c), 1)),
                   pl.BlockSpec((1, B_HEADS, B_KEY_DIM, B_VAL_DIM), lambda b, c: (b, 0, 0, 0))],
        out_shape=[jax.ShapeDtypeStruct((T_PAD, D_MODEL), F32),
                   jax.ShapeDtypeStruct((BATCH, B_HEADS, B_KEY_DIM, B_VAL_DIM), F32)],
        scratch_shapes=[pltpu.VMEM((B_HEADS, B_KEY_DIM, B_VAL_DIM), F32),
                        pltpu.VMEM((GLA_STEP_CHUNKS, GLA_CHUNK, B_KEY_WIDTH), F32)],
        input_output_aliases={8: 0},
        compiler_params=_params("arbitrary", "arbitrary"),
        name="gla_prompt",
    )(z, z, z, z, zg, wa_pad, ba, gg, mix)


SAMPLE_ROWS = 8


def _smix_body(z_ref, zg_ref, st_ref, wa_ref, ba_ref, gs_ref, gg_ref, w00_ref, b00_ref, mix_ref,
               o_ref, cv_ref, so_ref):
    del mix_ref
    u = _gelu(z_ref[:, 0:A_WIDTH])
    v = _gelu(z_ref[:, A_WIDTH:2 * A_WIDTH])
    for h in range(A_HEADS):
        sl = slice(h * LANES, (h + 1) * LANES)
        vh = v[:, sl]
        ms = jnp.mean(vh * vh, axis=-1, keepdims=True)
        va = (vh * lax.rsqrt(ms + EPS)) * gs_ref[:, sl]
        cv_ref[:, sl] = va
        o_ref[:, sl] = u[:, sl] * (w00_ref[:, sl] * va + b00_ref[:, sl])

    x = jnp.dot(zg_ref[...].astype(BF16), wa_ref[...], preferred_element_type=F32) + ba_ref[...]
    a = jnp.exp(_log_sigmoid(x) * (1.0 / GATE_TAU))
    q0, k0, v0, r0 = 2 * A_WIDTH, 2 * A_WIDTH + B_KEY_WIDTH, 2 * A_WIDTH + 2 * B_KEY_WIDTH, Z_COLS - B_VAL_WIDTH
    eye = _eye(B_KEY_DIM)
    for b in range(SAMPLE_ROWS):
        for h in range(B_HEADS):
            dk = slice(h * B_KEY_DIM, (h + 1) * B_KEY_DIM)
            a_col = _row_to_col(a[b:b + 1, dk], eye)
            k_col = _row_to_col(z_ref[b:b + 1, k0 + h * B_KEY_DIM:k0 + (h + 1) * B_KEY_DIM], eye)
            q_col = _row_to_col(z_ref[b:b + 1, q0 + h * B_KEY_DIM:q0 + (h + 1) * B_KEY_DIM], eye) * (B_KEY_DIM ** -0.5)
            vrow = z_ref[b:b + 1, v0 + h * B_VAL_DIM:v0 + (h + 1) * B_VAL_DIM]
            s_new = a_col * st_ref[b, h] + k_col * vrow
            so_ref[b, h] = s_new
            o = jnp.sum(q_col * s_new, axis=0, keepdims=True)
            ms = jnp.mean(o * o, axis=-1, keepdims=True)
            on = (o * lax.rsqrt(ms + EPS)) * gg_ref[:, h * B_VAL_DIM:(h + 1) * B_VAL_DIM]
            r = z_ref[b:b + 1, r0 + h * B_VAL_DIM:r0 + (h + 1) * B_VAL_DIM]
            o_ref[b:b + 1, A_WIDTH + h * B_VAL_DIM:A_WIDTH + (h + 1) * B_VAL_DIM] = (r * jax.nn.sigmoid(r)) * on


def _sample_mixers(z, zg, state, wa_pad, ba, gs, gg, w00, b00, mix):
    first = T_PROMPT // SAMPLE_ROWS
    return pl.pallas_call(
        _smix_body,
        grid=(T_SAMPLE // SAMPLE_ROWS,),
        in_specs=[pl.BlockSpec((SAMPLE_ROWS, Z_COLS), lambda i: (first + i, 0)),
                  pl.BlockSpec((SAMPLE_ROWS, LANES), lambda i: (first + i, 0)),
                  pl.BlockSpec((SAMPLE_ROWS, B_HEADS, B_KEY_DIM, B_VAL_DIM), lambda i: (i, 0, 0, 0)),
                  pl.BlockSpec((LANES, B_KEY_WIDTH), lambda i: (0, 0)),
                  pl.BlockSpec((1, B_KEY_WIDTH), lambda i: (0, 0)),
                  pl.BlockSpec((1, A_WIDTH), lambda i: (0, 0)),
                  pl.BlockSpec((1, B_VAL_WIDTH), lambda i: (0, 0)),
                  pl.BlockSpec((1, A_WIDTH), lambda i: (0, 0)),
                  pl.BlockSpec((1, A_WIDTH), lambda i: (0, 0)),
                  pl.BlockSpec(memory_space=pl.ANY)],
        out_specs=[pl.BlockSpec((SAMPLE_ROWS, D_MODEL), lambda i: (first + i, 0)),
                   pl.BlockSpec((SAMPLE_ROWS, A_WIDTH), lambda i: (i, 0)),
                   pl.BlockSpec((SAMPLE_ROWS, B_HEADS, B_KEY_DIM, B_VAL_DIM), lambda i: (i, 0, 0, 0))],
        out_shape=[jax.ShapeDtypeStruct((T_PAD, D_MODEL), F32),
                   jax.ShapeDtypeStruct((T_SAMPLE, A_WIDTH), F32),
                   jax.ShapeDtypeStruct((T_SAMPLE, B_HEADS, B_KEY_DIM, B_VAL_DIM), F32)],
        input_output_aliases={9: 0},
        compiler_params=_params("parallel"),
        name="sample_mixers",
    )(z, zg, state, wa_pad, ba, gs, gg, w00, b00, mix)


ATT_TILE = 1024


def _attn_prompt_body(q_ref, k_ref, v_ref, att_ref, o_ref):
    del att_ref
    for h in range(X_HEADS):
        sl = slice(h * X_HEAD_DIM, (h + 1) * X_HEAD_DIM)
        q = q_ref[:, sl]
        k = k_ref[0, :, sl].astype(BF16)
        s = lax.dot_general(q, k, (((1,), (1,)), ((), ())), preferred_element_type=F32) * (X_HEAD_DIM ** -0.5)
        s = s - jnp.max(s, axis=-1, keepdims=True)
        e = jnp.exp(s)
        p = e / jnp.sum(e, axis=-1, keepdims=True)
        o_ref[:, sl] = jnp.dot(p.astype(BF16), v_ref[0, :, sl].astype(BF16), preferred_element_type=F32).astype(BF16)


def _attn_prompt(qx, mk, mv, att):
    tiles = SEQ // ATT_TILE
    return pl.pallas_call(
        _attn_prompt_body,
        grid=(BATCH, tiles),
        in_specs=[pl.BlockSpec((ATT_TILE, D_MODEL), lambda b, t: (b * tiles + t, 0)),
                  pl.BlockSpec((1, MEM_LEN, D_MODEL), lambda b, t: (b, 0, 0)),
                  pl.BlockSpec((1, MEM_LEN, D_MODEL), lambda b, t: (b, 0, 0)),
                  pl.BlockSpec(memory_space=pl.ANY)],
        out_specs=pl.BlockSpec((ATT_TILE, D_MODEL), lambda b, t: (b * tiles + t, 0)),
        out_shape=jax.ShapeDtypeStruct((T_PAD, D_MODEL), BF16),
        input_output_aliases={3: 0},
        compiler_params=_params("parallel", "parallel"),
        name="attn_prompt",
    )(qx, mk, mv, att)


ATT_SEQS = 2


def _attn_sample_body(q_ref, k_ref, v_ref, o_ref):
    for b in range(ATT_SEQS):
        q = q_ref[b:b + 1].astype(F32)
        s = jnp.sum(k_ref[0, b] * q, axis=-1, keepdims=True) * (X_HEAD_DIM ** -0.5)
        s = s - jnp.max(s, axis=0, keepdims=True)
        e = jnp.exp(s)
        p = e / jnp.sum(e, axis=0, keepdims=True)
        o_ref[b] = jnp.sum(p * v_ref[0, b], axis=0).astype(BF16)


def _attn_sample(q, ck, cv):
    kv_spec = pl.BlockSpec((1, ATT_SEQS, MEM_LEN, X_HEADS, X_HEAD_DIM), lambda i: (0, i, 0, 0, 0))
    return pl.pallas_call(
        _attn_sample_body,
        grid=(T_SAMPLE // ATT_SEQS,),
        in_specs=[pl.BlockSpec((ATT_SEQS, X_HEADS, X_HEAD_DIM), lambda i: (i, 0, 0)), kv_spec, kv_spec],
        out_specs=pl.BlockSpec((ATT_SEQS, X_HEADS, X_HEAD_DIM), lambda i: (i, 0, 0)),
        out_shape=jax.ShapeDtypeStruct((T_SAMPLE, X_HEADS, X_HEAD_DIM), BF16),
        compiler_params=_params("parallel"),
        name="attn_sample",
    )(q, ck, cv)


def _odd_even_merge_sort_pairs(n):
    pairs = []
    p = 1
    while p < n:
        k = p
        while k >= 1:
            for j in range(k % p, n - k, 2 * k):
                for i in range(min(k, n - j - k)):
                    if (i + j) // (2 * p) == (i + j + k) // (2 * p):
                        pairs.append((i + j, i + j + k))
            k //= 2
        p *= 2
    return pairs


_SORT16 = tuple(_odd_even_merge_sort_pairs(PEER_TOPK))


def _exchange(vs, i, j):
    vs[i], vs[j] = jnp.maximum(vs[i], vs[j]), jnp.minimum(vs[i], vs[j])


def _top16_sorted(s):
    vs = [s[8 * k:8 * (k + 1), :] for k in range(PEER_TOPK)]
    for i, j in _SORT16:
        _exchange(vs, i, j)
    for shift in (4, 2, 1):
        vs = _merge_top16(vs, [pltpu.roll(v, shift, 0) for v in vs])
    return vs


def _merge_top16(a, b):
    c = list(a)
    for k, bk in enumerate(b):
        c[PEER_TOPK - 1 - k] = jnp.maximum(a[PEER_TOPK - 1 - k], bk)
    for stride in (8, 4, 2, 1):
        for k in range(PEER_TOPK):
            if k & stride == 0:
                _exchange(c, k, k + stride)
    return c


def _stack_rows(vs, rows8):
    out = vs[0]
    for k in range(1, 8):
        out = jnp.where(rows8 == k, vs[k], out)
    return out


_CAND_ROWS = tuple((a, PEER_TOPK // (a + 1)) for a in range(1, 8))


def _peer_route_body(pq_ref, keys_ref, s0_ref, s1_ref, a_ref, b_ref, th_ref):
    rows8 = lax.broadcasted_iota(jnp.int32, (8, LANES), 0)

    def head(h, carry):
        q0 = pq_ref[:, pl.ds(pl.multiple_of(h * 2 * LANES, LANES), LANES)]
        q1 = pq_ref[:, pl.ds(pl.multiple_of(h * 2 * LANES + LANES, LANES), LANES)]
        nt = (((1,), (1,)), ((), ()))
        s0 = lax.dot_general(keys_ref[2 * h], q0, nt, preferred_element_type=F32)
        s1 = lax.dot_general(keys_ref[2 * h + 1], q1, nt, preferred_element_type=F32)
        r0 = _top16_sorted(s0)
        r1 = _top16_sorted(s1)
        best = [r0[0] + r1[b] for b in range(PEER_TOPK)]
        for a, nb in _CAND_ROWS:
            best = _merge_top16(best, [r0[a] + r1[b] for b in range(nb)])
        best = _merge_top16(best, [r0[a] + r1[0] for a in range(8, PEER_TOPK)])
        theta = best[PEER_TOPK - 1][0:1]
        top = best[0][0:1]
        v1_lo = _stack_rows(r1[0:8], rows8)
        cs = [r0[0] + v1_lo, r0[0] + _stack_rows(r1[8:16], rows8)]
        for a, nb in _CAND_ROWS:
            cs.append(jnp.where(rows8 < nb, r0[a] + v1_lo, -jnp.inf))
        cs.append(_stack_rows(r0[8:16], rows8) + r1[0])
        cand = jnp.concatenate(cs, axis=0)
        zsum = jnp.sum(jnp.where(cand >= theta, jnp.exp(cand - top), 0.0), axis=0, keepdims=True)
        s0_ref[h] = s0
        s1_ref[h] = s1
        a_ref[h] = jnp.exp(s0 - r0[0][0:1]) / zsum
        b_ref[h] = jnp.exp(s1 - r1[0][0:1])
        th_ref[h] = theta
        return carry

    lax.fori_loop(0, PEER_HEADS, head, 0)


def _peer_route(pq, keys):
    big = jax.ShapeDtypeStruct((PEER_HEADS, N_KEYS, T_PAD), F32)
    big_spec = pl.BlockSpec((PEER_HEADS, N_KEYS, LANES), lambda i: (0, 0, i))
    return pl.pallas_call(
        _peer_route_body,
        grid=(T_PAD // LANES,),
        in_specs=[pl.BlockSpec((LANES, D_MODEL), lambda i: (i, 0)),
                  pl.BlockSpec((2 * PEER_HEADS, N_KEYS, LANES), lambda i: (0, 0, 0))],
        out_specs=[big_spec, big_spec, big_spec, big_spec,
                   pl.BlockSpec((PEER_HEADS, 1, LANES), lambda i: (0, 0, i))],
        out_shape=[big, big, big, big, jax.ShapeDtypeStruct((PEER_HEADS, 1, T_PAD), F32)],
        compiler_params=_params("parallel"),
        name="peer_route",
    )(pq, keys)


N_EXPERT_BLOCKS = N_EXPERTS // EXPERT_TILE
N_TOKEN_BLOCKS = T_PAD // TOKEN_TILE
N_PAIRS = N_TOKEN_BLOCKS * N_EXPERT_BLOCKS
PIPE_LAG = 2
GATE_ROWS = 32
N_GROUPS = EXPERT_TILE // N_KEYS


def _pair(n, lag):
    c = jnp.clip(n - lag, 0, N_PAIRS - 1)
    return c // N_EXPERT_BLOCKS, c % N_EXPERT_BLOCKS


def _peer_dense_body(h_ref, u_ref, vt_ref, s0_ref, s1_ref, a_ref, b_ref, th_ref, o_ref, s_even, s_odd, a_even, a_odd):
    n = pl.program_id(0)

    @pl.when(n == 0)
    def _():
        for ref in (s_even, s_odd, a_even, a_odd):
            ref[...] = jnp.zeros_like(ref)

    @pl.when(jnp.logical_or(n == 0, jnp.maximum(n - PIPE_LAG, 0) % N_EXPERT_BLOCKS == 0))
    def _():
        o_ref[...] = jnp.zeros_like(o_ref)

    def run(s_w, s_r, a_w, a_r):
        def cols(nn, carry):
            t0 = pl.multiple_of(nn * MXU_DIM, MXU_DIM)
            tok = pl.ds(t0, MXU_DIM)
            o_ref[:, tok] += jnp.dot(vt_ref[...], a_r[:, tok], preferred_element_type=F32)
            s_w[:, tok] = jnp.dot(u_ref[...], h_ref[:, tok].astype(F32), preferred_element_type=F32)
            for g in range(N_GROUPS):
                for lt in range(MXU_DIM // LANES):
                    lanes = pl.ds(pl.multiple_of(t0 + lt * LANES, LANES), LANES)
                    for j0 in range(0, N_KEYS, GATE_ROWS):
                        keys = slice(j0, j0 + GATE_ROWS)
                        rows = slice(g * N_KEYS + j0, g * N_KEYS + j0 + GATE_ROWS)
                        gate = jnp.zeros((GATE_ROWS, LANES), F32)
                        for h in range(PEER_HEADS):
                            c = s0_ref[h, g:g + 1, lanes] + s1_ref[h, keys, lanes]
                            w = a_ref[h, g:g + 1, lanes] * b_ref[h, keys, lanes]
                            gate = gate + jnp.where(c >= th_ref[h, :, lanes], w, 0.0)
                        a_w[rows, lanes] = (_gelu(s_r[rows, lanes]) * gate).astype(BF16)
            return carry

        lax.fori_loop(0, TOKEN_TILE // MXU_DIM, cols, 0)

    @pl.when(n % 2 == 0)
    def _():
        run(s_even, s_odd, a_odd, a_even)

    @pl.when(n % 2 == 1)
    def _():
        run(s_odd, s_even, a_even, a_odd)


def _peer_dense(hn_t, u, vt_bf, s0, s1, a, b, th):
    once = pl.Buffered(1)
    route_spec = pl.BlockSpec((PEER_HEADS, N_KEYS, TOKEN_TILE), lambda n: (0, 0, _pair(n, 1)[0]), pipeline_mode=once)
    row_spec = pl.BlockSpec((PEER_HEADS, EXPERT_TILE // N_KEYS, TOKEN_TILE),
                            lambda n: (0, _pair(n, 1)[1], _pair(n, 1)[0]))
    return pl.pallas_call(
        _peer_dense_body,
        grid=(N_PAIRS + PIPE_LAG,),
        in_specs=[pl.BlockSpec((D_MODEL, TOKEN_TILE), lambda n: (0, _pair(n, 0)[0]), pipeline_mode=once),
                  pl.BlockSpec((EXPERT_TILE, D_MODEL), lambda n: (_pair(n, 0)[1], 0)),
                  pl.BlockSpec((D_MODEL, EXPERT_TILE), lambda n: (0, _pair(n, PIPE_LAG)[1])),
                  row_spec, route_spec, row_spec, route_spec,
                  pl.BlockSpec((PEER_HEADS, 1, TOKEN_TILE), lambda n: (0, 0, _pair(n, 1)[0]))],
        out_specs=pl.BlockSpec((D_MODEL, TOKEN_TILE), lambda n: (0, _pair(n, PIPE_LAG)[0])),
        out_shape=jax.ShapeDtypeStruct((D_MODEL, T_PAD), F32),
        scratch_shapes=[pltpu.VMEM((EXPERT_TILE, TOKEN_TILE), F32), pltpu.VMEM((EXPERT_TILE, TOKEN_TILE), F32),
                        pltpu.VMEM((EXPERT_TILE, TOKEN_TILE), BF16), pltpu.VMEM((EXPERT_TILE, TOKEN_TILE), BF16)],
        compiler_params=_params("arbitrary", vmem=PEER_VMEM_LIMIT),
        name="peer_dense",
    )(hn_t, u, vt_bf, s0, s1, a, b, th)


VT_ROWS = 1024


def _transpose_cast_body(v_ref, o_ref):
    o_ref[...] = v_ref[...].T.astype(BF16)


def _transpose_cast(v):
    n, d = v.shape
    return pl.pallas_call(
        _transpose_cast_body,
        grid=(n // VT_ROWS,),
        in_specs=[pl.BlockSpec((VT_ROWS, d), lambda i: (i, 0))],
        out_specs=pl.BlockSpec((d, VT_ROWS), lambda i: (0, i)),
        out_shape=jax.ShapeDtypeStruct((d, n), BF16),
        compiler_params=_params("parallel"),
        name="transpose_cast",
    )(v)


def _final_body(x_ref, pt_ref, g_ref, o_ref):
    x = x_ref[...] + pt_ref[...].T
    ms = jnp.mean(x * x, axis=-1, keepdims=True)
    o_ref[...] = (x * lax.rsqrt(ms + EPS)) * g_ref[...]


def _final(x2, peer_t, g, first_row, n_rows, tile):
    first_tile = first_row // tile
    return pl.pallas_call(
        _final_body,
        grid=(n_rows // tile,),
        in_specs=[pl.BlockSpec((tile, D_MODEL), lambda i: (first_tile + i, 0)),
                  pl.BlockSpec((D_MODEL, tile), lambda i: (0, first_tile + i)),
                  pl.BlockSpec((1, D_MODEL), lambda i: (0, 0))],
        out_specs=pl.BlockSpec((tile, D_MODEL), lambda i: (i, 0)),
        out_shape=jax.ShapeDtypeStruct((n_rows, D_MODEL), F32),
        compiler_params=_params("parallel"),
        name="final_norm",
    )(x2, peer_t, g.reshape(1, D_MODEL))


def kernel(x_prompt, x_sample, mem_prompt, cache_mem_k, cache_mem_v, state_gla, g_mix, w_in, w_alpha, b_alpha, w_s, b_s, g_sgu, g_gla, w_out, g_mem, w_xk, w_xv, g_xq, w_xq, w_xo, g_ffn, w_pq, sub_keys, u_emb, v_emb, g_final):
    w_in_main = w_in[0, :, :Z_COLS].astype(BF16)
    w_in_gate = jnp.pad(w_in[0, :, Z_COLS:], ((0, 0), (0, LANES - GATE_RANK))).astype(BF16)
    wa_pad = jnp.pad(w_alpha[0], ((0, LANES - GATE_RANK), (0, 0))).astype(BF16)
    ba = b_alpha[0].reshape(1, B_KEY_WIDTH)
    causal = jnp.tril(jnp.ones((SGU_CHUNK, SGU_CHUNK), dtype=bool))
    ws_tril = jnp.where(causal, w_s[0], 0).astype(BF16)
    bs_t = b_s[0].T
    w00 = jnp.repeat(w_s[0, :, 0, 0], LANES).reshape(1, A_WIDTH)
    b00 = jnp.repeat(b_s[0, :, 0], LANES).reshape(1, A_WIDTH)
    gs = g_sgu[0].reshape(1, A_WIDTH)
    gg = g_gla[0].reshape(1, B_VAL_WIDTH)
    keys = sub_keys[0].reshape(2 * PEER_HEADS, N_KEYS, LANES).astype(BF16)
    vt_bf = _transpose_cast(v_emb[0])

    x_all = jnp.concatenate([x_prompt.reshape(T_PROMPT, D_MODEL), x_sample.reshape(T_SAMPLE, D_MODEL),
                             jnp.zeros((T_PAD - T_ALL, D_MODEL), F32)], axis=0)

    mem = mem_prompt.reshape(BATCH * MEM_LEN, D_MODEL)
    mk, mk_heads = _nmm(mem, g_mem[0], w_xk[0].astype(BF16), tm=512, tn=D_MODEL, heads=X_HEADS)
    mv, mv_heads = _nmm(mem, g_mem[0], w_xv[0].astype(BF16), tm=512, tn=D_MODEL, heads=X_HEADS)

    z, zg = _nmm(x_all, g_mix[0], w_in_main, tm=TOKEN_TILE // 2, tn=Z_COLS, w_aux=w_in_gate)
    mix, cv_p = _sgu_prompt(z, ws_tril, bs_t, gs, jnp.zeros((T_PAD, D_MODEL), F32))
    mix, st_p = _gla_prompt(z, zg, wa_pad, ba, gg, mix)
    mix, cv_s, st_s = _sample_mixers(z, zg, state_gla[0], wa_pad, ba, gs, gg, w00, b00, mix)
    x1 = _mmres(mix, w_out[0].astype(BF16), x_all, tm=TOKEN_TILE)

    qx, = _nmm(x1, g_xq[0], w_xq[0].astype(BF16), tm=TOKEN_TILE, tn=D_MODEL, out_dtype=BF16)
    att = _attn_prompt(qx, mk.reshape(BATCH, MEM_LEN, D_MODEL), mv.reshape(BATCH, MEM_LEN, D_MODEL),
                       jnp.zeros((T_PAD, D_MODEL), BF16))
    att_s = _attn_sample(qx[T_PROMPT:T_ALL].reshape(T_SAMPLE, X_HEADS, X_HEAD_DIM), cache_mem_k, cache_mem_v)
    att = lax.dynamic_update_slice(att, att_s.reshape(T_SAMPLE, D_MODEL), (T_PROMPT, 0))
    x2 = _mmres(att, w_xo[0].astype(BF16), x1, tm=TOKEN_TILE)

    pq, hn = _nmm(x2, g_ffn[0], w_pq[0].astype(BF16), tm=TOKEN_TILE, tn=D_MODEL, emit_xn=True, out_dtype=BF16)
    s0, s1, a, b, th = _peer_route(pq, keys)
    peer_t = _peer_dense(hn.T, u_emb[0], vt_bf, s0, s1, a, b, th)

    y_p = _final(x2, peer_t, g_final, 0, T_PROMPT, 2 * LANES)
    y_s = _final(x2, peer_t, g_final, T_PROMPT, T_SAMPLE, LANES)

    return (y_p.reshape(BATCH, SEQ, D_MODEL),
            y_s.reshape(DEC_BATCH, 1, D_MODEL),
            mk_heads.reshape(1, BATCH, MEM_LEN, X_HEADS, X_HEAD_DIM),
            mv_heads.reshape(1, BATCH, MEM_LEN, X_HEADS, X_HEAD_DIM),
            st_p.reshape(1, BATCH, B_HEADS, B_KEY_DIM, B_VAL_DIM),
            st_s.reshape(1, DEC_BATCH, B_HEADS, B_KEY_DIM, B_VAL_DIM),
            cv_p.reshape(1, BATCH, SGU_CHUNK, A_HEADS, LANES),
            cv_s.reshape(1, DEC_BATCH, 1, A_HEADS, LANES))
```
